```python
import math
import jax, jax.numpy as jnp
from jax import lax
import numpy as np


D_MODEL = 4096
BATCH = 4
SEQ = 4096
DEPTH = 2
DEC_BATCH = 16
DEC_SEQ = 16
PAST_LEN = 1024

CHUNK = 64
QBLOCK = 128
D_MIX = D_MODEL
D_SSM = D_MIX // 2
SSM_HEAD_DIM = 64
N_SSM_HEADS = D_SSM // SSM_HEAD_DIM
N_SSM_GROUPS = 4
D_STATE = 128
CONV_W = 4
D_CONV = D_SSM + 2 * N_SSM_GROUPS * D_STATE
D_DIFF = D_MIX // 4
N_DIFF_HEADS = 4
DIFF_HEAD_DIM = D_DIFF // (2 * N_DIFF_HEADS)
D_MEM = D_MIX - D_SSM - D_DIFF
N_MEM_HEADS = 4
MEM_HEAD_DIM = D_MEM // N_MEM_HEADS
N_MEM = 256
NORM_EPS = 1e-6
SUBLN_EPS = 1e-5
IN_SIZES = (D_SSM, D_CONV, N_SSM_HEADS, D_DIFF, D_DIFF, D_DIFF, D_DIFF, D_MEM, D_MEM)
IN_SPLITS = tuple(int(s) for s in np.cumsum(IN_SIZES)[:-1])
D_IN = sum(IN_SIZES)

kernel_name = 'hybrid_ssd_diffattn_memory_streaming_step'


def rmsnorm(x, w, eps=NORM_EPS):
    xf = x.astype(jnp.float32)
    y = xf * lax.rsqrt(jnp.mean(xf * xf, axis=-1, keepdims=True) + eps)
    return (y * w.astype(jnp.float32)).astype(x.dtype)


def causal_conv(xp, w, b, length):
    out = b
    for j in range(CONV_W):
        out = out + xp[:, j:j + length] * w[j]
    return out


def ssd_scan(x, dt, a, bm, cm, h0):
    bsz, length, n_heads, p_dim = x.shape
    g_n, n_st = bm.shape[2], bm.shape[3]
    r_n = n_heads // g_n
    q_len = min(CHUNK, length)
    nc = length // q_len
    xd = (x.astype(jnp.float32) * dt[..., None]).reshape(bsz, nc, q_len, g_n, r_n, p_dim)
    da = (dt * a).reshape(bsz, nc, q_len, g_n, r_n).transpose(0, 3, 4, 1, 2)
    bc = bm.reshape(bsz, nc, q_len, g_n, n_st)
    cc = cm.reshape(bsz, nc, q_len, g_n, n_st)
    cum = jnp.cumsum(da, axis=-1)
    causal = jnp.tril(jnp.ones((q_len, q_len), dtype=bool))
    seg = cum[..., :, None] - cum[..., None, :]
    lmat = jnp.exp(jnp.where(causal, seg, -jnp.inf))
    cb = jnp.einsum('bclgn,bcsgn->bgcls', cc, bc)
    y_diag = jnp.einsum('bgrcls,bcsgrp->bclgrp', cb[:, :, None] * lmat, xd)
    decay_st = jnp.exp(cum[..., -1:] - cum)
    states = jnp.einsum('bclgn,bgrcl,bclgrp->cbgrpn', bc, decay_st, xd)
    chunk_decay = jnp.exp(cum[..., -1]).transpose(3, 0, 1, 2)

    def step(h, inp):
        st, dc = inp
        return h * dc[..., None, None] + st, h

    h_init = h0.astype(jnp.float32).reshape(bsz, g_n, r_n, p_dim, n_st)
    h_last, h_prev = lax.scan(step, h_init, (states, chunk_decay))
    y_off = jnp.einsum('bclgn,cbgrpn,bgrcl->bclgrp', cc, h_prev, jnp.exp(cum))
    y = (y_diag + y_off).reshape(bsz, length, n_heads, p_dim)
    return y, h_last.reshape(bsz, n_heads, p_dim, n_st)


def diff_attention(q, k, v, q_start, lam, subln_w, lambda_init):
    bsz, lq = q.shape[0], q.shape[1]
    lk = k.shape[1]
    qb = min(QBLOCK, lq)
    nb = lq // qb
    k_chunk = jnp.arange(lk) // CHUNK
    scale = DIFF_HEAD_DIM ** -0.5

    def block(args):
        qi, start = args
        q_chunk = (q_start + start + jnp.arange(qb)) // CHUNK
        s = jnp.einsum('bqhmd,bkhmd->bhmqk', qi, k).astype(jnp.float32) * scale
        mask = k_chunk[None, :] <= q_chunk[:, None]
        s = jnp.where(mask, s, -jnp.inf)
        pr = jax.nn.softmax(s, axis=-1)
        wgt = pr[:, :, 0] - lam * pr[:, :, 1]
        return jnp.einsum('bhqk,bkhe->bqhe', wgt, v)

    qs = q.reshape(bsz, nb, qb, N_DIFF_HEADS, 2, DIFF_HEAD_DIM).transpose(1, 0, 2, 3, 4, 5)
    o = lax.map(block, (qs, jnp.arange(nb) * qb))
    o = o.transpose(1, 0, 2, 3, 4).reshape(bsz, lq, N_DIFF_HEADS, 2 * DIFF_HEAD_DIM)
    o = rmsnorm(o, subln_w, SUBLN_EPS) * (1.0 - lambda_init)
    return o.astype(q.dtype)


def mem_attention(q, mk, mv):
    s = jnp.einsum('bqhd,bkhd->bhqk', q, mk).astype(jnp.float32) * (MEM_HEAD_DIM ** -0.5)
    pr = jax.nn.softmax(s, axis=-1)
    return jnp.einsum('bhqk,bkhd->bqhd', pr, mv).astype(q.dtype)


def mixer_layer(x, conv_hist, h0, k_past, v_past, mem_k, mem_v, p, lambda_init):
    bsz, length, _ = x.shape
    h = rmsnorm(x, p['norm_pre_w'])
    proj = h @ p['w_in']
    z, xbc, dt_raw, q_d, k_d, v_d, g_d, q_m, g_m = jnp.split(proj, IN_SPLITS, axis=-1)
    xp = jnp.concatenate([conv_hist.astype(xbc.dtype), xbc], axis=1)
    new_conv = xp[:, length:]
    xbc_c = jax.nn.silu(causal_conv(xp, p['conv_w'], p['conv_b'], length))
    xs, bs, cs = jnp.split(xbc_c, [D_SSM, D_SSM + N_SSM_GROUPS * D_STATE], axis=-1)
    xs = xs.reshape(bsz, length, N_SSM_HEADS, SSM_HEAD_DIM)
    bs = bs.reshape(bsz, length, N_SSM_GROUPS, D_STATE)
    cs = cs.reshape(bsz, length, N_SSM_GROUPS, D_STATE)
    dt = jax.nn.softplus(dt_raw.astype(jnp.float32) + p['dt_bias'].astype(jnp.float32))
    a = -jnp.exp(p['a_log'].astype(jnp.float32))
    y_s, h_new = ssd_scan(xs, dt, a, bs, cs, h0)
    y_s = y_s + xs.astype(jnp.float32) * p['d_skip'].astype(jnp.float32)[:, None]
    y_s = y_s.reshape(bsz, length, D_SSM) * jax.nn.silu(z.astype(jnp.float32))
    y_s = rmsnorm(y_s.reshape(bsz, length, N_SSM_GROUPS, D_SSM // N_SSM_GROUPS),
                  p['ssm_norm_w'].reshape(N_SSM_GROUPS, D_SSM // N_SSM_GROUPS))
    y_s = y_s.reshape(bsz, length, D_SSM).astype(x.dtype)
    q = q_d.reshape(bsz, length, N_DIFF_HEADS, 2, DIFF_HEAD_DIM)
    k = k_d.reshape(bsz, length, N_DIFF_HEADS, 2, DIFF_HEAD_DIM)
    v = v_d.reshape(bsz, length, N_DIFF_HEADS, 2 * DIFF_HEAD_DIM)
    if k_past is None:
        keys, vals, q_start = k, v, 0
    else:
        keys = jnp.concatenate([k_past.astype(k.dtype), k], axis=1)
        vals = jnp.concatenate([v_past.astype(v.dtype), v], axis=1)
        q_start = k_past.shape[1]
    lam = (jnp.exp(jnp.sum(p['lambda_q1'].astype(jnp.float32) * p['lambda_k1'].astype(jnp.float32)))
           - jnp.exp(jnp.sum(p['lambda_q2'].astype(jnp.float32) * p['lambda_k2'].astype(jnp.float32)))
           + lambda_init)
    y_d = diff_attention(q, keys, vals, q_start, lam, p['subln_w'], lambda_init)
    y_d = y_d.reshape(bsz, length, D_DIFF) * jax.nn.silu(g_d)
    y_m = mem_attention(q_m.reshape(bsz, length, N_MEM_HEADS, MEM_HEAD_DIM), mem_k.astype(q_m.dtype), mem_v.astype(q_m.dtype))
    y_m = y_m.reshape(bsz, length, D_MEM) * jax.nn.silu(g_m)
    out = jnp.concatenate([y_s, y_d, y_m], axis=-1) @ p['w_out']
    x = x + rmsnorm(out, p['norm_post_w'])
    return x, new_conv, h_new, k, v


def setup_inputs(seed: int = 0) -> dict:
    key = jax.random.key(seed)
    ks = jax.random.split(key, 32)
    f32 = jnp.float32

    def nrm(k, shape, s):
        return jax.random.normal(k, shape, f32) * s

    dt0 = jnp.exp(jax.random.uniform(ks[14], (DEPTH, N_SSM_HEADS), f32, math.log(1e-3), math.log(1e-1)))
    return {
        'x_prompt': nrm(ks[0], (BATCH, SEQ, D_MODEL), 1.0),
        'x_sample': nrm(ks[1], (DEC_BATCH, DEC_SEQ, D_MODEL), 1.0),
        'mem_prompt': nrm(ks[2], (BATCH, N_MEM, D_MODEL), 1.0),
        'cache_conv': nrm(ks[3], (DEPTH, DEC_BATCH, CONV_W - 1, D_CONV), 1.0),
        'state_ssm': nrm(ks[4], (DEPTH, DEC_BATCH, N_SSM_HEADS, SSM_HEAD_DIM, D_STATE), 0.1),
        'cache_k': nrm(ks[5], (DEPTH, DEC_BATCH, PAST_LEN, N_DIFF_HEADS, 2, DIFF_HEAD_DIM), 1.0),
        'cache_v': nrm(ks[6], (DEPTH, DEC_BATCH, PAST_LEN, N_DIFF_HEADS, 2 * DIFF_HEAD_DIM), 1.0),
        'cache_mem_k': nrm(ks[7], (DEPTH, DEC_BATCH, N_MEM, N_MEM_HEADS, MEM_HEAD_DIM), 1.0),
        'cache_mem_v': nrm(ks[8], (DEPTH, DEC_BATCH, N_MEM, N_MEM_HEADS, MEM_HEAD_DIM), 1.0),
        'norm_pre_w': 1.0 + nrm(ks[9], (DEPTH, D_MODEL), 0.01),
        'norm_post_w': 1.0 + nrm(ks[10], (DEPTH, D_MODEL), 0.01),
        'w_in': nrm(ks[11], (DEPTH, D_MODEL, D_IN), D_MODEL ** -0.5),
        'conv_w': nrm(ks[12], (DEPTH, CONV_W, D_CONV), CONV_W ** -0.5),
        'conv_b': nrm(ks[13], (DEPTH, D_CONV), 0.01),
        'dt_bias': dt0 + jnp.log(-jnp.expm1(-dt0)),
        'a_log': jnp.log(jax.random.uniform(ks[15], (DEPTH, N_SSM_HEADS), f32, 1.0, 16.0)),
        'd_skip': 1.0 + nrm(ks[16], (DEPTH, N_SSM_HEADS), 0.01),
        'ssm_norm_w': 1.0 + nrm(ks[17], (DEPTH, D_SSM), 0.01),
        'lambda_q1': nrm(ks[18], (DEPTH, DIFF_HEAD_DIM), 0.1),
        'lambda_k1': nrm(ks[19], (DEPTH, DIFF_HEAD_DIM), 0.1),
        'lambda_q2': nrm(ks[20], (DEPTH, DIFF_HEAD_DIM), 0.1),
        'lambda_k2': nrm(ks[21], (DEPTH, DIFF_HEAD_DIM), 0.1),
        'subln_w': 1.0 + nrm(ks[22], (DEPTH, 2 * DIFF_HEAD_DIM), 0.01),
        'mem_norm_w': 1.0 + nrm(ks[23], (DEPTH, D_MODEL), 0.01),
        'w_mem_kv': nrm(ks[24], (DEPTH, D_MODEL, 2 * D_MEM), D_MODEL ** -0.5),
        'w_out': nrm(ks[25], (DEPTH, D_MIX, D_MODEL), D_MIX ** -0.5),
    }


def reference(x_prompt, x_sample, mem_prompt, cache_conv, state_ssm, cache_k, cache_v,
              cache_mem_k, cache_mem_v, norm_pre_w, norm_post_w, w_in, conv_w, conv_b,
              dt_bias, a_log, d_skip, ssm_norm_w, lambda_q1, lambda_k1, lambda_q2, lambda_k2,
              subln_w, mem_norm_w, w_mem_kv, w_out):
    bsz_p = x_prompt.shape[0]
    xp_ = x_prompt
    xs_ = x_sample
    p_conv, p_ssm, p_k, p_v, p_mk, p_mv = [], [], [], [], [], []
    s_conv, s_ssm, s_k, s_v = [], [], [], []
    for l in range(DEPTH):
        prm = {
            'norm_pre_w': norm_pre_w[l], 'norm_post_w': norm_post_w[l], 'w_in': w_in[l],
            'conv_w': conv_w[l], 'conv_b': conv_b[l], 'dt_bias': dt_bias[l], 'a_log': a_log[l],
            'd_skip': d_skip[l], 'ssm_norm_w': ssm_norm_w[l], 'lambda_q1': lambda_q1[l],
            'lambda_k1': lambda_k1[l], 'lambda_q2': lambda_q2[l], 'lambda_k2': lambda_k2[l],
            'subln_w': subln_w[l], 'w_out': w_out[l],
        }
        lambda_init = 0.8 - 0.6 * math.exp(-0.3 * l)
        mkv = rmsnorm(mem_prompt, mem_norm_w[l]) @ w_mem_kv[l]
        mk_p, mv_p = jnp.split(mkv, 2, axis=-1)
        mk_p = mk_p.reshape(bsz_p, N_MEM, N_MEM_HEADS, MEM_HEAD_DIM)
        mv_p = mv_p.reshape(bsz_p, N_MEM, N_MEM_HEADS, MEM_HEAD_DIM)
        conv0 = jnp.zeros((bsz_p, CONV_W - 1, D_CONV), x_prompt.dtype)
        h_zero = jnp.zeros((bsz_p, N_SSM_HEADS, SSM_HEAD_DIM, D_STATE), jnp.float32)
        xp_, c_p, h_p, k_p, v_p = mixer_layer(xp_, conv0, h_zero, None, None, mk_p, mv_p, prm, lambda_init)
        xs_, c_s, h_s, k_s, v_s = mixer_layer(xs_, cache_conv[l], state_ssm[l], cache_k[l], cache_v[l],
                                             cache_mem_k[l], cache_mem_v[l], prm, lambda_init)
        p_conv.append(c_p); p_ssm.append(h_p); p_k.append(k_p); p_v.append(v_p)
        p_mk.append(mk_p); p_mv.append(mv_p)
        s_conv.append(c_s); s_ssm.append(h_s); s_k.append(k_s); s_v.append(v_s)
    return (xp_, xs_, jnp.stack(p_conv), jnp.stack(p_ssm), jnp.stack(p_k), jnp.stack(p_v),
            jnp.stack(p_mk), jnp.stack(p_mv), jnp.stack(s_conv), jnp.stack(s_ssm),
            jnp.stack(s_k), jnp.stack(s_v))
```

```python
import functools
import math

import jax
import jax.numpy as jnp
from jax import lax
from jax.experimental import pallas as pl
from jax.experimental.pallas import tpu as pltpu

F32 = jnp.float32
BF16 = jnp.bfloat16

D_MODEL = 4096
D_SSM = 2048
SSM_HEAD_DIM = 64
N_SSM_HEADS = 32
N_SSM_GROUPS = 4
HEADS_PER_GROUP = N_SSM_HEADS // N_SSM_GROUPS
D_GROUP = D_SSM // N_SSM_GROUPS
D_STATE = 128
CONV_W = 4
D_BC = N_SSM_GROUPS * D_STATE
D_CONV = D_SSM + 2 * D_BC
D_DIFF = 1024
N_DIFF_HEADS = 4
DIFF_HEAD_DIM = 128
D_MEM = 1024
N_MEM_HEADS = 4
MEM_HEAD_DIM = 256
CHUNK = 64
NORM_EPS = 1e-6
SUBLN_EPS = 1e-5

LANE = 128
SUBLANE = 8
VMEM_LIMIT = 56 * 1024 * 1024

HEAD_W = 2 * DIFF_HEAD_DIM
COL_Z = 0
COL_X = D_SSM
COL_B = COL_X + D_SSM
COL_C = COL_B + D_BC
COL_QD = COL_C + D_BC
COL_GD = COL_QD + D_DIFF
COL_QM = COL_GD + D_DIFF
COL_GM = COL_QM + D_MEM
COL_KD = COL_GM + D_MEM
COL_VD = COL_KD + D_DIFF
D_MAIN = COL_VD + D_DIFF
PROJ_TN = 1024
K_TILE = COL_KD // PROJ_TN
V_TILE = COL_VD // PROJ_TN


def _params(n_axes):
    return pltpu.CompilerParams(dimension_semantics=("arbitrary",) * n_axes,
                                vmem_limit_bytes=VMEM_LIMIT)


def _silu(x):
    return x * jax.nn.sigmoid(x)


def _chunk_of(pos):
    assert CHUNK & (CHUNK - 1) == 0
    return lax.shift_right_logical(pos, jnp.int32(CHUNK.bit_length() - 1))


def _rms_scale(x, eps):
    return lax.rsqrt(jnp.mean(x * x, axis=-1, keepdims=True) + eps)


NORM_ROWS = 128


def _norm_rows(x_ref, nw_ref, h_scr):
    tm = x_ref.shape[0]
    step = min(NORM_ROWS, tm)
    for r in range(0, tm, step):
        x = x_ref[r:r + step, :]
        h_scr[r:r + step, :] = (x * _rms_scale(x, NORM_EPS) * nw_ref[...]).astype(BF16)


def _in_proj_kernel(x_ref, nw_ref, w_ref, wdt_ref, main_ref, k_ref, v_ref, dt_ref, h_scr):
    j = pl.program_id(1)

    @pl.when(j == 0)
    def _():
        _norm_rows(x_ref, nw_ref, h_scr)
        dt_ref[...] = jnp.dot(h_scr[...], wdt_ref[...], preferred_element_type=F32)

    acc = jnp.dot(h_scr[...], w_ref[...], preferred_element_type=F32)
    main_ref[...] = acc.astype(BF16)

    @pl.when(j == K_TILE)
    def _():
        k_ref[...] = acc

    @pl.when(j == V_TILE)
    def _():
        v_ref[...] = acc


def _in_proj(x2d, norm_w, w_main, w_dt, tm):
    m = x2d.shape[0]
    grid = (m // tm, D_MAIN // PROJ_TN)
    return pl.pallas_call(
        _in_proj_kernel,
        grid=grid,
        in_specs=[
            pl.BlockSpec((tm, D_MODEL), lambda i, j: (i, 0)),
            pl.BlockSpec((1, D_MODEL), lambda i, j: (0, 0)),
            pl.BlockSpec((D_MODEL, PROJ_TN), lambda i, j: (0, j)),
            pl.BlockSpec((D_MODEL, LANE), lambda i, j: (0, 0)),
        ],
        out_specs=[
            pl.BlockSpec((tm, PROJ_TN), lambda i, j: (i, j)),
            pl.BlockSpec((tm, D_DIFF), lambda i, j: (i, 0)),
            pl.BlockSpec((tm, D_DIFF), lambda i, j: (i, 0)),
            pl.BlockSpec((tm, LANE), lambda i, j: (i, 0)),
        ],
        out_shape=[
            jax.ShapeDtypeStruct((m, D_MAIN), BF16),
            jax.ShapeDtypeStruct((m, D_DIFF), F32),
            jax.ShapeDtypeStruct((m, D_DIFF), F32),
            jax.ShapeDtypeStruct((m, LANE), F32),
        ],
        scratch_shapes=[pltpu.VMEM((tm, D_MODEL), BF16)],
        compiler_params=_params(2),
        name="in_proj",
    )(x2d, norm_w, w_main, w_dt)


def _mem_kv_kernel(x_ref, nw_ref, w_ref, mk_ref, mv_ref, h_scr):
    j = pl.program_id(1)

    @pl.when(j == 0)
    def _():
        _norm_rows(x_ref, nw_ref, h_scr)

    acc = jnp.dot(h_scr[...], w_ref[...], preferred_element_type=F32)

    @pl.when(j == 0)
    def _():
        mk_ref[...] = acc

    @pl.when(j == 1)
    def _():
        mv_ref[...] = acc


def _mem_kv(mem2d, norm_w, w_kv, tm):
    m = mem2d.shape[0]
    return pl.pallas_call(
        _mem_kv_kernel,
        grid=(m // tm, 2),
        in_specs=[
            pl.BlockSpec((tm, D_MODEL), lambda i, j: (i, 0)),
            pl.BlockSpec((1, D_MODEL), lambda i, j: (0, 0)),
            pl.BlockSpec((D_MODEL, D_MEM), lambda i, j: (0, j)),
        ],
        out_specs=[
            pl.BlockSpec((tm, D_MEM), lambda i, j: (i, 0)),
            pl.BlockSpec((tm, D_MEM), lambda i, j: (i, 0)),
        ],
        out_shape=[jax.ShapeDtypeStruct((m, D_MEM), F32)] * 2,
        scratch_shapes=[pltpu.VMEM((tm, D_MODEL), BF16)],
        compiler_params=_params(2),
        name="mem_kv",
    )(mem2d, norm_w, w_kv)


def _split_bf16(x, n):
    parts = []
    r = x
    for _ in range(n - 1):
        p = r.astype(BF16)
        parts.append(p)
        r = r - p.astype(F32)
    parts.append(r.astype(BF16))
    return parts


def _sum_rows(a, n, rows):
    out = a[(n - 1) * rows:n * rows]
    for i in range(n - 2, -1, -1):
        out = out + a[i * rows:(i + 1) * rows]
    return out


def _ssd_kernel(z_ref, xx_ref, xb_ref, xc_ref, dt_ref, hist_ref, h0_ref,
                cw_ref, cb_ref, dtb_ref, alog_ref, dskip_ref, nw_ref, e_ref,
                y_ref, nconv_ref, hnew_ref, xp_scr, st_scr, y_scr, *, q):
    c = pl.program_id(1)
    nc = pl.num_programs(1)
    hist_row = SUBLANE - (CONV_W - 1)

    @pl.when(c == 0)
    def _():
        xp_scr[hist_row:SUBLANE, :] = hist_ref[0]
        for g in range(N_SSM_GROUPS):
            hg = h0_ref[0, g * HEADS_PER_GROUP:(g + 1) * HEADS_PER_GROUP]
            st_scr[g] = hg.reshape(D_GROUP, D_STATE).T

    xp_scr[SUBLANE:SUBLANE + q, 0:D_SSM] = xx_ref[...].astype(F32)
    xp_scr[SUBLANE:SUBLANE + q, D_SSM:D_SSM + D_BC] = xb_ref[...].astype(F32)
    xp_scr[SUBLANE:SUBLANE + q, D_SSM + D_BC:D_CONV] = xc_ref[...].astype(F32)
    acc = cb_ref[...] + xp_scr[hist_row:hist_row + q, :] * cw_ref[0:1, :]
    for j in range(1, CONV_W):
        acc = acc + xp_scr[hist_row + j:hist_row + j + q, :] * cw_ref[j:j + 1, :]
    xbc = _silu(acc)
    xs = xbc[:, 0:D_SSM]
    xs_bf = xs.astype(BF16)
    bm = xbc[:, D_SSM:D_SSM + D_BC].astype(BF16)
    cm = xbc[:, D_SSM + D_BC:D_CONV].astype(BF16)

    @pl.when(c == nc - 1)
    def _():
        nconv_ref[0] = xp_scr[q + hist_row:q + SUBLANE, :]

    xp_scr[0:SUBLANE, :] = xp_scr[q:q + SUBLANE, :]

    dtv = jax.nn.softplus(dt_ref[...] + dtb_ref[...])
    a = -jnp.exp(alog_ref[...])
    da = dtv * a
    ri = lax.broadcasted_iota(jnp.int32, (q, q), 0)
    ci = lax.broadcasted_iota(jnp.int32, (q, q), 1)
    causal = ri >= ci
    tril = jnp.where(causal, 1.0, 0.0).astype(BF16)
    triu = jnp.where(ri <= ci, 1.0, 0.0).astype(BF16)
    eye = jnp.where(ri == ci, 1.0, 0.0).astype(BF16)
    da3 = jnp.concatenate(_split_bf16(da, 3), axis=1)
    dt3 = jnp.concatenate(_split_bf16(dtv, 3), axis=1)
    cum3 = jnp.dot(tril, da3, preferred_element_type=F32)
    cum = cum3[:, 2 * LANE:3 * LANE] + cum3[:, LANE:2 * LANE] + cum3[:, 0:LANE]
    tn_dims = (((0,), (0,)), ((), ()))
    cum_t = _sum_rows(lax.dot_general(da3, triu, tn_dims, preferred_element_type=F32), 3, LANE)
    dt_t = _sum_rows(lax.dot_general(dt3, eye, tn_dims, preferred_element_type=F32), 3, LANE)

    ecum = jnp.exp(cum)
    dd = dtv * jnp.exp(cum[q - 1:q, :] - cum)
    ex_in = jnp.concatenate(_split_bf16(ecum, 2) + _split_bf16(dd, 2), axis=0)
    ex = jnp.dot(ex_in, e_ref[...], preferred_element_type=F32)
    ecum_e = ex[q:2 * q] + ex[0:q]
    dd_e = ex[3 * q:4 * q] + ex[2 * q:3 * q]
    xdd = (xs * dd_e).astype(BF16)

    lane = lax.broadcasted_iota(jnp.int32, (q, LANE), 1)
    lo_half = lane < SSM_HEAD_DIM
    nt_dims = (((1,), (1,)), ((), ()))
    for g in range(N_SSM_GROUPS):
        bg = bm[:, g * D_STATE:(g + 1) * D_STATE]
        cg = cm[:, g * D_STATE:(g + 1) * D_STATE]
        gs = slice(g * D_GROUP, (g + 1) * D_GROUP)
        cbg = lax.dot_general(cg, bg, nt_dims, preferred_element_type=F32)
        st = st_scr[g]
        y_off = jnp.dot(cg, st.astype(BF16), preferred_element_type=F32) * ecum_e[:, gs]
        st_scr[g] = st * ecum_e[q - 1:q, gs] + lax.dot_general(
            bg, xdd[:, gs], tn_dims, preferred_element_type=F32)
        for t in range(HEADS_PER_GROUP // 2):
            ws = []
            for h in (g * HEADS_PER_GROUP + 2 * t, g * HEADS_PER_GROUP + 2 * t + 1):
                seg = cum[:, h:h + 1] - cum_t[h:h + 1, :]
                lmat = jnp.exp(jnp.where(causal, seg, -jnp.inf))
                ws.append((cbg * lmat * dt_t[h:h + 1, :]).astype(BF16))
            cs = slice(g * D_GROUP + t * LANE, g * D_GROUP + (t + 1) * LANE)
            xpair = xs_bf[:, cs]
            zero = jnp.zeros_like(xpair)
            y_diag = (jnp.dot(ws[0], jnp.where(lo_half, xpair, zero), preferred_element_type=F32)
                      + jnp.dot(ws[1], jnp.where(lo_half, zero, xpair), preferred_element_type=F32))
            y_scr[:, cs] = (y_diag + y_off[:, t * LANE:(t + 1) * LANE]) + xs[:, cs] * dskip_ref[:, cs]

    for g in range(N_SSM_GROUPS):
        gs = slice(g * D_GROUP, (g + 1) * D_GROUP)
        yg = y_scr[:, gs] * _silu(z_ref[:, gs].astype(F32))
        y_ref[:, gs] = (yg * _rms_scale(yg, NORM_EPS) * nw_ref[:, gs]).astype(BF16)

    @pl.when(c == nc - 1)
    def _():
        for g in range(N_SSM_GROUPS):
            hnew_ref[0, g * HEADS_PER_GROUP:(g + 1) * HEADS_PER_GROUP] = (
                st_scr[g].T.reshape(HEADS_PER_GROUP, SSM_HEAD_DIM, D_STATE))


def _ssd(main, dt, conv_hist, h0, b_off, lp, bsz, length, q):
    nc = length // q
    m = bsz * length
    row = lambda b, c: b * nc + c
    return pl.pallas_call(
        functools.partial(_ssd_kernel, q=q),
        grid=(bsz, nc),
        in_specs=[
            pl.BlockSpec((q, D_SSM), lambda b, c: (row(b, c), COL_Z // D_SSM)),
            pl.BlockSpec((q, D_SSM), lambda b, c: (row(b, c), COL_X // D_SSM)),
            pl.BlockSpec((q, D_BC), lambda b, c: (row(b, c), COL_B // D_BC)),
            pl.BlockSpec((q, D_BC), lambda b, c: (row(b, c), COL_C // D_BC)),
            pl.BlockSpec((q, LANE), lambda b, c: (row(b, c), 0)),
            pl.BlockSpec((1, CONV_W - 1, D_CONV), lambda b, c: (b_off + b, 0, 0)),
            pl.BlockSpec((1, N_SSM_HEADS, SSM_HEAD_DIM, D_STATE), lambda b, c: (b_off + b, 0, 0, 0)),
            pl.BlockSpec((CONV_W, D_CONV), lambda b, c: (0, 0)),
            pl.BlockSpec((1, D_CONV), lambda b, c: (0, 0)),
            pl.BlockSpec((1, LANE), lambda b, c: (0, 0)),
            pl.BlockSpec((1, LANE), lambda b, c: (0, 0)),
            pl.BlockSpec((1, D_SSM), lambda b, c: (0, 0)),
            pl.BlockSpec((1, D_SSM), lambda b, c: (0, 0)),
            pl.BlockSpec((LANE, D_SSM), lambda b, c: (0, 0)),
        ],
        out_specs=[
            pl.BlockSpec((q, D_SSM), lambda b, c: (row(b, c), 0)),
            pl.BlockSpec((1, CONV_W - 1, D_CONV), lambda b, c: (b, 0, 0)),
            pl.BlockSpec((1, N_SSM_HEADS, SSM_HEAD_DIM, D_STATE), lambda b, c: (b, 0, 0, 0)),
        ],
        out_shape=[
            jax.ShapeDtypeStruct((m, D_SSM), BF16),
            jax.ShapeDtypeStruct((bsz, CONV_W - 1, D_CONV), F32),
            jax.ShapeDtypeStruct((bsz, N_SSM_HEADS, SSM_HEAD_DIM, D_STATE), F32),
        ],
        scratch_shapes=[
            pltpu.VMEM((q + SUBLANE, D_CONV), F32),
            pltpu.VMEM((N_SSM_GROUPS, D_STATE, D_GROUP), F32),
            pltpu.VMEM((q, D_SSM), F32),
        ],
        compiler_params=_params(2),
        name="ssd",
    )(main, main, main, main, dt, conv_hist, h0,
      lp["conv_w"], lp["conv_b"], lp["dt_bias"], lp["a_log"], lp["d_skip_e"], lp["ssm_norm_w"],
      lp["expand"])


def _lambda(lq1, lk1, lq2, lk2, lambda_init):
    return (jnp.exp(jnp.sum(lq1[...] * lk1[...], axis=-1, keepdims=True))
            - jnp.exp(jnp.sum(lq2[...] * lk2[...], axis=-1, keepdims=True)) + lambda_init)


def _diff_finish(o1, o2, lam, sw_ref, g_ref, lambda_init):
    o = o1 - lam * o2
    on = (o * _rms_scale(o, SUBLN_EPS) * sw_ref[...]) * (1.0 - lambda_init)
    return (on * _silu(g_ref[...].astype(F32))).astype(BF16)


def _diff_attn_kernel(q_ref, k_ref, v_ref, g_ref, lq1, lk1, lq2, lk2, sw_ref, o_ref,
                      m_scr, l_scr, acc_scr, *, tq, tk, lambda_init):
    qi = pl.program_id(2)
    ki = pl.program_id(3)
    scale = DIFF_HEAD_DIM ** -0.5

    @pl.when(ki == 0)
    def _():
        m_scr[...] = jnp.full(m_scr.shape, -jnp.inf, F32)
        l_scr[...] = jnp.zeros(l_scr.shape, F32)
        acc_scr[...] = jnp.zeros(acc_scr.shape, F32)

    @pl.when(ki <= qi)
    def _():
        mask = (_chunk_of(ki * tk + lax.broadcasted_iota(jnp.int32, (tq, tk), 1))
                <= _chunk_of(qi * tq + lax.broadcasted_iota(jnp.int32, (tq, tk), 0)))
        v = v_ref[...]
        for mp in range(2):
            cols = slice(mp * DIFF_HEAD_DIM, (mp + 1) * DIFF_HEAD_DIM)
            s = lax.dot_general(q_ref[:, cols], k_ref[:, cols], (((1,), (1,)), ((), ())),
                                preferred_element_type=F32) * scale
            s = jnp.where(mask, s, -jnp.inf)
            m_prev = m_scr[mp]
            m_new = jnp.maximum(m_prev, jnp.max(s, axis=-1, keepdims=True))
            alpha = jnp.exp(m_prev - m_new)
            p = jnp.exp(s - m_new)
            l_scr[mp] = alpha * l_scr[mp] + jnp.sum(p, axis=-1, keepdims=True)
            acc_scr[mp] = alpha * acc_scr[mp] + jnp.dot(p.astype(BF16), v, preferred_element_type=F32)
            m_scr[mp] = m_new

    @pl.when(ki == qi)
    def _():
        lam = _lambda(lq1, lk1, lq2, lk2, lambda_init)
        o_ref[...] = _diff_finish(acc_scr[0] / l_scr[0], acc_scr[1] / l_scr[1], lam, sw_ref, g_ref,
                                  lambda_init)


def _diff_attn_prompt(main, lp, bsz, length, lambda_init, tq):
    tk = tq
    nq = length // tq
    m = bsz * length
    vec = pl.BlockSpec((1, DIFF_HEAD_DIM), lambda b, h, qi, ki: (0, 0))
    return pl.pallas_call(
        functools.partial(_diff_attn_kernel, tq=tq, tk=tk, lambda_init=lambda_init),
        grid=(bsz, N_DIFF_HEADS, nq, nq),
        in_specs=[
            pl.BlockSpec((tq, HEAD_W), lambda b, h, qi, ki: (b * nq + qi, COL_QD // HEAD_W + h)),
            pl.BlockSpec((tk, HEAD_W), lambda b, h, qi, ki: (b * nq + jnp.minimum(ki, qi), COL_KD // HEAD_W + h)),
            pl.BlockSpec((tk, HEAD_W), lambda b, h, qi, ki: (b * nq + jnp.minimum(ki, qi), COL_VD // HEAD_W + h)),
            pl.BlockSpec((tq, HEAD_W), lambda b, h, qi, ki: (b * nq + qi, COL_GD // HEAD_W + h)),
            vec, vec, vec, vec,
            pl.BlockSpec((1, HEAD_W), lambda b, h, qi, ki: (0, 0)),
        ],
        out_specs=pl.BlockSpec((tq, HEAD_W), lambda b, h, qi, ki: (b * nq + qi, h)),
        out_shape=jax.ShapeDtypeStruct((m, D_DIFF), BF16),
        scratch_shapes=[
            pltpu.VMEM((2, tq, 1), F32),
            pltpu.VMEM((2, tq, 1), F32),
            pltpu.VMEM((2, tq, HEAD_W), F32),
        ],
        compiler_params=_params(4),
        name="diff_attn_prompt",
    )(main, main, main, main, lp["lambda_q1"], lp["lambda_k1"], lp["lambda_q2"], lp["lambda_k2"],
      lp["subln_w"])


def _diff_attn_decode_kernel(q_ref, kn_ref, vn_ref, g_ref, kp_ref, vp_ref, lq1, lk1, lq2, lk2, sw_ref,
                             o_ref, *, past, lambda_init):
    scale = DIFF_HEAD_DIM ** -0.5
    lq = q_ref.shape[0]
    nt_dims = (((1,), (1,)), ((), ()))
    mask_p = (_chunk_of(lax.broadcasted_iota(jnp.int32, (lq, past), 1))
              <= _chunk_of(past + lax.broadcasted_iota(jnp.int32, (lq, past), 0)))
    mask_n = (_chunk_of(past + lax.broadcasted_iota(jnp.int32, (lq, lq), 1))
              <= _chunk_of(past + lax.broadcasted_iota(jnp.int32, (lq, lq), 0)))
    vp = vp_ref[0].astype(BF16)
    vn = vn_ref[...]
    outs = []
    for mp in range(2):
        cols = slice(mp * DIFF_HEAD_DIM, (mp + 1) * DIFF_HEAD_DIM)
        qm = q_ref[:, cols]
        sp = lax.dot_general(qm, kp_ref[0, :, cols].astype(BF16), nt_dims,
                             preferred_element_type=F32) * scale
        sn = lax.dot_general(qm, kn_ref[:, cols], nt_dims, preferred_element_type=F32) * scale
        sp = jnp.where(mask_p, sp, -jnp.inf)
        sn = jnp.where(mask_n, sn, -jnp.inf)
        mx = jnp.maximum(jnp.max(sp, axis=-1, keepdims=True), jnp.max(sn, axis=-1, keepdims=True))
        pp = jnp.exp(sp - mx)
        pn = jnp.exp(sn - mx)
        denom = jnp.sum(pp, axis=-1, keepdims=True) + jnp.sum(pn, axis=-1, keepdims=True)
        o = (jnp.dot(pp.astype(BF16), vp, preferred_element_type=F32)
             + jnp.dot(pn.astype(BF16), vn, preferred_element_type=F32))
        outs.append(o / denom)
    lam = _lambda(lq1, lk1, lq2, lk2, lambda_init)
    o_ref[...] = _diff_finish(outs[0], outs[1], lam, sw_ref, g_ref, lambda_init)


def _diff_attn_decode(main, k_past, v_past, b_off, lp, bsz, length, lambda_init):
    past = k_past.shape[1]
    m = bsz * length
    vec = pl.BlockSpec((1, DIFF_HEAD_DIM), lambda b, h: (0, 0))
    return pl.pallas_call(
        functools.partial(_diff_attn_decode_kernel, past=past, lambda_init=lambda_init),
        grid=(bsz, N_DIFF_HEADS),
        in_specs=[
            pl.BlockSpec((length, HEAD_W), lambda b, h: (b, COL_QD // HEAD_W + h)),
            pl.BlockSpec((length, HEAD_W), lambda b, h: (b, COL_KD // HEAD_W + h)),
            pl.BlockSpec((length, HEAD_W), lambda b, h: (b, COL_VD // HEAD_W + h)),
            pl.BlockSpec((length, HEAD_W), lambda b, h: (b, COL_GD // HEAD_W + h)),
            pl.BlockSpec((1, past, HEAD_W), lambda b, h: (b_off + b, 0, h)),
            pl.BlockSpec((1, past, HEAD_W), lambda b, h: (b_off + b, 0, h)),
            vec, vec, vec, vec,
            pl.BlockSpec((1, HEAD_W), lambda b, h: (0, 0)),
        ],
        out_specs=pl.BlockSpec((length, HEAD_W), lambda b, h: (b, h)),
        out_shape=jax.ShapeDtypeStruct((m, D_DIFF), BF16),
        compiler_params=_params(2),
        name="diff_attn_decode",
    )(main, main, main, main, k_past, v_past,
      lp["lambda_q1"], lp["lambda_k1"], lp["lambda_q2"], lp["lambda_k2"], lp["subln_w"])


def _mem_attn_kernel(q_ref, g_ref, mk_ref, mv_ref, o_ref):
    scale = MEM_HEAD_DIM ** -0.5
    s = lax.dot_general(q_ref[...], mk_ref[0].astype(BF16), (((1,), (1,)), ((), ())),
                        preferred_element_type=F32) * scale
    p = jnp.exp(s - jnp.max(s, axis=-1, keepdims=True))
    pr = p / jnp.sum(p, axis=-1, keepdims=True)
    o = jnp.dot(pr.astype(BF16), mv_ref[0].astype(BF16), preferred_element_type=F32)
    o_ref[...] = (o * _silu(g_ref[...].astype(F32))).astype(BF16)


def _mem_attn(main, mk, mv, b_off, bsz, length, tq):
    nq = length // tq
    m = bsz * length
    n_mem = mk.shape[1]
    return pl.pallas_call(
        _mem_attn_kernel,
        grid=(bsz, N_MEM_HEADS, nq),
        in_specs=[
            pl.BlockSpec((tq, HEAD_W), lambda b, h, qi: (b * nq + qi, COL_QM // HEAD_W + h)),
            pl.BlockSpec((tq, HEAD_W), lambda b, h, qi: (b * nq + qi, COL_GM // HEAD_W + h)),
            pl.BlockSpec((1, n_mem, MEM_HEAD_DIM), lambda b, h, qi: (b_off + b, 0, h)),
            pl.BlockSpec((1, n_mem, MEM_HEAD_DIM), lambda b, h, qi: (b_off + b, 0, h)),
        ],
        out_specs=pl.BlockSpec((tq, HEAD_W), lambda b, h, qi: (b * nq + qi, h)),
        out_shape=jax.ShapeDtypeStruct((m, D_MEM), BF16),
        compiler_params=_params(3),
        name="mem_attn",
    )(main, main, mk, mv)


def _out_proj_kernel(ys_ref, yd_ref, ym_ref, w_ref, x_ref, nw_ref, o_ref, acc_scr, *, nj, tn):
    j = pl.program_id(1)
    acc_scr[j] = (jnp.dot(ys_ref[...], w_ref[0:D_SSM, :], preferred_element_type=F32)
                  + jnp.dot(yd_ref[...], w_ref[D_SSM:D_SSM + D_DIFF, :], preferred_element_type=F32)
                  + jnp.dot(ym_ref[...], w_ref[D_SSM + D_DIFF:D_MODEL, :], preferred_element_type=F32))

    @pl.when(j == nj - 1)
    def _():
        ss = jnp.sum(acc_scr[0] * acc_scr[0], axis=-1, keepdims=True)
        for t in range(1, nj):
            ss = ss + jnp.sum(acc_scr[t] * acc_scr[t], axis=-1, keepdims=True)
        inv = lax.rsqrt(ss / D_MODEL + NORM_EPS)
        for t in range(nj):
            cs = slice(t * tn, (t + 1) * tn)
            o_ref[:, cs] = x_ref[:, cs] + acc_scr[t] * inv * nw_ref[:, cs]


def _out_proj(ys, yd, ym, w_out, x2d, norm_w, tm, tn):
    m = x2d.shape[0]
    nj = D_MODEL // tn
    return pl.pallas_call(
        functools.partial(_out_proj_kernel, nj=nj, tn=tn),
        grid=(m // tm, nj),
        in_specs=[
            pl.BlockSpec((tm, D_SSM), lambda i, j: (i, 0)),
            pl.BlockSpec((tm, D_DIFF), lambda i, j: (i, 0)),
            pl.BlockSpec((tm, D_MEM), lambda i, j: (i, 0)),
            pl.BlockSpec((D_MODEL, tn), lambda i, j: (0, j)),
            pl.BlockSpec((tm, D_MODEL), lambda i, j: (i, 0)),
            pl.BlockSpec((1, D_MODEL), lambda i, j: (0, 0)),
        ],
        out_specs=pl.BlockSpec((tm, D_MODEL), lambda i, j: (i, 0)),
        out_shape=jax.ShapeDtypeStruct((m, D_MODEL), F32),
        scratch_shapes=[pltpu.VMEM((nj, tm, tn), F32)],
        compiler_params=_params(2),
        name="out_proj",
    )(ys, yd, ym, w_out, x2d, norm_w)


def _mixer_layer(x2d, bsz, length, b_off, mem_off, conv_hist, h0, kv_past, mk, mv, lp, lambda_init,
                 tiles):
    main, k32, v32, dt = _in_proj(x2d, lp["norm_pre_w"], lp["w_main"], lp["w_dt"], tiles["proj_tm"])
    ys, new_conv, h_new = _ssd(main, dt, conv_hist, h0, b_off, lp, bsz, length, tiles["ssd_q"])
    if kv_past is None:
        yd = _diff_attn_prompt(main, lp, bsz, length, lambda_init, tiles["attn_tq"])
    else:
        yd = _diff_attn_decode(main, kv_past[0], kv_past[1], b_off, lp, bsz, length, lambda_init)
    ym = _mem_attn(main, mk, mv, mem_off, bsz, length, tiles["mem_tq"])
    x_new = _out_proj(ys, yd, ym, lp["w_out"], x2d, lp["norm_post_w"], tiles["out_tm"], tiles["out_tn"])
    return x_new, new_conv, h_new, k32, v32


def _layer_params(l, norm_pre_w, norm_post_w, w_in, conv_w, conv_b, dt_bias, a_log, d_skip,
                  ssm_norm_w, lambda_q1, lambda_k1, lambda_q2, lambda_k2, subln_w, mem_norm_w,
                  w_mem_kv, w_out):
    w = w_in[l]
    o_dt = D_SSM + D_CONV
    o_q = o_dt + N_SSM_HEADS
    seg = lambda i: w[:, o_q + i * D_DIFF:o_q + (i + 1) * D_DIFF]
    w_main = jnp.concatenate([w[:, :o_dt], seg(0), seg(3), seg(4), seg(5), seg(1), seg(2)],
                             axis=1).astype(BF16)
    w_dt = jnp.pad(w[:, o_dt:o_q], ((0, 0), (0, LANE - N_SSM_HEADS))).astype(BF16)
    pad_heads = lambda v: jnp.pad(v, (0, LANE - N_SSM_HEADS)).reshape(1, LANE)
    head_of_channel = jnp.arange(D_SSM) // SSM_HEAD_DIM
    expand = (jnp.arange(LANE)[:, None] == head_of_channel[None, :]).astype(BF16)
    return {
        "norm_pre_w": norm_pre_w[l].reshape(1, D_MODEL),
        "norm_post_w": norm_post_w[l].reshape(1, D_MODEL),
        "w_main": w_main,
        "w_dt": w_dt,
        "conv_w": conv_w[l],
        "conv_b": conv_b[l].reshape(1, D_CONV),
        "dt_bias": pad_heads(dt_bias[l]),
        "a_log": pad_heads(a_log[l]),
        "d_skip_e": jnp.repeat(d_skip[l], SSM_HEAD_DIM).reshape(1, D_SSM),
        "ssm_norm_w": ssm_norm_w[l].reshape(1, D_SSM),
        "expand": expand,
        "lambda_q1": lambda_q1[l].reshape(1, DIFF_HEAD_DIM),
        "lambda_k1": lambda_k1[l].reshape(1, DIFF_HEAD_DIM),
        "lambda_q2": lambda_q2[l].reshape(1, DIFF_HEAD_DIM),
        "lambda_k2": lambda_k2[l].reshape(1, DIFF_HEAD_DIM),
        "subln_w": subln_w[l].reshape(1, HEAD_W),
        "mem_norm_w": mem_norm_w[l].reshape(1, D_MODEL),
        "w_mem_kv": w_mem_kv[l].astype(BF16),
        "w_out": w_out[l].astype(BF16),
    }


def _tiles(bsz, length):
    m = bsz * length
    return {
        "proj_tm": min(512, m),
        "ssd_q": min(128, length),
        "attn_tq": min(1024, length),
        "mem_tq": min(1024, length),
        "out_tm": min(256, m),
        "out_tn": 1024,
    }


def kernel(x_prompt, x_sample, mem_prompt, cache_conv, state_ssm, cache_k, cache_v, cache_mem_k,
           cache_mem_v, norm_pre_w, norm_post_w, w_in, conv_w, conv_b, dt_bias, a_log, d_skip,
           ssm_norm_w, lambda_q1, lambda_k1, lambda_q2, lambda_k2, subln_w, mem_norm_w, w_mem_kv,
           w_out):
    depth = w_in.shape[0]
    bp, lp_, _ = x_prompt.shape
    bs, ls, _ = x_sample.shape
    n_mem = mem_prompt.shape[1]
    past = cache_k.shape[2]
    xp = x_prompt.reshape(bp * lp_, D_MODEL)
    xs = x_sample.reshape(bs * ls, D_MODEL)
    mem2d = mem_prompt.reshape(bp * n_mem, D_MODEL)
    tiles_p = _tiles(bp, lp_)
    tiles_s = _tiles(bs, ls)
    conv0 = jnp.zeros((bp, CONV_W - 1, D_CONV), F32)
    h_zero = jnp.zeros((bp, N_SSM_HEADS, SSM_HEAD_DIM, D_STATE), F32)
    conv_all = cache_conv.reshape(depth * bs, CONV_W - 1, D_CONV)
    ssm_all = state_ssm.reshape(depth * bs, N_SSM_HEADS, SSM_HEAD_DIM, D_STATE)
    kv_all = (cache_k.reshape(depth * bs, past, D_DIFF), cache_v.reshape(depth * bs, past, D_DIFF))
    mem_k_all = cache_mem_k.reshape(depth * bs, n_mem, D_MEM)
    mem_v_all = cache_mem_v.reshape(depth * bs, n_mem, D_MEM)
    outs = [[] for _ in range(10)]
    for l in range(depth):
        lp = _layer_params(l, norm_pre_w, norm_post_w, w_in, conv_w, conv_b, dt_bias, a_log, d_skip,
                           ssm_norm_w, lambda_q1, lambda_k1, lambda_q2, lambda_k2, subln_w,
                           mem_norm_w, w_mem_kv, w_out)
        lambda_init = 0.8 - 0.6 * math.exp(-0.3 * l)
        mk_p, mv_p = _mem_kv(mem2d, lp["mem_norm_w"], lp["w_mem_kv"], min(512, bp * n_mem))
        mk_p = mk_p.reshape(bp, n_mem, D_MEM)
        mv_p = mv_p.reshape(bp, n_mem, D_MEM)
        xp, c_p, h_p, k_p, v_p = _mixer_layer(xp, bp, lp_, 0, 0, conv0, h_zero, None, mk_p, mv_p, lp,
                                              lambda_init, tiles_p)
        xs, c_s, h_s, k_s, v_s = _mixer_layer(xs, bs, ls, l * bs, l * bs, conv_all, ssm_all, kv_all,
                                              mem_k_all, mem_v_all, lp, lambda_init, tiles_s)
        for lst, val in zip(outs, (c_p, h_p, k_p, v_p, mk_p, mv_p, c_s, h_s, k_s, v_s)):
            lst.append(val)
    st = [jnp.stack(o) for o in outs]
    return (
        xp.reshape(bp, lp_, D_MODEL),
        xs.reshape(bs, ls, D_MODEL),
        st[0],
        st[1],
        st[2].reshape(depth, bp, lp_, N_DIFF_HEADS, 2, DIFF_HEAD_DIM),
        st[3].reshape(depth, bp, lp_, N_DIFF_HEADS, 2 * DIFF_HEAD_DIM),
        st[4].reshape(depth, bp, n_mem, N_MEM_HEADS, MEM_HEAD_DIM),
        st[5].reshape(depth, bp, n_mem, N_MEM_HEADS, MEM_HEAD_DIM),
        st[6],
        st[7],
        st[8].reshape(depth, bs, ls, N_DIFF_HEADS, 2, DIFF_HEAD_DIM),
        st[9].reshape(depth, bs, ls, N_DIFF_HEADS, 2 * DIFF_HEAD_DIM),
    )
```

```python
import functools
import math

import jax
import jax.numpy as jnp
from jax import lax
from jax.experimental import pallas as pl
from jax.experimental.pallas import tpu as pltpu

F32 = jnp.float32
BF16 = jnp.bfloat16

D_MODEL = 4096
D_SSM = 2048
SSM_HEAD_DIM = 64
N_SSM_HEADS = 32
N_SSM_GROUPS = 4
HEADS_PER_GROUP = N_SSM_HEADS // N_SSM_GROUPS
D_GROUP = D_SSM // N_SSM_GROUPS
D_STATE = 128
CONV_W = 4
D_BC = N_SSM_GROUPS * D_STATE
D_CONV = D_SSM + 2 * D_BC
D_DIFF = 1024
N_DIFF_HEADS = 4
DIFF_HEAD_DIM = 128
D_MEM = 1024
N_MEM_HEADS = 4
MEM_HEAD_DIM = 256
CHUNK = 64
NORM_EPS = 1e-6
SUBLN_EPS = 1e-5

LANE = 128
SUBLANE = 8
VMEM_LIMIT = 56 * 1024 * 1024

HEAD_W = 2 * DIFF_HEAD_DIM
COL_Z = 0
COL_X = D_SSM
COL_B = COL_X + D_SSM
COL_C = COL_B + D_BC
COL_QD = COL_C + D_BC
COL_GD = COL_QD + D_DIFF
COL_QM = COL_GD + D_DIFF
COL_GM = COL_QM + D_MEM
COL_KD = COL_GM + D_MEM
COL_VD = COL_KD + D_DIFF
D_MAIN = COL_VD + D_DIFF
PROJ_TN = 1024
K_TILE = COL_KD // PROJ_TN
V_TILE = COL_VD // PROJ_TN


def _params(n_axes):
    return pltpu.CompilerParams(dimension_semantics=("arbitrary",) * n_axes,
                                vmem_limit_bytes=VMEM_LIMIT)


def _silu(x):
    return x * jax.nn.sigmoid(x)


def _chunk_of(pos):
    assert CHUNK & (CHUNK - 1) == 0
    return lax.shift_right_logical(pos, jnp.int32(CHUNK.bit_length() - 1))


def _rms_scale(x, eps):
    return lax.rsqrt(jnp.mean(x * x, axis=-1, keepdims=True) + eps)


ROW_TILES = 1024 // LANE
N_HEADS_1024 = 1024 // HEAD_W


def _flat_row(c, halves_split):
    head, half = divmod(c, 2)
    return half * N_HEADS_1024 + head if halves_split else c


def _store_flat(ref, val, halves_split):
    rows = val.shape[0]
    for c in range(ROW_TILES):
        ref[pl.ds(_flat_row(c, halves_split), rows, stride=ROW_TILES), :] = val[:, c * LANE:(c + 1) * LANE]


def _load_flat(ref, rows, col0, width, halves_split):
    parts = [ref[pl.ds(_flat_row(col0 // LANE + c, halves_split), rows, stride=ROW_TILES), :]
             for c in range(width // LANE)]
    return parts[0] if len(parts) == 1 else jnp.concatenate(parts, axis=1)


def _to_flat_halves(a):
    return a.reshape(-1, N_HEADS_1024, 2, LANE).transpose(0, 2, 1, 3).reshape(-1, LANE)


def _from_flat_halves(flat, shape):
    return flat.reshape(-1, 2, N_HEADS_1024, LANE).transpose(0, 2, 1, 3).reshape(shape)


NORM_ROWS = 128


def _norm_rows(x_ref, nw_ref, h_scr):
    tm = x_ref.shape[0]
    step = min(NORM_ROWS, tm)
    for r in range(0, tm, step):
        x = x_ref[r:r + step, :]
        h_scr[r:r + step, :] = (x * _rms_scale(x, NORM_EPS) * nw_ref[...]).astype(BF16)


O_DT = D_SSM + D_CONV
O_Q = O_DT + N_SSM_HEADS
SRC_SEGMENT_OF = ((COL_QD, 0), (COL_KD, 1), (COL_VD, 2), (COL_GD, 3), (COL_QM, 4), (COL_GM, 5))
PREP_ROWS = 256


def _w_prep_kernel(w_ref, main_ref, dt_ref):
    main_ref[:, 0:O_DT] = w_ref[0, :, 0:O_DT].astype(BF16)
    lane = lax.broadcasted_iota(jnp.int32, (1, LANE), 1)
    dt_ref[...] = jnp.where(lane < N_SSM_HEADS, w_ref[0, :, O_DT:O_DT + LANE], 0.0).astype(BF16)
    for dst, seg in SRC_SEGMENT_OF:
        src = O_Q + seg * D_DIFF
        main_ref[:, dst:dst + D_DIFF] = w_ref[0, :, src:src + D_DIFF].astype(BF16)


def _w_prep(w_in, layer):
    d_in = w_in.shape[2]
    return pl.pallas_call(
        _w_prep_kernel,
        grid=(D_MODEL // PREP_ROWS,),
        in_specs=[pl.BlockSpec((1, PREP_ROWS, d_in), lambda i: (layer, i, 0))],
        out_specs=[
            pl.BlockSpec((PREP_ROWS, D_MAIN), lambda i: (i, 0)),
            pl.BlockSpec((PREP_ROWS, LANE), lambda i: (i, 0)),
        ],
        out_shape=[
            jax.ShapeDtypeStruct((D_MODEL, D_MAIN), BF16),
            jax.ShapeDtypeStruct((D_MODEL, LANE), BF16),
        ],
        compiler_params=_params(1),
        name="w_prep",
    )(w_in)


def _in_proj_kernel(x_ref, nw_ref, w_ref, wdt_ref, *refs):
    main_ref, k_ref, v_ref, dt_ref, h_scr = refs[-5:]
    j = pl.program_id(1)

    @pl.when(j == 0)
    def _():
        _norm_rows(x_ref, nw_ref, h_scr)
        dt_ref[...] = jnp.dot(h_scr[...], wdt_ref[...], preferred_element_type=F32)

    acc = jnp.dot(h_scr[...], w_ref[...], preferred_element_type=F32)
    main_ref[...] = acc.astype(BF16)

    @pl.when(j == K_TILE)
    def _():
        _store_flat(k_ref, acc, halves_split=False)

    @pl.when(j == V_TILE)
    def _():
        _store_flat(v_ref, acc, halves_split=True)


def _in_proj(x2d, norm_w, w_main, w_dt, tm, layer, depth, kv_all):
    m = x2d.shape[0]
    ni = m // tm
    grid = (ni, D_MAIN // PROJ_TN)
    kv_spec = pl.BlockSpec((tm * ROW_TILES, LANE), lambda i, j: (layer * ni + i, 0))
    kv_shape = jax.ShapeDtypeStruct((depth * m * ROW_TILES, LANE), F32)
    in_specs = [
        pl.BlockSpec((tm, D_MODEL), lambda i, j: (i, 0)),
        pl.BlockSpec((1, D_MODEL), lambda i, j: (0, 0)),
        pl.BlockSpec((D_MODEL, PROJ_TN), lambda i, j: (0, j)),
        pl.BlockSpec((D_MODEL, LANE), lambda i, j: (0, 0)),
    ]
    operands = [x2d, norm_w, w_main, w_dt]
    aliases = {}
    if kv_all is not None:
        in_specs += [pl.BlockSpec(memory_space=pl.ANY)] * 2
        aliases = {len(operands): 1, len(operands) + 1: 2}
        operands += list(kv_all)
    return pl.pallas_call(
        _in_proj_kernel,
        grid=grid,
        in_specs=in_specs,
        out_specs=[
            pl.BlockSpec((tm, PROJ_TN), lambda i, j: (i, j)),
            kv_spec,
            kv_spec,
            pl.BlockSpec((tm, LANE), lambda i, j: (i, 0)),
        ],
        out_shape=[
            jax.ShapeDtypeStruct((m, D_MAIN), BF16),
            kv_shape,
            kv_shape,
            jax.ShapeDtypeStruct((m, LANE), F32),
        ],
        scratch_shapes=[pltpu.VMEM((tm, D_MODEL), BF16)],
        input_output_aliases=aliases,
        compiler_params=_params(2),
        name="in_proj",
    )(*operands)


def _mem_kv_kernel(x_ref, nw_ref, w_ref, mk_ref, mv_ref, h_scr):
    j = pl.program_id(1)

    @pl.when(j == 0)
    def _():
        _norm_rows(x_ref, nw_ref, h_scr)

    acc = jnp.dot(h_scr[...], w_ref[...], preferred_element_type=F32)

    @pl.when(j == 0)
    def _():
        _store_flat(mk_ref, acc, halves_split=True)

    @pl.when(j == 1)
    def _():
        _store_flat(mv_ref, acc, halves_split=True)


def _mem_kv(mem2d, norm_w, w_kv, tm):
    m = mem2d.shape[0]
    return pl.pallas_call(
        _mem_kv_kernel,
        grid=(m // tm, 2),
        in_specs=[
            pl.BlockSpec((tm, D_MODEL), lambda i, j: (i, 0)),
            pl.BlockSpec((1, D_MODEL), lambda i, j: (0, 0)),
            pl.BlockSpec((D_MODEL, D_MEM), lambda i, j: (0, j)),
        ],
        out_specs=[
            pl.BlockSpec((tm * ROW_TILES, LANE), lambda i, j: (i, 0)),
            pl.BlockSpec((tm * ROW_TILES, LANE), lambda i, j: (i, 0)),
        ],
        out_shape=[jax.ShapeDtypeStruct((m * ROW_TILES, LANE), F32)] * 2,
        scratch_shapes=[pltpu.VMEM((tm, D_MODEL), BF16)],
        compiler_params=_params(2),
        name="mem_kv",
    )(mem2d, norm_w, w_kv)


def _split_bf16(x, n):
    parts = []
    r = x
    for _ in range(n - 1):
        p = r.astype(BF16)
        parts.append(p)
        r = r - p.astype(F32)
    parts.append(r.astype(BF16))
    return parts


def _sum_rows(a, n, rows):
    out = a[(n - 1) * rows:n * rows]
    for i in range(n - 2, -1, -1):
        out = out + a[i * rows:(i + 1) * rows]
    return out


def _ssd_kernel(z_ref, xx_ref, xb_ref, xc_ref, dt_ref, hist_ref, h0_ref,
                cw_ref, cb_ref, dtb_ref, alog_ref, dskip_ref, nw_ref, e_ref,
                y_ref, nconv_ref, hnew_ref, xp_scr, st_scr, y_scr, *, q):
    c = pl.program_id(1)
    nc = pl.num_programs(1)
    hist_row = SUBLANE - (CONV_W - 1)

    @pl.when(c == 0)
    def _():
        xp_scr[hist_row:SUBLANE, :] = hist_ref[0]
        for g in range(N_SSM_GROUPS):
            hg = h0_ref[0, g * HEADS_PER_GROUP:(g + 1) * HEADS_PER_GROUP]
            st_scr[g] = hg.reshape(D_GROUP, D_STATE).T

    xp_scr[SUBLANE:SUBLANE + q, 0:D_SSM] = xx_ref[...].astype(F32)
    xp_scr[SUBLANE:SUBLANE + q, D_SSM:D_SSM + D_BC] = xb_ref[...].astype(F32)
    xp_scr[SUBLANE:SUBLANE + q, D_SSM + D_BC:D_CONV] = xc_ref[...].astype(F32)
    acc = cb_ref[...] + xp_scr[hist_row:hist_row + q, :] * cw_ref[0:1, :]
    for j in range(1, CONV_W):
        acc = acc + xp_scr[hist_row + j:hist_row + j + q, :] * cw_ref[j:j + 1, :]
    xbc = _silu(acc)
    xs = xbc[:, 0:D_SSM]
    xs_bf = xs.astype(BF16)
    bm = xbc[:, D_SSM:D_SSM + D_BC].astype(BF16)
    cm = xbc[:, D_SSM + D_BC:D_CONV].astype(BF16)

    @pl.when(c == nc - 1)
    def _():
        nconv_ref[0] = xp_scr[q + hist_row:q + SUBLANE, :]

    xp_scr[0:SUBLANE, :] = xp_scr[q:q + SUBLANE, :]

    dtv = jax.nn.softplus(dt_ref[...] + dtb_ref[...])
    a = -jnp.exp(alog_ref[...])
    da = dtv * a
    ri = lax.broadcasted_iota(jnp.int32, (q, q), 0)
    ci = lax.broadcasted_iota(jnp.int32, (q, q), 1)
    causal = ri >= ci
    tril = jnp.where(causal, 1.0, 0.0).astype(BF16)
    triu = jnp.where(ri <= ci, 1.0, 0.0).astype(BF16)
    eye = jnp.where(ri == ci, 1.0, 0.0).astype(BF16)
    da3 = jnp.concatenate(_split_bf16(da, 3), axis=1)
    dt3 = jnp.concatenate(_split_bf16(dtv, 3), axis=1)
    cum3 = jnp.dot(tril, da3, preferred_element_type=F32)
    cum = cum3[:, 2 * LANE:3 * LANE] + cum3[:, LANE:2 * LANE] + cum3[:, 0:LANE]
    tn_dims = (((0,), (0,)), ((), ()))
    cum_t = _sum_rows(lax.dot_general(da3, triu, tn_dims, preferred_element_type=F32), 3, LANE)
    dt_t = _sum_rows(lax.dot_general(dt3, eye, tn_dims, preferred_element_type=F32), 3, LANE)

    ecum = jnp.exp(cum)
    dd = dtv * jnp.exp(cum[q - 1:q, :] - cum)
    ex_in = jnp.concatenate(_split_bf16(ecum, 2) + _split_bf16(dd, 2), axis=0)
    ex = jnp.dot(ex_in, e_ref[...], preferred_element_type=F32)
    ecum_e = ex[q:2 * q] + ex[0:q]
    dd_e = ex[3 * q:4 * q] + ex[2 * q:3 * q]
    xdd = (xs * dd_e).astype(BF16)

    lane = lax.broadcasted_iota(jnp.int32, (q, LANE), 1)
    lo_half = lane < SSM_HEAD_DIM
    nt_dims = (((1,), (1,)), ((), ()))
    for g in range(N_SSM_GROUPS):
        bg = bm[:, g * D_STATE:(g + 1) * D_STATE]
        cg = cm[:, g * D_STATE:(g + 1) * D_STATE]
        gs = slice(g * D_GROUP, (g + 1) * D_GROUP)
        cbg = lax.dot_general(cg, bg, nt_dims, preferred_element_type=F32)
        st = st_scr[g]
        y_off = jnp.dot(cg, st.astype(BF16), preferred_element_type=F32) * ecum_e[:, gs]
        st_scr[g] = st * ecum_e[q - 1:q, gs] + lax.dot_general(
            bg, xdd[:, gs], tn_dims, preferred_element_type=F32)
        for t in range(HEADS_PER_GROUP // 2):
            ws = []
            for h in (g * HEADS_PER_GROUP + 2 * t, g * HEADS_PER_GROUP + 2 * t + 1):
                seg = cum[:, h:h + 1] - cum_t[h:h + 1, :]
                lmat = jnp.exp(jnp.where(causal, seg, -jnp.inf))
                ws.append((cbg * lmat * dt_t[h:h + 1, :]).astype(BF16))
            cs = slice(g * D_GROUP + t * LANE, g * D_GROUP + (t + 1) * LANE)
            xpair = xs_bf[:, cs]
            zero = jnp.zeros_like(xpair)
            y_diag = (jnp.dot(ws[0], jnp.where(lo_half, xpair, zero), preferred_element_type=F32)
                      + jnp.dot(ws[1], jnp.where(lo_half, zero, xpair), preferred_element_type=F32))
            y_scr[:, cs] = (y_diag + y_off[:, t * LANE:(t + 1) * LANE]) + xs[:, cs] * dskip_ref[:, cs]

    for g in range(N_SSM_GROUPS):
        gs = slice(g * D_GROUP, (g + 1) * D_GROUP)
        yg = y_scr[:, gs] * _silu(z_ref[:, gs].astype(F32))
        y_ref[:, gs] = (yg * _rms_scale(yg, NORM_EPS) * nw_ref[:, gs]).astype(BF16)

    @pl.when(c == nc - 1)
    def _():
        for g in range(N_SSM_GROUPS):
            hnew_ref[0, g * HEADS_PER_GROUP:(g + 1) * HEADS_PER_GROUP] = (
                st_scr[g].T.reshape(HEADS_PER_GROUP, SSM_HEAD_DIM, D_STATE))


def _ssd(main, dt, conv_hist, h0, b_off, lp, bsz, length, q):
    nc = length // q
    m = bsz * length
    row = lambda b, c: b * nc + c
    return pl.pallas_call(
        functools.partial(_ssd_kernel, q=q),
        grid=(bsz, nc),
        in_specs=[
            pl.BlockSpec((q, D_SSM), lambda b, c: (row(b, c), COL_Z // D_SSM)),
            pl.BlockSpec((q, D_SSM), lambda b, c: (row(b, c), COL_X // D_SSM)),
            pl.BlockSpec((q, D_BC), lambda b, c: (row(b, c), COL_B // D_BC)),
            pl.BlockSpec((q, D_BC), lambda b, c: (row(b, c), COL_C // D_BC)),
            pl.BlockSpec((q, LANE), lambda b, c: (row(b, c), 0)),
            pl.BlockSpec((1, CONV_W - 1, D_CONV), lambda b, c: (b_off + b, 0, 0)),
            pl.BlockSpec((1, N_SSM_HEADS, SSM_HEAD_DIM, D_STATE), lambda b, c: (b_off + b, 0, 0, 0)),
            pl.BlockSpec((CONV_W, D_CONV), lambda b, c: (0, 0)),
            pl.BlockSpec((1, D_CONV), lambda b, c: (0, 0)),
            pl.BlockSpec((1, LANE), lambda b, c: (0, 0)),
            pl.BlockSpec((1, LANE), lambda b, c: (0, 0)),
            pl.BlockSpec((1, D_SSM), lambda b, c: (0, 0)),
            pl.BlockSpec((1, D_SSM), lambda b, c: (0, 0)),
            pl.BlockSpec((LANE, D_SSM), lambda b, c: (0, 0)),
        ],
        out_specs=[
            pl.BlockSpec((q, D_SSM), lambda b, c: (row(b, c), 0)),
            pl.BlockSpec((1, CONV_W - 1, D_CONV), lambda b, c: (b, 0, 0)),
            pl.BlockSpec((1, N_SSM_HEADS, SSM_HEAD_DIM, D_STATE), lambda b, c: (b, 0, 0, 0)),
        ],
        out_shape=[
            jax.ShapeDtypeStruct((m, D_SSM), BF16),
            jax.ShapeDtypeStruct((bsz, CONV_W - 1, D_CONV), F32),
            jax.ShapeDtypeStruct((bsz, N_SSM_HEADS, SSM_HEAD_DIM, D_STATE), F32),
        ],
        scratch_shapes=[
            pltpu.VMEM((q + SUBLANE, D_CONV), F32),
            pltpu.VMEM((N_SSM_GROUPS, D_STATE, D_GROUP), F32),
            pltpu.VMEM((q, D_SSM), F32),
        ],
        compiler_params=_params(2),
        name="ssd",
    )(main, main, main, main, dt, conv_hist, h0,
      lp["conv_w"], lp["conv_b"], lp["dt_bias"], lp["a_log"], lp["d_skip_e"], lp["ssm_norm_w"],
      lp["expand"])


def _lambda(lq1, lk1, lq2, lk2, lambda_init):
    return (jnp.exp(jnp.sum(lq1[...] * lk1[...], axis=-1, keepdims=True))
            - jnp.exp(jnp.sum(lq2[...] * lk2[...], axis=-1, keepdims=True)) + lambda_init)


def _diff_finish(o1, o2, lam, sw_ref, g, lambda_init):
    o = o1 - lam * o2
    on = (o * _rms_scale(o, SUBLN_EPS) * sw_ref[...]) * (1.0 - lambda_init)
    return (on * _silu(g.astype(F32))).astype(BF16)


ATTN_ROWS = 32


def _attn_update(q_ref, k_ref, v_ref, m_scr, l_scr, acc_scr, mp, q_pos0, k_pos0, masked):
    c = (DIFF_HEAD_DIM ** -0.5) * math.log2(math.e)
    cols = slice(mp * DIFF_HEAD_DIM, (mp + 1) * DIFF_HEAD_DIM)
    n, kw = q_ref.shape[0], k_ref.shape[0]
    s = lax.dot_general(q_ref[:, cols], k_ref[:, cols], (((1,), (1,)), ((), ())),
                        preferred_element_type=F32)
    m_prev = m_scr[mp]
    if masked:
        k_chunk = _chunk_of(k_pos0 + lax.broadcasted_iota(jnp.int32, (1, kw), 1))
    step = min(ATTN_ROWS, n)
    ps, m_news, p_sums = [], [], []
    for r in range(0, n, step):
        sr = s[r:r + step]
        if masked:
            q_chunk = _chunk_of(q_pos0 + r + lax.broadcasted_iota(jnp.int32, (step, 1), 0))
            sr = jnp.where(k_chunk <= q_chunk, sr, -jnp.inf)
        mn = jnp.maximum(m_prev[r:r + step], jnp.max(sr, axis=-1, keepdims=True))
        p = jnp.exp2((sr - mn) * c)
        ps.append(p.astype(BF16))
        m_news.append(mn)
        p_sums.append(jnp.sum(p, axis=-1, keepdims=True))
    m_new = jnp.concatenate(m_news, axis=0)
    alpha = jnp.exp2((m_prev - m_new) * c)
    l_scr[mp] = alpha * l_scr[mp] + jnp.concatenate(p_sums, axis=0)
    acc_scr[mp] = alpha * acc_scr[mp] + jnp.dot(
        jnp.concatenate(ps, axis=0), v_ref[...], preferred_element_type=F32)
    m_scr[mp] = m_new


def _diff_attn_kernel(q_ref, k_ref, v_ref, g_ref, lq1, lk1, lq2, lk2, sw_ref, o_ref,
                      m_scr, l_scr, acc_scr, *, tq, tk, lambda_init):
    qi = pl.program_id(2)
    ki = pl.program_id(3)
    update = functools.partial(_attn_update, q_ref, k_ref, v_ref, m_scr, l_scr, acc_scr,
                               q_pos0=qi * tq, k_pos0=ki * tk)

    @pl.when(ki == 0)
    def _():
        m_scr[...] = jnp.full(m_scr.shape, -jnp.inf, F32)
        l_scr[...] = jnp.zeros(l_scr.shape, F32)
        acc_scr[...] = jnp.zeros(acc_scr.shape, F32)

    @pl.when(ki < qi)
    def _():
        for mp in range(2):
            update(mp=mp, masked=False)

    @pl.when(ki == qi)
    def _():
        for mp in range(2):
            update(mp=mp, masked=True)
        lam = _lambda(lq1, lk1, lq2, lk2, lambda_init)
        o_ref[...] = _diff_finish(acc_scr[0] / l_scr[0], acc_scr[1] / l_scr[1], lam, sw_ref, g_ref[...],
                                  lambda_init)


def _diff_attn_prompt(main, lp, bsz, length, lambda_init, tq):
    tk = tq
    nq = length // tq
    m = bsz * length
    vec = pl.BlockSpec((1, DIFF_HEAD_DIM), lambda b, h, qi, ki: (0, 0))
    return pl.pallas_call(
        functools.partial(_diff_attn_kernel, tq=tq, tk=tk, lambda_init=lambda_init),
        grid=(bsz, N_DIFF_HEADS, nq, nq),
        in_specs=[
            pl.BlockSpec((tq, HEAD_W), lambda b, h, qi, ki: (b * nq + qi, COL_QD // HEAD_W + h)),
            pl.BlockSpec((tk, HEAD_W), lambda b, h, qi, ki: (b * nq + jnp.minimum(ki, qi), COL_KD // HEAD_W + h)),
            pl.BlockSpec((tk, HEAD_W), lambda b, h, qi, ki: (b * nq + jnp.minimum(ki, qi), COL_VD // HEAD_W + h)),
            pl.BlockSpec((tq, HEAD_W), lambda b, h, qi, ki: (b * nq + qi, COL_GD // HEAD_W + h)),
            vec, vec, vec, vec,
            pl.BlockSpec((1, HEAD_W), lambda b, h, qi, ki: (0, 0)),
        ],
        out_specs=pl.BlockSpec((tq, HEAD_W), lambda b, h, qi, ki: (b * nq + qi, h)),
        out_shape=jax.ShapeDtypeStruct((m, D_DIFF), BF16),
        scratch_shapes=[
            pltpu.VMEM((2, tq, 1), F32),
            pltpu.VMEM((2, tq, 1), F32),
            pltpu.VMEM((2, tq, HEAD_W), F32),
        ],
        compiler_params=_params(4),
        name="diff_attn_prompt",
    )(main, main, main, main, lp["lambda_q1"], lp["lambda_k1"], lp["lambda_q2"], lp["lambda_k2"],
      lp["subln_w"])


def _diff_attn_decode_kernel(q_ref, kn_ref, vn_ref, g_ref, kp_ref, vp_ref, lq1, lk1, lq2, lk2, sw_ref,
                             o_ref, *, past, lambda_init):
    scale = DIFF_HEAD_DIM ** -0.5
    lq = q_ref.shape[0]
    nt_dims = (((1,), (1,)), ((), ()))
    mask_p = (_chunk_of(lax.broadcasted_iota(jnp.int32, (lq, past), 1))
              <= _chunk_of(past + lax.broadcasted_iota(jnp.int32, (lq, past), 0)))
    mask_n = (_chunk_of(past + lax.broadcasted_iota(jnp.int32, (lq, lq), 1))
              <= _chunk_of(past + lax.broadcasted_iota(jnp.int32, (lq, lq), 0)))
    lam = _lambda(lq1, lk1, lq2, lk2, lambda_init)
    for h in range(N_DIFF_HEADS):
        hs = slice(h * HEAD_W, (h + 1) * HEAD_W)
        vp = _load_flat(vp_ref, past, h * HEAD_W, HEAD_W, halves_split=True).astype(BF16)
        vn = vn_ref[:, hs]
        outs = []
        for mp in range(2):
            c0 = h * HEAD_W + mp * DIFF_HEAD_DIM
            qm = q_ref[:, c0:c0 + DIFF_HEAD_DIM]
            kp = _load_flat(kp_ref, past, c0, DIFF_HEAD_DIM, halves_split=False).astype(BF16)
            sp = lax.dot_general(qm, kp, nt_dims, preferred_element_type=F32) * scale
            sn = lax.dot_general(qm, kn_ref[:, c0:c0 + DIFF_HEAD_DIM], nt_dims,
                                 preferred_element_type=F32) * scale
            sp = jnp.where(mask_p, sp, -jnp.inf)
            sn = jnp.where(mask_n, sn, -jnp.inf)
            mx = jnp.maximum(jnp.max(sp, axis=-1, keepdims=True), jnp.max(sn, axis=-1, keepdims=True))
            pp = jnp.exp(sp - mx)
            pn = jnp.exp(sn - mx)
            denom = jnp.sum(pp, axis=-1, keepdims=True) + jnp.sum(pn, axis=-1, keepdims=True)
            o = (jnp.dot(pp.astype(BF16), vp, preferred_element_type=F32)
                 + jnp.dot(pn.astype(BF16), vn, preferred_element_type=F32))
            outs.append(o / denom)
        o_ref[:, hs] = _diff_finish(outs[0], outs[1], lam, sw_ref, g_ref[:, hs], lambda_init)


def _diff_attn_decode(main, k_past, v_past, b_off, lp, bsz, length, past, lambda_init):
    m = bsz * length
    vec = pl.BlockSpec((1, DIFF_HEAD_DIM), lambda b: (0, 0))
    return pl.pallas_call(
        functools.partial(_diff_attn_decode_kernel, past=past, lambda_init=lambda_init),
        grid=(bsz,),
        in_specs=[
            pl.BlockSpec((length, D_DIFF), lambda b: (b, COL_QD // D_DIFF)),
            pl.BlockSpec((length, D_DIFF), lambda b: (b, COL_KD // D_DIFF)),
            pl.BlockSpec((length, D_DIFF), lambda b: (b, COL_VD // D_DIFF)),
            pl.BlockSpec((length, D_DIFF), lambda b: (b, COL_GD // D_DIFF)),
            pl.BlockSpec((past * ROW_TILES, LANE), lambda b: (b_off + b, 0)),
            pl.BlockSpec((past * ROW_TILES, LANE), lambda b: (b_off + b, 0)),
            vec, vec, vec, vec,
            pl.BlockSpec((1, HEAD_W), lambda b: (0, 0)),
        ],
        out_specs=pl.BlockSpec((length, D_DIFF), lambda b: (b, 0)),
        out_shape=jax.ShapeDtypeStruct((m, D_DIFF), BF16),
        compiler_params=_params(1),
        name="diff_attn_decode",
    )(main, main, main, main, k_past, v_past,
      lp["lambda_q1"], lp["lambda_k1"], lp["lambda_q2"], lp["lambda_k2"], lp["subln_w"])


def _mem_attn_kernel(q_ref, g_ref, mk_ref, mv_ref, o_ref, *, n_mem):
    scale = MEM_HEAD_DIM ** -0.5
    for h in range(N_MEM_HEADS):
        hs = slice(h * MEM_HEAD_DIM, (h + 1) * MEM_HEAD_DIM)
        mk = _load_flat(mk_ref, n_mem, h * MEM_HEAD_DIM, MEM_HEAD_DIM, halves_split=True).astype(BF16)
        mv = _load_flat(mv_ref, n_mem, h * MEM_HEAD_DIM, MEM_HEAD_DIM, halves_split=True).astype(BF16)
        s = lax.dot_general(q_ref[:, hs], mk, (((1,), (1,)), ((), ())), preferred_element_type=F32) * scale
        p = jnp.exp(s - jnp.max(s, axis=-1, keepdims=True))
        pr = p / jnp.sum(p, axis=-1, keepdims=True)
        o = jnp.dot(pr.astype(BF16), mv, preferred_element_type=F32)
        o_ref[:, hs] = (o * _silu(g_ref[:, hs].astype(F32))).astype(BF16)


def _mem_attn(main, mk, mv, b_off, bsz, length, n_mem, tq):
    nq = length // tq
    m = bsz * length
    return pl.pallas_call(
        functools.partial(_mem_attn_kernel, n_mem=n_mem),
        grid=(bsz, nq),
        in_specs=[
            pl.BlockSpec((tq, D_MEM), lambda b, qi: (b * nq + qi, COL_QM // D_MEM)),
            pl.BlockSpec((tq, D_MEM), lambda b, qi: (b * nq + qi, COL_GM // D_MEM)),
            pl.BlockSpec((n_mem * ROW_TILES, LANE), lambda b, qi: (b_off + b, 0)),
            pl.BlockSpec((n_mem * ROW_TILES, LANE), lambda b, qi: (b_off + b, 0)),
        ],
        out_specs=pl.BlockSpec((tq, D_MEM), lambda b, qi: (b * nq + qi, 0)),
        out_shape=jax.ShapeDtypeStruct((m, D_MEM), BF16),
        compiler_params=_params(2),
        name="mem_attn",
    )(main, main, mk, mv)


def _out_proj_kernel(ys_ref, yd_ref, ym_ref, w_ref, x_ref, nw_ref, o_ref, *, nj, tn):
    j = pl.program_id(1)
    acc = (jnp.dot(ys_ref[...], w_ref[0:D_SSM, :], preferred_element_type=F32)
           + jnp.dot(yd_ref[...], w_ref[D_SSM:D_SSM + D_DIFF, :], preferred_element_type=F32)
           + jnp.dot(ym_ref[...], w_ref[D_SSM + D_DIFF:D_MODEL, :], preferred_element_type=F32))
    for t in range(nj):
        @pl.when(j == t)
        def _(t=t):
            o_ref[:, t * tn:(t + 1) * tn] = acc

    @pl.when(j == nj - 1)
    def _():
        tm = o_ref.shape[0]
        step = min(NORM_ROWS, tm)
        for r in range(0, tm, step):
            o = o_ref[r:r + step, :]
            o_ref[r:r + step, :] = x_ref[r:r + step, :] + o * _rms_scale(o, NORM_EPS) * nw_ref[...]


def _out_proj(ys, yd, ym, w_out, x2d, norm_w, tm, tn):
    m = x2d.shape[0]
    nj = D_MODEL // tn
    return pl.pallas_call(
        functools.partial(_out_proj_kernel, nj=nj, tn=tn),
        grid=(m // tm, nj),
        in_specs=[
            pl.BlockSpec((tm, D_SSM), lambda i, j: (i, 0)),
            pl.BlockSpec((tm, D_DIFF), lambda i, j: (i, 0)),
            pl.BlockSpec((tm, D_MEM), lambda i, j: (i, 0)),
            pl.BlockSpec((D_MODEL, tn), lambda i, j: (0, j)),
            pl.BlockSpec((tm, D_MODEL), lambda i, j: (i, 0)),
            pl.BlockSpec((1, D_MODEL), lambda i, j: (0, 0)),
        ],
        out_specs=pl.BlockSpec((tm, D_MODEL), lambda i, j: (i, 0)),
        out_shape=jax.ShapeDtypeStruct((m, D_MODEL), F32),
        compiler_params=_params(2),
        name="out_proj",
    )(ys, yd, ym, w_out, x2d, norm_w)


def _mixer_layer(x2d, bsz, length, layer, depth, kv_all, b_off, mem_off, conv_hist, h0, kv_past, past, mk,
                 mv, n_mem, lp, lambda_init, tiles):
    main, k32, v32, dt = _in_proj(x2d, lp["norm_pre_w"], lp["w_main"], lp["w_dt"], tiles["proj_tm"],
                                  layer, depth, kv_all)
    ys, new_conv, h_new = _ssd(main, dt, conv_hist, h0, b_off, lp, bsz, length, tiles["ssd_q"])
    if kv_past is None:
        yd = _diff_attn_prompt(main, lp, bsz, length, lambda_init, tiles["attn_tq"])
    else:
        yd = _diff_attn_decode(main, kv_past[0], kv_past[1], b_off, lp, bsz, length, past, lambda_init)
    ym = _mem_attn(main, mk, mv, mem_off, bsz, length, n_mem, tiles["mem_tq"])
    x_new = _out_proj(ys, yd, ym, lp["w_out"], x2d, lp["norm_post_w"], tiles["out_tm"], tiles["out_tn"])
    return x_new, new_conv, h_new, (k32, v32)


def _layer_params(l, norm_pre_w, norm_post_w, w_in, conv_w, conv_b, dt_bias, a_log, d_skip,
                  ssm_norm_w, lambda_q1, lambda_k1, lambda_q2, lambda_k2, subln_w, mem_norm_w,
                  w_mem_kv, w_out):
    w_main, w_dt = _w_prep(w_in, l)
    pad_heads = lambda v: jnp.pad(v, (0, LANE - N_SSM_HEADS)).reshape(1, LANE)
    head_of_channel = jnp.arange(D_SSM) // SSM_HEAD_DIM
    expand = (jnp.arange(LANE)[:, None] == head_of_channel[None, :]).astype(BF16)
    return {
        "norm_pre_w": norm_pre_w[l].reshape(1, D_MODEL),
        "norm_post_w": norm_post_w[l].reshape(1, D_MODEL),
        "w_main": w_main,
        "w_dt": w_dt,
        "conv_w": conv_w[l],
        "conv_b": conv_b[l].reshape(1, D_CONV),
        "dt_bias": pad_heads(dt_bias[l]),
        "a_log": pad_heads(a_log[l]),
        "d_skip_e": jnp.repeat(d_skip[l], SSM_HEAD_DIM).reshape(1, D_SSM),
        "ssm_norm_w": ssm_norm_w[l].reshape(1, D_SSM),
        "expand": expand,
        "lambda_q1": lambda_q1[l].reshape(1, DIFF_HEAD_DIM),
        "lambda_k1": lambda_k1[l].reshape(1, DIFF_HEAD_DIM),
        "lambda_q2": lambda_q2[l].reshape(1, DIFF_HEAD_DIM),
        "lambda_k2": lambda_k2[l].reshape(1, DIFF_HEAD_DIM),
        "subln_w": subln_w[l].reshape(1, HEAD_W),
        "mem_norm_w": mem_norm_w[l].reshape(1, D_MODEL),
        "w_mem_kv": w_mem_kv[l].astype(BF16),
        "w_out": w_out[l].astype(BF16),
    }


def _tiles(bsz, length):
    m = bsz * length
    return {
        "proj_tm": min(512, m),
        "ssd_q": min(128, length),
        "attn_tq": min(1024, length),
        "mem_tq": min(1024, length),
        "out_tm": min(512, m),
        "out_tn": 512,
    }


def kernel(x_prompt, x_sample, mem_prompt, cache_conv, state_ssm, cache_k, cache_v, cache_mem_k,
           cache_mem_v, norm_pre_w, norm_post_w, w_in, conv_w, conv_b, dt_bias, a_log, d_skip,
           ssm_norm_w, lambda_q1, lambda_k1, lambda_q2, lambda_k2, subln_w, mem_norm_w, w_mem_kv,
           w_out):
    depth = w_in.shape[0]
    bp, lp_, _ = x_prompt.shape
    bs, ls, _ = x_sample.shape
    n_mem = mem_prompt.shape[1]
    past = cache_k.shape[2]
    xp = x_prompt.reshape(bp * lp_, D_MODEL)
    xs = x_sample.reshape(bs * ls, D_MODEL)
    mem2d = mem_prompt.reshape(bp * n_mem, D_MODEL)
    tiles_p = _tiles(bp, lp_)
    tiles_s = _tiles(bs, ls)
    conv0 = jnp.zeros((bp, CONV_W - 1, D_CONV), F32)
    h_zero = jnp.zeros((bp, N_SSM_HEADS, SSM_HEAD_DIM, D_STATE), F32)
    conv_all = cache_conv.reshape(depth * bs, CONV_W - 1, D_CONV)
    ssm_all = state_ssm.reshape(depth * bs, N_SSM_HEADS, SSM_HEAD_DIM, D_STATE)
    kv_all = (cache_k.reshape(-1, LANE), _to_flat_halves(cache_v))
    mem_k_all = _to_flat_halves(cache_mem_k)
    mem_v_all = _to_flat_halves(cache_mem_v)
    outs = [[] for _ in range(6)]
    kv_p = kv_s = None
    for l in range(depth):
        lp = _layer_params(l, norm_pre_w, norm_post_w, w_in, conv_w, conv_b, dt_bias, a_log, d_skip,
                           ssm_norm_w, lambda_q1, lambda_k1, lambda_q2, lambda_k2, subln_w,
                           mem_norm_w, w_mem_kv, w_out)
        lambda_init = 0.8 - 0.6 * math.exp(-0.3 * l)
        mk_p, mv_p = _mem_kv(mem2d, lp["mem_norm_w"], lp["w_mem_kv"], min(512, bp * n_mem))
        xp, c_p, h_p, kv_p = _mixer_layer(xp, bp, lp_, l, depth, kv_p, 0, 0, conv0, h_zero, None, 0, mk_p,
                                          mv_p, n_mem, lp, lambda_init, tiles_p)
        xs, c_s, h_s, kv_s = _mixer_layer(xs, bs, ls, l, depth, kv_s, l * bs, l * bs, conv_all, ssm_all,
                                          kv_all, past, mem_k_all, mem_v_all, n_mem, lp, lambda_init,
                                          tiles_s)
        for lst, val in zip(outs, (c_p, h_p, mk_p, mv_p, c_s, h_s)):
            lst.append(val)
    st = [jnp.stack(o) for o in outs]
    return (
        xp.reshape(bp, lp_, D_MODEL),
        xs.reshape(bs, ls, D_MODEL),
        st[0],
        st[1],
        kv_p[0].reshape(depth, bp, lp_, N_DIFF_HEADS, 2, DIFF_HEAD_DIM),
        _from_flat_halves(kv_p[1], (depth, bp, lp_, N_DIFF_HEADS, 2 * DIFF_HEAD_DIM)),
        _from_flat_halves(st[2], (depth, bp, n_mem, N_MEM_HEADS, MEM_HEAD_DIM)),
        _from_flat_halves(st[3], (depth, bp, n_mem, N_MEM_HEADS, MEM_HEAD_DIM)),
        st[4],
        st[5],
        kv_s[0].reshape(depth, bs, ls, N_DIFF_HEADS, 2, DIFF_HEAD_DIM),
        _from_flat_halves(kv_s[1], (depth, bs, ls, N_DIFF_HEADS, 2 * DIFF_HEAD_DIM)),
    )
```

```python
import functools
import math

import jax
import jax.numpy as jnp
from jax import lax
from jax.experimental import pallas as pl
from jax.experimental.pallas import tpu as pltpu

F32 = jnp.float32
BF16 = jnp.bfloat16

D_MODEL = 4096
D_SSM = 2048
SSM_HEAD_DIM = 64
N_SSM_HEADS = 32
N_SSM_GROUPS = 4
HEADS_PER_GROUP = N_SSM_HEADS // N_SSM_GROUPS
D_GROUP = D_SSM // N_SSM_GROUPS
D_STATE = 128
CONV_W = 4
D_BC = N_SSM_GROUPS * D_STATE
D_CONV = D_SSM + 2 * D_BC
D_DIFF = 1024
N_DIFF_HEADS = 4
DIFF_HEAD_DIM = 128
D_MEM = 1024
N_MEM_HEADS = 4
MEM_HEAD_DIM = 256
CHUNK = 64
NORM_EPS = 1e-6
SUBLN_EPS = 1e-5

LANE = 128
SUBLANE = 8
VMEM_LIMIT = 56 * 1024 * 1024

HEAD_W = 2 * DIFF_HEAD_DIM
COL_Z = 0
COL_X = D_SSM
COL_B = COL_X + D_SSM
COL_C = COL_B + D_BC
COL_QD = COL_C + D_BC
COL_GD = COL_QD + D_DIFF
COL_QM = COL_GD + D_DIFF
COL_GM = COL_QM + D_MEM
COL_KD = COL_GM + D_MEM
COL_VD = COL_KD + D_DIFF
D_MAIN = COL_VD + D_DIFF
PROJ_TN = 1024
K_TILE = COL_KD // PROJ_TN
V_TILE = COL_VD // PROJ_TN


def _params(n_axes):
    return pltpu.CompilerParams(dimension_semantics=("arbitrary",) * n_axes,
                                vmem_limit_bytes=VMEM_LIMIT)


def _silu(x):
    return x * jax.nn.sigmoid(x)


def _chunk_of(pos):
    assert CHUNK & (CHUNK - 1) == 0
    return lax.shift_right_logical(pos, jnp.int32(CHUNK.bit_length() - 1))


def _rms_scale(x, eps):
    return lax.rsqrt(jnp.mean(x * x, axis=-1, keepdims=True) + eps)


ROW_TILES = 1024 // LANE
N_HEADS_1024 = 1024 // HEAD_W


def _flat_row(c, halves_split):
    head, half = divmod(c, 2)
    return half * N_HEADS_1024 + head if halves_split else c


def _store_flat(ref, val, halves_split):
    rows = val.shape[0]
    for c in range(ROW_TILES):
        ref[pl.ds(_flat_row(c, halves_split), rows, stride=ROW_TILES), :] = val[:, c * LANE:(c + 1) * LANE]


def _load_flat(ref, rows, col0, width, halves_split):
    parts = [ref[pl.ds(_flat_row(col0 // LANE + c, halves_split), rows, stride=ROW_TILES), :]
             for c in range(width // LANE)]
    return parts[0] if len(parts) == 1 else jnp.concatenate(parts, axis=1)


def _to_flat_halves(a):
    return a.reshape(-1, N_HEADS_1024, 2, LANE).transpose(0, 2, 1, 3).reshape(-1, LANE)


def _from_flat_halves(flat, shape):
    return flat.reshape(-1, 2, N_HEADS_1024, LANE).transpose(0, 2, 1, 3).reshape(shape)


NORM_ROWS = 128


def _norm_rows(x_ref, nw_ref, h_scr):
    tm = x_ref.shape[0]
    step = min(NORM_ROWS, tm)
    for r in range(0, tm, step):
        x = x_ref[r:r + step, :]
        h_scr[r:r + step, :] = (x * _rms_scale(x, NORM_EPS) * nw_ref[...]).astype(BF16)


O_DT = D_SSM + D_CONV
O_Q = O_DT + N_SSM_HEADS
SRC_SEGMENT_OF = ((COL_QD, 0), (COL_KD, 1), (COL_VD, 2), (COL_GD, 3), (COL_QM, 4), (COL_GM, 5))
PREP_COLS = 256


def _prep_source_columns():
    src = list(range(0, O_DT, PREP_COLS))
    for dst, seg in sorted(SRC_SEGMENT_OF):
        assert dst == len(src) * PREP_COLS
        src += list(range(O_Q + seg * D_DIFF, O_Q + (seg + 1) * D_DIFF, PREP_COLS))
    return jnp.asarray(src, jnp.int32)


def _w_prep_kernel(src_ref, w_ref, wdt_ref, main_ref, dt_ref):
    del src_ref
    main_ref[...] = w_ref[0].T.astype(BF16)

    @pl.when(pl.program_id(0) == 0)
    def _():
        lane = lax.broadcasted_iota(jnp.int32, (1, LANE), 1)
        dt_ref[...] = jnp.where(lane < N_SSM_HEADS, wdt_ref[...].T, 0.0).astype(BF16)


def _w_prep(w_in_t, layer):
    return pl.pallas_call(
        _w_prep_kernel,
        grid_spec=pltpu.PrefetchScalarGridSpec(
            num_scalar_prefetch=1,
            grid=(D_MAIN // PREP_COLS,),
            in_specs=[
                pl.BlockSpec((pl.Element(1), pl.Element(PREP_COLS), pl.Element(D_MODEL)),
                             lambda j, src: (layer, pl.multiple_of(src[j], N_SSM_HEADS), 0)),
                pl.BlockSpec((None, LANE, D_MODEL), lambda j, src: (layer, O_DT // LANE, 0)),
            ],
            out_specs=[
                pl.BlockSpec((D_MODEL, PREP_COLS), lambda j, src: (0, j)),
                pl.BlockSpec((D_MODEL, LANE), lambda j, src: (0, 0)),
            ],
        ),
        out_shape=[
            jax.ShapeDtypeStruct((D_MODEL, D_MAIN), BF16),
            jax.ShapeDtypeStruct((D_MODEL, LANE), BF16),
        ],
        compiler_params=_params(1),
        name="w_prep",
    )(_prep_source_columns(), w_in_t, w_in_t)


CAST_ROWS = 512


def _cast_kernel(w_ref, o_ref):
    o_ref[...] = w_ref[...].astype(BF16)


def _layer_bf16(w, layer):
    _, rows, cols = w.shape
    return pl.pallas_call(
        _cast_kernel,
        grid=(rows // CAST_ROWS,),
        in_specs=[pl.BlockSpec((None, CAST_ROWS, cols), lambda i: (layer, i, 0))],
        out_specs=pl.BlockSpec((CAST_ROWS, cols), lambda i: (i, 0)),
        out_shape=jax.ShapeDtypeStruct((rows, cols), BF16),
        compiler_params=_params(1),
        name="cast_bf16",
    )(w)


def _in_proj_kernel(x_ref, nw_ref, w_ref, wdt_ref, *refs):
    main_ref, k_ref, v_ref, dt_ref, h_scr = refs[-5:]
    j = pl.program_id(1)

    @pl.when(j == 0)
    def _():
        _norm_rows(x_ref, nw_ref, h_scr)
        dt_ref[...] = jnp.dot(h_scr[...], wdt_ref[...], preferred_element_type=F32)

    acc = jnp.dot(h_scr[...], w_ref[...], preferred_element_type=F32)
    main_ref[...] = acc.astype(BF16)

    @pl.when(j == K_TILE)
    def _():
        _store_flat(k_ref, acc, halves_split=False)

    @pl.when(j == V_TILE)
    def _():
        _store_flat(v_ref, acc, halves_split=True)


def _in_proj(x2d, norm_w, w_main, w_dt, tm, layer, depth, kv_all):
    m = x2d.shape[0]
    ni = m // tm
    grid = (ni, D_MAIN // PROJ_TN)
    kv_spec = pl.BlockSpec((tm * ROW_TILES, LANE), lambda i, j: (layer * ni + i, 0))
    kv_shape = jax.ShapeDtypeStruct((depth * m * ROW_TILES, LANE), F32)
    in_specs = [
        pl.BlockSpec((tm, D_MODEL), lambda i, j: (i, 0)),
        pl.BlockSpec((1, D_MODEL), lambda i, j: (0, 0)),
        pl.BlockSpec((D_MODEL, PROJ_TN), lambda i, j: (0, j)),
        pl.BlockSpec((D_MODEL, LANE), lambda i, j: (0, 0)),
    ]
    operands = [x2d, norm_w, w_main, w_dt]
    aliases = {}
    if kv_all is not None:
        in_specs += [pl.BlockSpec(memory_space=pl.ANY)] * 2
        aliases = {len(operands): 1, len(operands) + 1: 2}
        operands += list(kv_all)
    return pl.pallas_call(
        _in_proj_kernel,
        grid=grid,
        in_specs=in_specs,
        out_specs=[
            pl.BlockSpec((tm, PROJ_TN), lambda i, j: (i, j)),
            kv_spec,
            kv_spec,
            pl.BlockSpec((tm, LANE), lambda i, j: (i, 0)),
        ],
        out_shape=[
            jax.ShapeDtypeStruct((m, D_MAIN), BF16),
            kv_shape,
            kv_shape,
            jax.ShapeDtypeStruct((m, LANE), F32),
        ],
        scratch_shapes=[pltpu.VMEM((tm, D_MODEL), BF16)],
        input_output_aliases=aliases,
        compiler_params=_params(2),
        name="in_proj",
    )(*operands)


def _mem_kv_kernel(x_ref, nw_ref, w_ref, mk_ref, mv_ref, h_scr):
    j = pl.program_id(1)

    @pl.when(j == 0)
    def _():
        _norm_rows(x_ref, nw_ref, h_scr)

    acc = jnp.dot(h_scr[...], w_ref[...], preferred_element_type=F32)

    @pl.when(j == 0)
    def _():
        _store_flat(mk_ref, acc, halves_split=True)

    @pl.when(j == 1)
    def _():
        _store_flat(mv_ref, acc, halves_split=True)


def _mem_kv(mem2d, norm_w, w_kv, tm):
    m = mem2d.shape[0]
    return pl.pallas_call(
        _mem_kv_kernel,
        grid=(m // tm, 2),
        in_specs=[
            pl.BlockSpec((tm, D_MODEL), lambda i, j: (i, 0)),
            pl.BlockSpec((1, D_MODEL), lambda i, j: (0, 0)),
            pl.BlockSpec((D_MODEL, D_MEM), lambda i, j: (0, j)),
        ],
        out_specs=[
            pl.BlockSpec((tm * ROW_TILES, LANE), lambda i, j: (i, 0)),
            pl.BlockSpec((tm * ROW_TILES, LANE), lambda i, j: (i, 0)),
        ],
        out_shape=[jax.ShapeDtypeStruct((m * ROW_TILES, LANE), F32)] * 2,
        scratch_shapes=[pltpu.VMEM((tm, D_MODEL), BF16)],
        compiler_params=_params(2),
        name="mem_kv",
    )(mem2d, norm_w, w_kv)


def _split_bf16(x, n):
    parts = []
    r = x
    for _ in range(n - 1):
        p = r.astype(BF16)
        parts.append(p)
        r = r - p.astype(F32)
    parts.append(r.astype(BF16))
    return parts


def _sum_rows(a, n, rows):
    out = a[(n - 1) * rows:n * rows]
    for i in range(n - 2, -1, -1):
        out = out + a[i * rows:(i + 1) * rows]
    return out


def _ssd_kernel(z_ref, xx_ref, xb_ref, xc_ref, dt_ref, hist_ref, h0_ref,
                cw_ref, cb_ref, dtb_ref, alog_ref, dskip_ref, nw_ref, e_ref,
                y_ref, nconv_ref, hnew_ref, xp_scr, st_scr, y_scr, *, q):
    c = pl.program_id(1)
    nc = pl.num_programs(1)
    hist_row = SUBLANE - (CONV_W - 1)

    @pl.when(c == 0)
    def _():
        xp_scr[hist_row:SUBLANE, :] = hist_ref[0]
        for g in range(N_SSM_GROUPS):
            hg = h0_ref[0, g * HEADS_PER_GROUP:(g + 1) * HEADS_PER_GROUP]
            st_scr[g] = hg.reshape(D_GROUP, D_STATE).T

    xp_scr[SUBLANE:SUBLANE + q, 0:D_SSM] = xx_ref[...].astype(F32)
    xp_scr[SUBLANE:SUBLANE + q, D_SSM:D_SSM + D_BC] = xb_ref[...].astype(F32)
    xp_scr[SUBLANE:SUBLANE + q, D_SSM + D_BC:D_CONV] = xc_ref[...].astype(F32)
    acc = cb_ref[...] + xp_scr[hist_row:hist_row + q, :] * cw_ref[0:1, :]
    for j in range(1, CONV_W):
        acc = acc + xp_scr[hist_row + j:hist_row + j + q, :] * cw_ref[j:j + 1, :]
    xbc = _silu(acc)
    xs = xbc[:, 0:D_SSM]
    xs_bf = xs.astype(BF16)
    bm = xbc[:, D_SSM:D_SSM + D_BC].astype(BF16)
    cm = xbc[:, D_SSM + D_BC:D_CONV].astype(BF16)

    @pl.when(c == nc - 1)
    def _():
        nconv_ref[0] = xp_scr[q + hist_row:q + SUBLANE, :]

    xp_scr[0:SUBLANE, :] = xp_scr[q:q + SUBLANE, :]

    dtv = jax.nn.softplus(dt_ref[...] + dtb_ref[...])
    a = -jnp.exp(alog_ref[...])
    da = dtv * a
    ri = lax.broadcasted_iota(jnp.int32, (q, q), 0)
    ci = lax.broadcasted_iota(jnp.int32, (q, q), 1)
    causal = ri >= ci
    tril = jnp.where(causal, 1.0, 0.0).astype(BF16)
    triu = jnp.where(ri <= ci, 1.0, 0.0).astype(BF16)
    eye = jnp.where(ri == ci, 1.0, 0.0).astype(BF16)
    da3 = jnp.concatenate(_split_bf16(da, 3), axis=1)
    dt3 = jnp.concatenate(_split_bf16(dtv, 3), axis=1)
    cum3 = jnp.dot(tril, da3, preferred_element_type=F32)
    cum = cum3[:, 2 * LANE:3 * LANE] + cum3[:, LANE:2 * LANE] + cum3[:, 0:LANE]
    tn_dims = (((0,), (0,)), ((), ()))
    cum_t = _sum_rows(lax.dot_general(da3, triu, tn_dims, preferred_element_type=F32), 3, LANE)
    dt_t = _sum_rows(lax.dot_general(dt3, eye, tn_dims, preferred_element_type=F32), 3, LANE)

    ecum = jnp.exp(cum)
    dd = dtv * jnp.exp(cum[q - 1:q, :] - cum)
    ex_in = jnp.concatenate(_split_bf16(ecum, 2) + _split_bf16(dd, 2), axis=0)
    ex = jnp.dot(ex_in, e_ref[...], preferred_element_type=F32)
    ecum_e = ex[q:2 * q] + ex[0:q]
    dd_e = ex[3 * q:4 * q] + ex[2 * q:3 * q]
    xdd = (xs * dd_e).astype(BF16)

    lane = lax.broadcasted_iota(jnp.int32, (q, LANE), 1)
    lo_half = lane < SSM_HEAD_DIM
    nt_dims = (((1,), (1,)), ((), ()))
    for g in range(N_SSM_GROUPS):
        bg = bm[:, g * D_STATE:(g + 1) * D_STATE]
        cg = cm[:, g * D_STATE:(g + 1) * D_STATE]
        gs = slice(g * D_GROUP, (g + 1) * D_GROUP)
        cbg = lax.dot_general(cg, bg, nt_dims, preferred_element_type=F32)
        st = st_scr[g]
        y_off = jnp.dot(cg, st.astype(BF16), preferred_element_type=F32) * ecum_e[:, gs]
        st_scr[g] = st * ecum_e[q - 1:q, gs] + lax.dot_general(
            bg, xdd[:, gs], tn_dims, preferred_element_type=F32)
        for t in range(HEADS_PER_GROUP // 2):
            ws = []
            for h in (g * HEADS_PER_GROUP + 2 * t, g * HEADS_PER_GROUP + 2 * t + 1):
                seg = cum[:, h:h + 1] - cum_t[h:h + 1, :]
                lmat = jnp.exp(jnp.where(causal, seg, -jnp.inf))
                ws.append((cbg * lmat * dt_t[h:h + 1, :]).astype(BF16))
            cs = slice(g * D_GROUP + t * LANE, g * D_GROUP + (t + 1) * LANE)
            xpair = xs_bf[:, cs]
            zero = jnp.zeros_like(xpair)
            y_diag = (jnp.dot(ws[0], jnp.where(lo_half, xpair, zero), preferred_element_type=F32)
                      + jnp.dot(ws[1], jnp.where(lo_half, zero, xpair), preferred_element_type=F32))
            y_scr[:, cs] = (y_diag + y_off[:, t * LANE:(t + 1) * LANE]) + xs[:, cs] * dskip_ref[:, cs]

    for g in range(N_SSM_GROUPS):
        gs = slice(g * D_GROUP, (g + 1) * D_GROUP)
        yg = y_scr[:, gs] * _silu(z_ref[:, gs].astype(F32))
        y_ref[:, gs] = (yg * _rms_scale(yg, NORM_EPS) * nw_ref[:, gs]).astype(BF16)

    @pl.when(c == nc - 1)
    def _():
        for g in range(N_SSM_GROUPS):
            hnew_ref[0, g * HEADS_PER_GROUP:(g + 1) * HEADS_PER_GROUP] = (
                st_scr[g].T.reshape(HEADS_PER_GROUP, SSM_HEAD_DIM, D_STATE))


def _ssd(main, dt, conv_hist, h0, b_off, lp, bsz, length, q):
    nc = length // q
    m = bsz * length
    row = lambda b, c: b * nc + c
    return pl.pallas_call(
        functools.partial(_ssd_kernel, q=q),
        grid=(bsz, nc),
        in_specs=[
            pl.BlockSpec((q, D_SSM), lambda b, c: (row(b, c), COL_Z // D_SSM)),
            pl.BlockSpec((q, D_SSM), lambda b, c: (row(b, c), COL_X // D_SSM)),
            pl.BlockSpec((q, D_BC), lambda b, c: (row(b, c), COL_B // D_BC)),
            pl.BlockSpec((q, D_BC), lambda b, c: (row(b, c), COL_C // D_BC)),
            pl.BlockSpec((q, LANE), lambda b, c: (row(b, c), 0)),
            pl.BlockSpec((1, CONV_W - 1, D_CONV), lambda b, c: (b_off + b, 0, 0)),
            pl.BlockSpec((1, N_SSM_HEADS, SSM_HEAD_DIM, D_STATE), lambda b, c: (b_off + b, 0, 0, 0)),
            pl.BlockSpec((CONV_W, D_CONV), lambda b, c: (0, 0)),
            pl.BlockSpec((1, D_CONV), lambda b, c: (0, 0)),
            pl.BlockSpec((1, LANE), lambda b, c: (0, 0)),
            pl.BlockSpec((1, LANE), lambda b, c: (0, 0)),
            pl.BlockSpec((1, D_SSM), lambda b, c: (0, 0)),
            pl.BlockSpec((1, D_SSM), lambda b, c: (0, 0)),
            pl.BlockSpec((LANE, D_SSM), lambda b, c: (0, 0)),
        ],
        out_specs=[
            pl.BlockSpec((q, D_SSM), lambda b, c: (row(b, c), 0)),
            pl.BlockSpec((1, CONV_W - 1, D_CONV), lambda b, c: (b, 0, 0)),
            pl.BlockSpec((1, N_SSM_HEADS, SSM_HEAD_DIM, D_STATE), lambda b, c: (b, 0, 0, 0)),
        ],
        out_shape=[
            jax.ShapeDtypeStruct((m, D_SSM), BF16),
            jax.ShapeDtypeStruct((bsz, CONV_W - 1, D_CONV), F32),
            jax.ShapeDtypeStruct((bsz, N_SSM_HEADS, SSM_HEAD_DIM, D_STATE), F32),
        ],
        scratch_shapes=[
            pltpu.VMEM((q + SUBLANE, D_CONV), F32),
            pltpu.VMEM((N_SSM_GROUPS, D_STATE, D_GROUP), F32),
            pltpu.VMEM((q, D_SSM), F32),
        ],
        compiler_params=_params(2),
        name="ssd",
    )(main, main, main, main, dt, conv_hist, h0,
      lp["conv_w"], lp["conv_b"], lp["dt_bias"], lp["a_log"], lp["d_skip_e"], lp["ssm_norm_w"],
      lp["expand"])


def _lambda(lq1, lk1, lq2, lk2, lambda_init):
    return (jnp.exp(jnp.sum(lq1[...] * lk1[...], axis=-1, keepdims=True))
            - jnp.exp(jnp.sum(lq2[...] * lk2[...], axis=-1, keepdims=True)) + lambda_init)


def _diff_finish(o1, o2, lam, sw_ref, g, lambda_init):
    o = o1 - lam * o2
    on = (o * _rms_scale(o, SUBLN_EPS) * sw_ref[...]) * (1.0 - lambda_init)
    return (on * _silu(g.astype(F32))).astype(BF16)


ATTN_ROWS = 32
ATTN_BAND_ROWS = 1024


def _attn_update(q_ref, k_ref, v_ref, m_scr, l_scr, acc_scr, mp, q0, n, kw, diagonal):
    c = (DIFF_HEAD_DIM ** -0.5) * math.log2(math.e)
    cols = slice(mp * DIFF_HEAD_DIM, (mp + 1) * DIFF_HEAD_DIM)
    rows = slice(q0, q0 + n)
    s = lax.dot_general(q_ref[rows, cols], k_ref[0:kw, cols], (((1,), (1,)), ((), ())),
                        preferred_element_type=F32)
    m_prev = m_scr[mp, rows]
    step = min(ATTN_ROWS, n)
    assert CHUNK % step == 0
    ps, m_news, p_sums = [], [], []
    for r in range(0, n, step):
        if diagonal:
            lim = ((q0 + r) // CHUNK + 1) * CHUNK
            w = min(kw, -(-lim // LANE) * LANE)
            sr = s[r:r + step, 0:w]
            if lim < w:
                sr = jnp.where(lax.broadcasted_iota(jnp.int32, (1, w), 1) < lim, sr, -jnp.inf)
        else:
            w = kw
            sr = s[r:r + step]
        mn = jnp.maximum(m_prev[r:r + step], jnp.max(sr, axis=-1, keepdims=True))
        p = jnp.exp2((sr - mn) * c)
        pb = p.astype(BF16)
        ps.append(pb if w == kw else jnp.concatenate([pb, jnp.zeros((step, kw - w), BF16)], axis=1))
        m_news.append(mn)
        p_sums.append(jnp.sum(p, axis=-1, keepdims=True))
    m_new = jnp.concatenate(m_news, axis=0)
    alpha = jnp.exp2((m_prev - m_new) * c)
    l_scr[mp, rows] = alpha * l_scr[mp, rows] + jnp.concatenate(p_sums, axis=0)
    acc_scr[mp, rows] = alpha * acc_scr[mp, rows] + jnp.dot(
        jnp.concatenate(ps, axis=0), v_ref[0:kw, :], preferred_element_type=F32)
    m_scr[mp, rows] = m_new


def _diff_attn_kernel(q_ref, k_ref, v_ref, g_ref, lq1, lk1, lq2, lk2, sw_ref, o_ref,
                      m_scr, l_scr, acc_scr, *, tq, tk, lambda_init):
    qi = pl.program_id(2)
    ki = pl.program_id(3)
    update = functools.partial(_attn_update, q_ref, k_ref, v_ref, m_scr, l_scr, acc_scr)
    assert tq == tk and tq % CHUNK == 0

    @pl.when(ki == 0)
    def _():
        m_scr[...] = jnp.full(m_scr.shape, -jnp.inf, F32)
        l_scr[...] = jnp.zeros(l_scr.shape, F32)
        acc_scr[...] = jnp.zeros(acc_scr.shape, F32)

    band = min(ATTN_BAND_ROWS, tq)

    @pl.when(ki < qi)
    def _():
        for q0 in range(0, tq, band):
            for mp in range(2):
                update(mp=mp, q0=q0, n=band, kw=tk, diagonal=False)

    @pl.when(ki == qi)
    def _():
        for q0 in range(0, tq, band):
            for mp in range(2):
                update(mp=mp, q0=q0, n=band, kw=q0 + band, diagonal=True)
        lam = _lambda(lq1, lk1, lq2, lk2, lambda_init)
        o_ref[...] = _diff_finish(acc_scr[0] / l_scr[0], acc_scr[1] / l_scr[1], lam, sw_ref, g_ref[...],
                                  lambda_init)


def _diff_attn_prompt(main, lp, bsz, length, lambda_init, tq):
    tk = tq
    nq = length // tq
    m = bsz * length
    vec = pl.BlockSpec((1, DIFF_HEAD_DIM), lambda b, h, qi, ki: (0, 0))
    return pl.pallas_call(
        functools.partial(_diff_attn_kernel, tq=tq, tk=tk, lambda_init=lambda_init),
        grid=(bsz, N_DIFF_HEADS, nq, nq),
        in_specs=[
            pl.BlockSpec((tq, HEAD_W), lambda b, h, qi, ki: (b * nq + qi, COL_QD // HEAD_W + h)),
            pl.BlockSpec((tk, HEAD_W), lambda b, h, qi, ki: (b * nq + jnp.minimum(ki, qi), COL_KD // HEAD_W + h)),
            pl.BlockSpec((tk, HEAD_W), lambda b, h, qi, ki: (b * nq + jnp.minimum(ki, qi), COL_VD // HEAD_W + h)),
            pl.BlockSpec((tq, HEAD_W), lambda b, h, qi, ki: (b * nq + qi, COL_GD // HEAD_W + h)),
            vec, vec, vec, vec,
            pl.BlockSpec((1, HEAD_W), lambda b, h, qi, ki: (0, 0)),
        ],
        out_specs=pl.BlockSpec((tq, HEAD_W), lambda b, h, qi, ki: (b * nq + qi, h)),
        out_shape=jax.ShapeDtypeStruct((m, D_DIFF), BF16),
        scratch_shapes=[
            pltpu.VMEM((2, tq, 1), F32),
            pltpu.VMEM((2, tq, 1), F32),
            pltpu.VMEM((2, tq, HEAD_W), F32),
        ],
        compiler_params=_params(4),
        name="diff_attn_prompt",
    )(main, main, main, main, lp["lambda_q1"], lp["lambda_k1"], lp["lambda_q2"], lp["lambda_k2"],
      lp["subln_w"])


def _diff_attn_decode_kernel(q_ref, kn_ref, vn_ref, g_ref, kp_ref, vp_ref, lq1, lk1, lq2, lk2, sw_ref,
                             o_ref, *, past, lambda_init):
    scale = DIFF_HEAD_DIM ** -0.5
    lq = q_ref.shape[0]
    nt_dims = (((1,), (1,)), ((), ()))
    mask_p = (_chunk_of(lax.broadcasted_iota(jnp.int32, (lq, past), 1))
              <= _chunk_of(past + lax.broadcasted_iota(jnp.int32, (lq, past), 0)))
    mask_n = (_chunk_of(past + lax.broadcasted_iota(jnp.int32, (lq, lq), 1))
              <= _chunk_of(past + lax.broadcasted_iota(jnp.int32, (lq, lq), 0)))
    lam = _lambda(lq1, lk1, lq2, lk2, lambda_init)
    for h in range(N_DIFF_HEADS):
        hs = slice(h * HEAD_W, (h + 1) * HEAD_W)
        vp = _load_flat(vp_ref, past, h * HEAD_W, HEAD_W, halves_split=True).astype(BF16)
        vn = vn_ref[:, hs]
        outs = []
        for mp in range(2):
            c0 = h * HEAD_W + mp * DIFF_HEAD_DIM
            qm = q_ref[:, c0:c0 + DIFF_HEAD_DIM]
            kp = _load_flat(kp_ref, past, c0, DIFF_HEAD_DIM, halves_split=False).astype(BF16)
            sp = lax.dot_general(qm, kp, nt_dims, preferred_element_type=F32) * scale
            sn = lax.dot_general(qm, kn_ref[:, c0:c0 + DIFF_HEAD_DIM], nt_dims,
                                 preferred_element_type=F32) * scale
            sp = jnp.where(mask_p, sp, -jnp.inf)
            sn = jnp.where(mask_n, sn, -jnp.inf)
            mx = jnp.maximum(jnp.max(sp, axis=-1, keepdims=True), jnp.max(sn, axis=-1, keepdims=True))
            pp = jnp.exp(sp - mx)
            pn = jnp.exp(sn - mx)
            denom = jnp.sum(pp, axis=-1, keepdims=True) + jnp.sum(pn, axis=-1, keepdims=True)
            o = (jnp.dot(pp.astype(BF16), vp, preferred_element_type=F32)
                 + jnp.dot(pn.astype(BF16), vn, preferred_element_type=F32))
            outs.append(o / denom)
        o_ref[:, hs] = _diff_finish(outs[0], outs[1], lam, sw_ref, g_ref[:, hs], lambda_init)


def _diff_attn_decode(main, k_past, v_past, b_off, lp, bsz, length, past, lambda_init):
    m = bsz * length
    vec = pl.BlockSpec((1, DIFF_HEAD_DIM), lambda b: (0, 0))
    return pl.pallas_call(
        functools.partial(_diff_attn_decode_kernel, past=past, lambda_init=lambda_init),
        grid=(bsz,),
        in_specs=[
            pl.BlockSpec((length, D_DIFF), lambda b: (b, COL_QD // D_DIFF)),
            pl.BlockSpec((length, D_DIFF), lambda b: (b, COL_KD // D_DIFF)),
            pl.BlockSpec((length, D_DIFF), lambda b: (b, COL_VD // D_DIFF)),
            pl.BlockSpec((length, D_DIFF), lambda b: (b, COL_GD // D_DIFF)),
            pl.BlockSpec((past * ROW_TILES, LANE), lambda b: (b_off + b, 0)),
            pl.BlockSpec((past * ROW_TILES, LANE), lambda b: (b_off + b, 0)),
            vec, vec, vec, vec,
            pl.BlockSpec((1, HEAD_W), lambda b: (0, 0)),
        ],
        out_specs=pl.BlockSpec((length, D_DIFF), lambda b: (b, 0)),
        out_shape=jax.ShapeDtypeStruct((m, D_DIFF), BF16),
        compiler_params=_params(1),
        name="diff_attn_decode",
    )(main, main, main, main, k_past, v_past,
      lp["lambda_q1"], lp["lambda_k1"], lp["lambda_q2"], lp["lambda_k2"], lp["subln_w"])


def _mem_attn_kernel(q_ref, g_ref, mk_ref, mv_ref, o_ref, *, n_mem):
    scale = MEM_HEAD_DIM ** -0.5
    for h in range(N_MEM_HEADS):
        hs = slice(h * MEM_HEAD_DIM, (h + 1) * MEM_HEAD_DIM)
        mk = _load_flat(mk_ref, n_mem, h * MEM_HEAD_DIM, MEM_HEAD_DIM, halves_split=True).astype(BF16)
        mv = _load_flat(mv_ref, n_mem, h * MEM_HEAD_DIM, MEM_HEAD_DIM, halves_split=True).astype(BF16)
        s = lax.dot_general(q_ref[:, hs], mk, (((1,), (1,)), ((), ())), preferred_element_type=F32) * scale
        p = jnp.exp(s - jnp.max(s, axis=-1, keepdims=True))
        pr = p / jnp.sum(p, axis=-1, keepdims=True)
        o = jnp.dot(pr.astype(BF16), mv, preferred_element_type=F32)
        o_ref[:, hs] = (o * _silu(g_ref[:, hs].astype(F32))).astype(BF16)


def _mem_attn(main, mk, mv, b_off, bsz, length, n_mem, tq):
    nq = length // tq
    m = bsz * length
    return pl.pallas_call(
        functools.partial(_mem_attn_kernel, n_mem=n_mem),
        grid=(bsz, nq),
        in_specs=[
            pl.BlockSpec((tq, D_MEM), lambda b, qi: (b * nq + qi, COL_QM // D_MEM)),
            pl.BlockSpec((tq, D_MEM), lambda b, qi: (b * nq + qi, COL_GM // D_MEM)),
            pl.BlockSpec((n_mem * ROW_TILES, LANE), lambda b, qi: (b_off + b, 0)),
            pl.BlockSpec((n_mem * ROW_TILES, LANE), lambda b, qi: (b_off + b, 0)),
        ],
        out_specs=pl.BlockSpec((tq, D_MEM), lambda b, qi: (b * nq + qi, 0)),
        out_shape=jax.ShapeDtypeStruct((m, D_MEM), BF16),
        compiler_params=_params(2),
        name="mem_attn",
    )(main, main, mk, mv)


def _out_proj_kernel(ys_ref, yd_ref, ym_ref, w_ref, x_ref, nw_ref, o_ref, *, nj, tn):
    j = pl.program_id(1)
    acc = (jnp.dot(ys_ref[...], w_ref[0:D_SSM, :], preferred_element_type=F32)
           + jnp.dot(yd_ref[...], w_ref[D_SSM:D_SSM + D_DIFF, :], preferred_element_type=F32)
           + jnp.dot(ym_ref[...], w_ref[D_SSM + D_DIFF:D_MODEL, :], preferred_element_type=F32))
    for t in range(nj):
        @pl.when(j == t)
        def _(t=t):
            o_ref[:, t * tn:(t + 1) * tn] = acc

    @pl.when(j == nj - 1)
    def _():
        tm = o_ref.shape[0]
        step = min(NORM_ROWS, tm)
        for r in range(0, tm, step):
            o = o_ref[r:r + step, :]
            o_ref[r:r + step, :] = x_ref[r:r + step, :] + o * _rms_scale(o, NORM_EPS) * nw_ref[...]


def _out_proj(ys, yd, ym, w_out, x2d, norm_w, tm, tn):
    m = x2d.shape[0]
    nj = D_MODEL // tn
    return pl.pallas_call(
        functools.partial(_out_proj_kernel, nj=nj, tn=tn),
        grid=(m // tm, nj),
        in_specs=[
            pl.BlockSpec((tm, D_SSM), lambda i, j: (i, 0)),
            pl.BlockSpec((tm, D_DIFF), lambda i, j: (i, 0)),
            pl.BlockSpec((tm, D_MEM), lambda i, j: (i, 0)),
            pl.BlockSpec((D_MODEL, tn), lambda i, j: (0, j)),
            pl.BlockSpec((tm, D_MODEL), lambda i, j: (i, 0)),
            pl.BlockSpec((1, D_MODEL), lambda i, j: (0, 0)),
        ],
        out_specs=pl.BlockSpec((tm, D_MODEL), lambda i, j: (i, 0)),
        out_shape=jax.ShapeDtypeStruct((m, D_MODEL), F32),
        compiler_params=_params(2),
        name="out_proj",
    )(ys, yd, ym, w_out, x2d, norm_w)


def _mixer_layer(x2d, bsz, length, layer, depth, kv_all, b_off, mem_off, conv_hist, h0, kv_past, past, mk,
                 mv, n_mem, lp, lambda_init, tiles):
    main, k32, v32, dt = _in_proj(x2d, lp["norm_pre_w"], lp["w_main"], lp["w_dt"], tiles["proj_tm"],
                                  layer, depth, kv_all)
    ys, new_conv, h_new = _ssd(main, dt, conv_hist, h0, b_off, lp, bsz, length, tiles["ssd_q"])
    if kv_past is None:
        yd = _diff_attn_prompt(main, lp, bsz, length, lambda_init, tiles["attn_tq"])
    else:
        yd = _diff_attn_decode(main, kv_past[0], kv_past[1], b_off, lp, bsz, length, past, lambda_init)
    ym = _mem_attn(main, mk, mv, mem_off, bsz, length, n_mem, tiles["mem_tq"])
    x_new = _out_proj(ys, yd, ym, lp["w_out"], x2d, lp["norm_post_w"], tiles["out_tm"], tiles["out_tn"])
    return x_new, new_conv, h_new, (k32, v32)


def _layer_params(l, norm_pre_w, norm_post_w, w_in_t, conv_w, conv_b, dt_bias, a_log, d_skip,
                  ssm_norm_w, lambda_q1, lambda_k1, lambda_q2, lambda_k2, subln_w, mem_norm_w,
                  w_mem_kv, w_out):
    w_main, w_dt = _w_prep(w_in_t, l)
    pad_heads = lambda v: jnp.pad(v, (0, LANE - N_SSM_HEADS)).reshape(1, LANE)
    head_of_channel = jnp.arange(D_SSM) // SSM_HEAD_DIM
    expand = (jnp.arange(LANE)[:, None] == head_of_channel[None, :]).astype(BF16)
    return {
        "norm_pre_w": norm_pre_w[l].reshape(1, D_MODEL),
        "norm_post_w": norm_post_w[l].reshape(1, D_MODEL),
        "w_main": w_main,
        "w_dt": w_dt,
        "conv_w": conv_w[l],
        "conv_b": conv_b[l].reshape(1, D_CONV),
        "dt_bias": pad_heads(dt_bias[l]),
        "a_log": pad_heads(a_log[l]),
        "d_skip_e": jnp.repeat(d_skip[l], SSM_HEAD_DIM).reshape(1, D_SSM),
        "ssm_norm_w": ssm_norm_w[l].reshape(1, D_SSM),
        "expand": expand,
        "lambda_q1": lambda_q1[l].reshape(1, DIFF_HEAD_DIM),
        "lambda_k1": lambda_k1[l].reshape(1, DIFF_HEAD_DIM),
        "lambda_q2": lambda_q2[l].reshape(1, DIFF_HEAD_DIM),
        "lambda_k2": lambda_k2[l].reshape(1, DIFF_HEAD_DIM),
        "subln_w": subln_w[l].reshape(1, HEAD_W),
        "mem_norm_w": mem_norm_w[l].reshape(1, D_MODEL),
        "w_mem_kv": _layer_bf16(w_mem_kv, l),
        "w_out": _layer_bf16(w_out, l),
    }


def _tiles(bsz, length):
    m = bsz * length
    return {
        "proj_tm": min(512, m),
        "ssd_q": min(128, length),
        "attn_tq": min(1024, length),
        "mem_tq": min(1024, length),
        "out_tm": min(512, m),
        "out_tn": 512,
    }


def kernel(x_prompt, x_sample, mem_prompt, cache_conv, state_ssm, cache_k, cache_v, cache_mem_k,
           cache_mem_v, norm_pre_w, norm_post_w, w_in, conv_w, conv_b, dt_bias, a_log, d_skip,
           ssm_norm_w, lambda_q1, lambda_k1, lambda_q2, lambda_k2, subln_w, mem_norm_w, w_mem_kv,
           w_out):
    depth = w_in.shape[0]
    bp, lp_, _ = x_prompt.shape
    bs, ls, _ = x_sample.shape
    n_mem = mem_prompt.shape[1]
    past = cache_k.shape[2]
    xp = x_prompt.reshape(bp * lp_, D_MODEL)
    xs = x_sample.reshape(bs * ls, D_MODEL)
    mem2d = mem_prompt.reshape(bp * n_mem, D_MODEL)
    tiles_p = _tiles(bp, lp_)
    tiles_s = _tiles(bs, ls)
    conv0 = jnp.zeros((bp, CONV_W - 1, D_CONV), F32)
    h_zero = jnp.zeros((bp, N_SSM_HEADS, SSM_HEAD_DIM, D_STATE), F32)
    conv_all = cache_conv.reshape(depth * bs, CONV_W - 1, D_CONV)
    ssm_all = state_ssm.reshape(depth * bs, N_SSM_HEADS, SSM_HEAD_DIM, D_STATE)
    kv_all = (cache_k.reshape(-1, LANE), _to_flat_halves(cache_v))
    mem_k_all = _to_flat_halves(cache_mem_k)
    mem_v_all = _to_flat_halves(cache_mem_v)
    outs = [[] for _ in range(6)]
    kv_p = kv_s = None
    w_in_t = jnp.swapaxes(w_in, 1, 2)
    for l in range(depth):
        lp = _layer_params(l, norm_pre_w, norm_post_w, w_in_t, conv_w, conv_b, dt_bias, a_log, d_skip,
                           ssm_norm_w, lambda_q1, lambda_k1, lambda_q2, lambda_k2, subln_w,
                           mem_norm_w, w_mem_kv, w_out)
        lambda_init = 0.8 - 0.6 * math.exp(-0.3 * l)
        mk_p, mv_p = _mem_kv(mem2d, lp["mem_norm_w"], lp["w_mem_kv"], min(512, bp * n_mem))
        xp, c_p, h_p, kv_p = _mixer_layer(xp, bp, lp_, l, depth, kv_p, 0, 0, conv0, h_zero, None, 0, mk_p,
                                          mv_p, n_mem, lp, lambda_init, tiles_p)
        xs, c_s, h_s, kv_s = _mixer_layer(xs, bs, ls, l, depth, kv_s, l * bs, l * bs, conv_all, ssm_all,
                                          kv_all, past, mem_k_all, mem_v_all, n_mem, lp, lambda_init,
                                          tiles_s)
        for lst, val in zip(outs, (c_p, h_p, mk_p, mv_p, c_s, h_s)):
            lst.append(val)
    st = [jnp.stack(o) for o in outs]
    return (
        xp.reshape(bp, lp_, D_MODEL),
        xs.reshape(bs, ls, D_MODEL),
        st[0],
        st[1],
        kv_p[0].reshape(depth, bp, lp_, N_DIFF_HEADS, 2, DIFF_HEAD_DIM),
        _from_flat_halves(kv_p[1], (depth, bp, lp_, N_DIFF_HEADS, 2 * DIFF_HEAD_DIM)),
        _from_flat_halves(st[2], (depth, bp, n_mem, N_MEM_HEADS, MEM_HEAD_DIM)),
        _from_flat_halves(st[3], (depth, bp, n_mem, N_MEM_HEADS, MEM_HEAD_DIM)),
        st[4],
        st[5],
        kv_s[0].reshape(depth, bs, ls, N_DIFF_HEADS, 2, DIFF_HEAD_DIM),
        _from_flat_halves(kv_s[1], (depth, bs, ls, N_DIFF_HEADS, 2 * DIFF_HEAD_DIM)),
    )
```

```python
import functools
import math

import jax
import jax.numpy as jnp
from jax import lax
from jax.experimental import pallas as pl
from jax.experimental.pallas import tpu as pltpu

F32 = jnp.float32
BF16 = jnp.bfloat16

D_MODEL = 4096
D_SSM = 2048
SSM_HEAD_DIM = 64
N_SSM_HEADS = 32
N_SSM_GROUPS = 4
HEADS_PER_GROUP = N_SSM_HEADS // N_SSM_GROUPS
D_GROUP = D_SSM // N_SSM_GROUPS
D_STATE = 128
CONV_W = 4
D_BC = N_SSM_GROUPS * D_STATE
D_CONV = D_SSM + 2 * D_BC
D_DIFF = 1024
N_DIFF_HEADS = 4
DIFF_HEAD_DIM = 128
D_MEM = 1024
N_MEM_HEADS = 4
MEM_HEAD_DIM = 256
CHUNK = 64
NORM_EPS = 1e-6
SUBLN_EPS = 1e-5

LANE = 128
SUBLANE = 8
VMEM_LIMIT = 56 * 1024 * 1024

HEAD_W = 2 * DIFF_HEAD_DIM
COL_Z = 0
COL_X = D_SSM
COL_B = COL_X + D_SSM
COL_C = COL_B + D_BC
COL_QD = COL_C + D_BC
COL_GD = COL_QD + D_DIFF
COL_QM = COL_GD + D_DIFF
COL_GM = COL_QM + D_MEM
COL_KD = COL_GM + D_MEM
COL_VD = COL_KD + D_DIFF
D_MAIN = COL_VD + D_DIFF
PROJ_TN = 1024
K_TILE = COL_KD // PROJ_TN
V_TILE = COL_VD // PROJ_TN


def _params(n_axes):
    return pltpu.CompilerParams(dimension_semantics=("arbitrary",) * n_axes,
                                vmem_limit_bytes=VMEM_LIMIT)


def _silu(x):
    return x * jax.nn.sigmoid(x)


def _chunk_of(pos):
    assert CHUNK & (CHUNK - 1) == 0
    return lax.shift_right_logical(pos, jnp.int32(CHUNK.bit_length() - 1))


def _rms_scale(x, eps):
    return lax.rsqrt(jnp.mean(x * x, axis=-1, keepdims=True) + eps)


ROW_TILES = 1024 // LANE
N_HEADS_1024 = 1024 // HEAD_W


def _flat_row(c, halves_split):
    head, half = divmod(c, 2)
    return half * N_HEADS_1024 + head if halves_split else c


def _store_flat(ref, val, halves_split):
    rows = val.shape[0]
    for c in range(ROW_TILES):
        ref[pl.ds(_flat_row(c, halves_split), rows, stride=ROW_TILES), :] = val[:, c * LANE:(c + 1) * LANE]


def _load_flat(ref, rows, col0, width, halves_split):
    parts = [ref[pl.ds(_flat_row(col0 // LANE + c, halves_split), rows, stride=ROW_TILES), :]
             for c in range(width // LANE)]
    return parts[0] if len(parts) == 1 else jnp.concatenate(parts, axis=1)


def _to_flat_halves(a):
    return a.reshape(-1, N_HEADS_1024, 2, LANE).transpose(0, 2, 1, 3).reshape(-1, LANE)


def _from_flat_halves(flat, shape):
    return flat.reshape(-1, 2, N_HEADS_1024, LANE).transpose(0, 2, 1, 3).reshape(shape)


NORM_ROWS = 128


def _norm_rows(x_ref, nw_ref, h_scr):
    tm = x_ref.shape[0]
    step = min(NORM_ROWS, tm)
    for r in range(0, tm, step):
        x = x_ref[r:r + step, :]
        h_scr[r:r + step, :] = (x * _rms_scale(x, NORM_EPS) * nw_ref[...]).astype(BF16)


O_DT = D_SSM + D_CONV
O_Q = O_DT + N_SSM_HEADS
SRC_SEGMENT_OF = ((COL_QD, 0), (COL_KD, 1), (COL_VD, 2), (COL_GD, 3), (COL_QM, 4), (COL_GM, 5))
PREP_COLS = 256


def _prep_source_columns():
    src = list(range(0, O_DT, PREP_COLS))
    for dst, seg in sorted(SRC_SEGMENT_OF):
        assert dst == len(src) * PREP_COLS
        src += list(range(O_Q + seg * D_DIFF, O_Q + (seg + 1) * D_DIFF, PREP_COLS))
    return jnp.asarray(src, jnp.int32)


def _w_prep_kernel(src_ref, w_ref, wdt_ref, main_ref, dt_ref):
    del src_ref
    main_ref[...] = w_ref[0].T.astype(BF16)

    @pl.when(pl.program_id(0) == 0)
    def _():
        lane = lax.broadcasted_iota(jnp.int32, (1, LANE), 1)
        dt_ref[...] = jnp.where(lane < N_SSM_HEADS, wdt_ref[...].T, 0.0).astype(BF16)


def _w_prep(w_in_t, layer):
    return pl.pallas_call(
        _w_prep_kernel,
        grid_spec=pltpu.PrefetchScalarGridSpec(
            num_scalar_prefetch=1,
            grid=(D_MAIN // PREP_COLS,),
            in_specs=[
                pl.BlockSpec((pl.Element(1), pl.Element(PREP_COLS), pl.Element(D_MODEL)),
                             lambda j, src: (layer, pl.multiple_of(src[j], N_SSM_HEADS), 0)),
                pl.BlockSpec((None, LANE, D_MODEL), lambda j, src: (layer, O_DT // LANE, 0)),
            ],
            out_specs=[
                pl.BlockSpec((D_MODEL, PREP_COLS), lambda j, src: (0, j)),
                pl.BlockSpec((D_MODEL, LANE), lambda j, src: (0, 0)),
            ],
        ),
        out_shape=[
            jax.ShapeDtypeStruct((D_MODEL, D_MAIN), BF16),
            jax.ShapeDtypeStruct((D_MODEL, LANE), BF16),
        ],
        compiler_params=_params(1),
        name="w_prep",
    )(_prep_source_columns(), w_in_t, w_in_t)


CAST_ROWS = 512


def _cast_kernel(w_ref, o_ref):
    o_ref[...] = w_ref[...].astype(BF16)


def _layer_bf16(w, layer):
    _, rows, cols = w.shape
    return pl.pallas_call(
        _cast_kernel,
        grid=(rows // CAST_ROWS,),
        in_specs=[pl.BlockSpec((None, CAST_ROWS, cols), lambda i: (layer, i, 0))],
        out_specs=pl.BlockSpec((CAST_ROWS, cols), lambda i: (i, 0)),
        out_shape=jax.ShapeDtypeStruct((rows, cols), BF16),
        compiler_params=_params(1),
        name="cast_bf16",
    )(w)


def _in_proj_kernel(x_ref, nw_ref, w_ref, wdt_ref, *refs):
    main_ref, k_ref, v_ref, dt_ref, h_scr = refs[-5:]
    j = pl.program_id(1)

    @pl.when(j == 0)
    def _():
        _norm_rows(x_ref, nw_ref, h_scr)
        dt_ref[...] = jnp.dot(h_scr[...], wdt_ref[...], preferred_element_type=F32)

    acc = jnp.dot(h_scr[...], w_ref[...], preferred_element_type=F32)
    main_ref[...] = acc.astype(BF16)

    @pl.when(j == K_TILE)
    def _():
        _store_flat(k_ref, acc, halves_split=False)

    @pl.when(j == V_TILE)
    def _():
        _store_flat(v_ref, acc, halves_split=True)


def _in_proj(x2d, norm_w, w_main, w_dt, tm, layer, depth, kv_all):
    m = x2d.shape[0]
    ni = m // tm
    grid = (ni, D_MAIN // PROJ_TN)
    kv_spec = pl.BlockSpec((tm * ROW_TILES, LANE), lambda i, j: (layer * ni + i, 0))
    kv_shape = jax.ShapeDtypeStruct((depth * m * ROW_TILES, LANE), F32)
    in_specs = [
        pl.BlockSpec((tm, D_MODEL), lambda i, j: (i, 0)),
        pl.BlockSpec((1, D_MODEL), lambda i, j: (0, 0)),
        pl.BlockSpec((D_MODEL, PROJ_TN), lambda i, j: (0, j)),
        pl.BlockSpec((D_MODEL, LANE), lambda i, j: (0, 0)),
    ]
    operands = [x2d, norm_w, w_main, w_dt]
    aliases = {}
    if kv_all is not None:
        in_specs += [pl.BlockSpec(memory_space=pl.ANY)] * 2
        aliases = {len(operands): 1, len(operands) + 1: 2}
        operands += list(kv_all)
    return pl.pallas_call(
        _in_proj_kernel,
        grid=grid,
        in_specs=in_specs,
        out_specs=[
            pl.BlockSpec((tm, PROJ_TN), lambda i, j: (i, j)),
            kv_spec,
            kv_spec,
            pl.BlockSpec((tm, LANE), lambda i, j: (i, 0)),
        ],
        out_shape=[
            jax.ShapeDtypeStruct((m, D_MAIN), BF16),
            kv_shape,
            kv_shape,
            jax.ShapeDtypeStruct((m, LANE), F32),
        ],
        scratch_shapes=[pltpu.VMEM((tm, D_MODEL), BF16)],
        input_output_aliases=aliases,
        compiler_params=_params(2),
        name="in_proj",
    )(*operands)


def _mem_kv_kernel(x_ref, nw_ref, w_ref, mk_ref, mv_ref, h_scr):
    j = pl.program_id(1)

    @pl.when(j == 0)
    def _():
        _norm_rows(x_ref, nw_ref, h_scr)

    acc = jnp.dot(h_scr[...], w_ref[...], preferred_element_type=F32)

    @pl.when(j == 0)
    def _():
        _store_flat(mk_ref, acc, halves_split=True)

    @pl.when(j == 1)
    def _():
        _store_flat(mv_ref, acc, halves_split=True)


def _mem_kv(mem2d, norm_w, w_kv, tm):
    m = mem2d.shape[0]
    return pl.pallas_call(
        _mem_kv_kernel,
        grid=(m // tm, 2),
        in_specs=[
            pl.BlockSpec((tm, D_MODEL), lambda i, j: (i, 0)),
            pl.BlockSpec((1, D_MODEL), lambda i, j: (0, 0)),
            pl.BlockSpec((D_MODEL, D_MEM), lambda i, j: (0, j)),
        ],
        out_specs=[
            pl.BlockSpec((tm * ROW_TILES, LANE), lambda i, j: (i, 0)),
            pl.BlockSpec((tm * ROW_TILES, LANE), lambda i, j: (i, 0)),
        ],
        out_shape=[jax.ShapeDtypeStruct((m * ROW_TILES, LANE), F32)] * 2,
        scratch_shapes=[pltpu.VMEM((tm, D_MODEL), BF16)],
        compiler_params=_params(2),
        name="mem_kv",
    )(mem2d, norm_w, w_kv)


def _split_bf16(x, n):
    parts = []
    r = x
    for _ in range(n - 1):
        p = r.astype(BF16)
        parts.append(p)
        r = r - p.astype(F32)
    parts.append(r.astype(BF16))
    return parts


def _sum_rows(a, n, rows):
    out = a[(n - 1) * rows:n * rows]
    for i in range(n - 2, -1, -1):
        out = out + a[i * rows:(i + 1) * rows]
    return out


def _ssd_kernel(z_ref, xx_ref, xb_ref, xc_ref, dt_ref, hist_ref, h0_ref,
                cw_ref, cb_ref, dtb_ref, alog_ref, dskip_ref, nw_ref, e_ref,
                y_ref, nconv_ref, hnew_ref, xp_scr, st_scr, y_scr, *, q):
    c = pl.program_id(1)
    nc = pl.num_programs(1)
    hist_row = SUBLANE - (CONV_W - 1)

    @pl.when(c == 0)
    def _():
        xp_scr[hist_row:SUBLANE, :] = hist_ref[0]
        for g in range(N_SSM_GROUPS):
            hg = h0_ref[0, g * HEADS_PER_GROUP:(g + 1) * HEADS_PER_GROUP]
            st_scr[g] = hg.reshape(D_GROUP, D_STATE).T

    xp_scr[SUBLANE:SUBLANE + q, 0:D_SSM] = xx_ref[...].astype(F32)
    xp_scr[SUBLANE:SUBLANE + q, D_SSM:D_SSM + D_BC] = xb_ref[...].astype(F32)
    xp_scr[SUBLANE:SUBLANE + q, D_SSM + D_BC:D_CONV] = xc_ref[...].astype(F32)
    acc = cb_ref[...] + xp_scr[hist_row:hist_row + q, :] * cw_ref[0:1, :]
    for j in range(1, CONV_W):
        acc = acc + xp_scr[hist_row + j:hist_row + j + q, :] * cw_ref[j:j + 1, :]
    xbc = _silu(acc)
    xs = xbc[:, 0:D_SSM]
    xs_bf = xs.astype(BF16)
    bm = xbc[:, D_SSM:D_SSM + D_BC].astype(BF16)
    cm = xbc[:, D_SSM + D_BC:D_CONV].astype(BF16)

    @pl.when(c == nc - 1)
    def _():
        nconv_ref[0] = xp_scr[q + hist_row:q + SUBLANE, :]

    xp_scr[0:SUBLANE, :] = xp_scr[q:q + SUBLANE, :]

    dtv = jax.nn.softplus(dt_ref[...] + dtb_ref[...])
    a = -jnp.exp(alog_ref[...])
    da = dtv * a
    ri = lax.broadcasted_iota(jnp.int32, (q, q), 0)
    ci = lax.broadcasted_iota(jnp.int32, (q, q), 1)
    causal = ri >= ci
    tril = jnp.where(causal, 1.0, 0.0).astype(BF16)
    triu = jnp.where(ri <= ci, 1.0, 0.0).astype(BF16)
    eye = jnp.where(ri == ci, 1.0, 0.0).astype(BF16)
    da3 = jnp.concatenate(_split_bf16(da, 3), axis=1)
    dt3 = jnp.concatenate(_split_bf16(dtv, 3), axis=1)
    cum3 = jnp.dot(tril, da3, preferred_element_type=F32)
    cum = cum3[:, 2 * LANE:3 * LANE] + cum3[:, LANE:2 * LANE] + cum3[:, 0:LANE]
    tn_dims = (((0,), (0,)), ((), ()))
    cum_t = _sum_rows(lax.dot_general(da3, triu, tn_dims, preferred_element_type=F32), 3, LANE)
    dt_t = _sum_rows(lax.dot_general(dt3, eye, tn_dims, preferred_element_type=F32), 3, LANE)

    ecum = jnp.exp(cum)
    dd = dtv * jnp.exp(cum[q - 1:q, :] - cum)
    ex_in = jnp.concatenate(_split_bf16(ecum, 2) + _split_bf16(dd, 2), axis=0)
    ex = jnp.dot(ex_in, e_ref[...], preferred_element_type=F32)
    ecum_e = ex[q:2 * q] + ex[0:q]
    dd_e = ex[3 * q:4 * q] + ex[2 * q:3 * q]
    xdd = (xs * dd_e).astype(BF16)

    lane = lax.broadcasted_iota(jnp.int32, (q, LANE), 1)
    lo_half = lane < SSM_HEAD_DIM
    nt_dims = (((1,), (1,)), ((), ()))
    for g in range(N_SSM_GROUPS):
        bg = bm[:, g * D_STATE:(g + 1) * D_STATE]
        cg = cm[:, g * D_STATE:(g + 1) * D_STATE]
        gs = slice(g * D_GROUP, (g + 1) * D_GROUP)
        cbg = lax.dot_general(cg, bg, nt_dims, preferred_element_type=F32)
        st = st_scr[g]
        y_off = jnp.dot(cg, st.astype(BF16), preferred_element_type=F32) * ecum_e[:, gs]
        st_scr[g] = st * ecum_e[q - 1:q, gs] + lax.dot_general(
            bg, xdd[:, gs], tn_dims, preferred_element_type=F32)
        for t in range(HEADS_PER_GROUP // 2):
            ws = []
            for h in (g * HEADS_PER_GROUP + 2 * t, g * HEADS_PER_GROUP + 2 * t + 1):
                seg = cum[:, h:h + 1] - cum_t[h:h + 1, :]
                lmat = jnp.exp(jnp.where(causal, seg, -jnp.inf))
                ws.append((cbg * lmat * dt_t[h:h + 1, :]).astype(BF16))
            cs = slice(g * D_GROUP + t * LANE, g * D_GROUP + (t + 1) * LANE)
            xpair = xs_bf[:, cs]
            zero = jnp.zeros_like(xpair)
            y_diag = (jnp.dot(ws[0], jnp.where(lo_half, xpair, zero), preferred_element_type=F32)
                      + jnp.dot(ws[1], jnp.where(lo_half, zero, xpair), preferred_element_type=F32))
            y_scr[:, cs] = (y_diag + y_off[:, t * LANE:(t + 1) * LANE]) + xs[:, cs] * dskip_ref[:, cs]

    for g in range(N_SSM_GROUPS):
        gs = slice(g * D_GROUP, (g + 1) * D_GROUP)
        yg = y_scr[:, gs] * _silu(z_ref[:, gs].astype(F32))
        y_ref[:, gs] = (yg * _rms_scale(yg, NORM_EPS) * nw_ref[:, gs]).astype(BF16)

    @pl.when(c == nc - 1)
    def _():
        for g in range(N_SSM_GROUPS):
            hnew_ref[0, g * HEADS_PER_GROUP:(g + 1) * HEADS_PER_GROUP] = (
                st_scr[g].T.reshape(HEADS_PER_GROUP, SSM_HEAD_DIM, D_STATE))


def _ssd(main, dt, conv_hist, h0, b_off, lp, bsz, length, q):
    nc = length // q
    m = bsz * length
    row = lambda b, c: b * nc + c
    return pl.pallas_call(
        functools.partial(_ssd_kernel, q=q),
        grid=(bsz, nc),
        in_specs=[
            pl.BlockSpec((q, D_SSM), lambda b, c: (row(b, c), COL_Z // D_SSM)),
            pl.BlockSpec((q, D_SSM), lambda b, c: (row(b, c), COL_X // D_SSM)),
            pl.BlockSpec((q, D_BC), lambda b, c: (row(b, c), COL_B // D_BC)),
            pl.BlockSpec((q, D_BC), lambda b, c: (row(b, c), COL_C // D_BC)),
            pl.BlockSpec((q, LANE), lambda b, c: (row(b, c), 0)),
            pl.BlockSpec((1, CONV_W - 1, D_CONV), lambda b, c: (b_off + b, 0, 0)),
            pl.BlockSpec((1, N_SSM_HEADS, SSM_HEAD_DIM, D_STATE), lambda b, c: (b_off + b, 0, 0, 0)),
            pl.BlockSpec((CONV_W, D_CONV), lambda b, c: (0, 0)),
            pl.BlockSpec((1, D_CONV), lambda b, c: (0, 0)),
            pl.BlockSpec((1, LANE), lambda b, c: (0, 0)),
            pl.BlockSpec((1, LANE), lambda b, c: (0, 0)),
            pl.BlockSpec((1, D_SSM), lambda b, c: (0, 0)),
            pl.BlockSpec((1, D_SSM), lambda b, c: (0, 0)),
            pl.BlockSpec((LANE, D_SSM), lambda b, c: (0, 0)),
        ],
        out_specs=[
            pl.BlockSpec((q, D_SSM), lambda b, c: (row(b, c), 0)),
            pl.BlockSpec((1, CONV_W - 1, D_CONV), lambda b, c: (b, 0, 0)),
            pl.BlockSpec((1, N_SSM_HEADS, SSM_HEAD_DIM, D_STATE), lambda b, c: (b, 0, 0, 0)),
        ],
        out_shape=[
            jax.ShapeDtypeStruct((m, D_MODEL), BF16),
            jax.ShapeDtypeStruct((bsz, CONV_W - 1, D_CONV), F32),
            jax.ShapeDtypeStruct((bsz, N_SSM_HEADS, SSM_HEAD_DIM, D_STATE), F32),
        ],
        scratch_shapes=[
            pltpu.VMEM((q + SUBLANE, D_CONV), F32),
            pltpu.VMEM((N_SSM_GROUPS, D_STATE, D_GROUP), F32),
            pltpu.VMEM((q, D_SSM), F32),
        ],
        compiler_params=_params(2),
        name="ssd",
    )(main, main, main, main, dt, conv_hist, h0,
      lp["conv_w"], lp["conv_b"], lp["dt_bias"], lp["a_log"], lp["d_skip_e"], lp["ssm_norm_w"],
      lp["expand"])


def _lambda(lq1, lk1, lq2, lk2, lambda_init):
    return (jnp.exp(jnp.sum(lq1[...] * lk1[...], axis=-1, keepdims=True))
            - jnp.exp(jnp.sum(lq2[...] * lk2[...], axis=-1, keepdims=True)) + lambda_init)


def _diff_finish(o1, o2, lam, sw_ref, g, lambda_init):
    o = o1 - lam * o2
    on = (o * _rms_scale(o, SUBLN_EPS) * sw_ref[...]) * (1.0 - lambda_init)
    return (on * _silu(g.astype(F32))).astype(BF16)


ATTN_ROWS = 32
ATTN_BAND_ROWS = 1024


def _attn_update(q_ref, k_ref, v_ref, m_scr, l_scr, acc_scr, mp, q0, n, kw, diagonal):
    c = (DIFF_HEAD_DIM ** -0.5) * math.log2(math.e)
    cols = slice(mp * DIFF_HEAD_DIM, (mp + 1) * DIFF_HEAD_DIM)
    rows = slice(q0, q0 + n)
    s = lax.dot_general(q_ref[rows, cols], k_ref[0:kw, cols], (((1,), (1,)), ((), ())),
                        preferred_element_type=F32)
    m_prev = m_scr[mp, rows]
    step = min(ATTN_ROWS, n)
    assert CHUNK % step == 0
    ps, m_news, p_sums = [], [], []
    for r in range(0, n, step):
        if diagonal:
            lim = ((q0 + r) // CHUNK + 1) * CHUNK
            w = min(kw, -(-lim // LANE) * LANE)
            sr = s[r:r + step, 0:w]
            if lim < w:
                sr = jnp.where(lax.broadcasted_iota(jnp.int32, (1, w), 1) < lim, sr, -jnp.inf)
        else:
            w = kw
            sr = s[r:r + step]
        mn = jnp.maximum(m_prev[r:r + step], jnp.max(sr, axis=-1, keepdims=True))
        p = jnp.exp2((sr - mn) * c)
        pb = p.astype(BF16)
        ps.append(pb if w == kw else jnp.concatenate([pb, jnp.zeros((step, kw - w), BF16)], axis=1))
        m_news.append(mn)
        p_sums.append(jnp.sum(p, axis=-1, keepdims=True))
    m_new = jnp.concatenate(m_news, axis=0)
    alpha = jnp.exp2((m_prev - m_new) * c)
    l_scr[mp, rows] = alpha * l_scr[mp, rows] + jnp.concatenate(p_sums, axis=0)
    acc_scr[mp, rows] = alpha * acc_scr[mp, rows] + jnp.dot(
        jnp.concatenate(ps, axis=0), v_ref[0:kw, :], preferred_element_type=F32)
    m_scr[mp, rows] = m_new


def _diff_attn_kernel(q_ref, k_ref, v_ref, g_ref, lq1, lk1, lq2, lk2, sw_ref, y_hbm, o_ref,
                      m_scr, l_scr, acc_scr, *, tq, tk, lambda_init):
    del y_hbm
    qi = pl.program_id(2)
    ki = pl.program_id(3)
    update = functools.partial(_attn_update, q_ref, k_ref, v_ref, m_scr, l_scr, acc_scr)
    assert tq == tk and tq % CHUNK == 0

    @pl.when(ki == 0)
    def _():
        m_scr[...] = jnp.full(m_scr.shape, -jnp.inf, F32)
        l_scr[...] = jnp.zeros(l_scr.shape, F32)
        acc_scr[...] = jnp.zeros(acc_scr.shape, F32)

    band = min(ATTN_BAND_ROWS, tq)

    @pl.when(ki < qi)
    def _():
        for q0 in range(0, tq, band):
            for mp in range(2):
                update(mp=mp, q0=q0, n=band, kw=tk, diagonal=False)

    @pl.when(ki == qi)
    def _():
        for q0 in range(0, tq, band):
            for mp in range(2):
                update(mp=mp, q0=q0, n=band, kw=q0 + band, diagonal=True)
        lam = _lambda(lq1, lk1, lq2, lk2, lambda_init)
        o_ref[...] = _diff_finish(acc_scr[0] / l_scr[0], acc_scr[1] / l_scr[1], lam, sw_ref, g_ref[...],
                                  lambda_init)


def _diff_attn_prompt(main, y, lp, bsz, length, lambda_init, tq):
    tk = tq
    nq = length // tq
    m = bsz * length
    vec = pl.BlockSpec((1, DIFF_HEAD_DIM), lambda b, h, qi, ki: (0, 0))
    return pl.pallas_call(
        functools.partial(_diff_attn_kernel, tq=tq, tk=tk, lambda_init=lambda_init),
        grid=(bsz, N_DIFF_HEADS, nq, nq),
        in_specs=[
            pl.BlockSpec((tq, HEAD_W), lambda b, h, qi, ki: (b * nq + qi, COL_QD // HEAD_W + h)),
            pl.BlockSpec((tk, HEAD_W), lambda b, h, qi, ki: (b * nq + jnp.minimum(ki, qi), COL_KD // HEAD_W + h)),
            pl.BlockSpec((tk, HEAD_W), lambda b, h, qi, ki: (b * nq + jnp.minimum(ki, qi), COL_VD // HEAD_W + h)),
            pl.BlockSpec((tq, HEAD_W), lambda b, h, qi, ki: (b * nq + qi, COL_GD // HEAD_W + h)),
            vec, vec, vec, vec,
            pl.BlockSpec((1, HEAD_W), lambda b, h, qi, ki: (0, 0)),
            pl.BlockSpec(memory_space=pl.ANY),
        ],
        out_specs=pl.BlockSpec((tq, HEAD_W), lambda b, h, qi, ki: (b * nq + qi, D_SSM // HEAD_W + h)),
        out_shape=jax.ShapeDtypeStruct((m, D_MODEL), BF16),
        input_output_aliases={9: 0},
        scratch_shapes=[
            pltpu.VMEM((2, tq, 1), F32),
            pltpu.VMEM((2, tq, 1), F32),
            pltpu.VMEM((2, tq, HEAD_W), F32),
        ],
        compiler_params=_params(4),
        name="diff_attn_prompt",
    )(main, main, main, main, lp["lambda_q1"], lp["lambda_k1"], lp["lambda_q2"], lp["lambda_k2"],
      lp["subln_w"], y)


def _diff_attn_decode_kernel(q_ref, kn_ref, vn_ref, g_ref, kp_ref, vp_ref, lq1, lk1, lq2, lk2, sw_ref,
                             y_hbm, o_ref, *, past, lambda_init):
    del y_hbm
    scale = DIFF_HEAD_DIM ** -0.5
    lq = q_ref.shape[0]
    nt_dims = (((1,), (1,)), ((), ()))
    mask_p = (_chunk_of(lax.broadcasted_iota(jnp.int32, (lq, past), 1))
              <= _chunk_of(past + lax.broadcasted_iota(jnp.int32, (lq, past), 0)))
    mask_n = (_chunk_of(past + lax.broadcasted_iota(jnp.int32, (lq, lq), 1))
              <= _chunk_of(past + lax.broadcasted_iota(jnp.int32, (lq, lq), 0)))
    lam = _lambda(lq1, lk1, lq2, lk2, lambda_init)
    for h in range(N_DIFF_HEADS):
        hs = slice(h * HEAD_W, (h + 1) * HEAD_W)
        vp = _load_flat(vp_ref, past, h * HEAD_W, HEAD_W, halves_split=True).astype(BF16)
        vn = vn_ref[:, hs]
        outs = []
        for mp in range(2):
            c0 = h * HEAD_W + mp * DIFF_HEAD_DIM
            qm = q_ref[:, c0:c0 + DIFF_HEAD_DIM]
            kp = _load_flat(kp_ref, past, c0, DIFF_HEAD_DIM, halves_split=False).astype(BF16)
            sp = lax.dot_general(qm, kp, nt_dims, preferred_element_type=F32) * scale
            sn = lax.dot_general(qm, kn_ref[:, c0:c0 + DIFF_HEAD_DIM], nt_dims,
                                 preferred_element_type=F32) * scale
            sp = jnp.where(mask_p, sp, -jnp.inf)
            sn = jnp.where(mask_n, sn, -jnp.inf)
            mx = jnp.maximum(jnp.max(sp, axis=-1, keepdims=True), jnp.max(sn, axis=-1, keepdims=True))
            pp = jnp.exp(sp - mx)
            pn = jnp.exp(sn - mx)
            denom = jnp.sum(pp, axis=-1, keepdims=True) + jnp.sum(pn, axis=-1, keepdims=True)
            o = (jnp.dot(pp.astype(BF16), vp, preferred_element_type=F32)
                 + jnp.dot(pn.astype(BF16), vn, preferred_element_type=F32))
            outs.append(o / denom)
        o_ref[:, hs] = _diff_finish(outs[0], outs[1], lam, sw_ref, g_ref[:, hs], lambda_init)


def _diff_attn_decode(main, y, k_past, v_past, b_off, lp, bsz, length, past, lambda_init):
    m = bsz * length
    vec = pl.BlockSpec((1, DIFF_HEAD_DIM), lambda b: (0, 0))
    return pl.pallas_call(
        functools.partial(_diff_attn_decode_kernel, past=past, lambda_init=lambda_init),
        grid=(bsz,),
        in_specs=[
            pl.BlockSpec((length, D_DIFF), lambda b: (b, COL_QD // D_DIFF)),
            pl.BlockSpec((length, D_DIFF), lambda b: (b, COL_KD // D_DIFF)),
            pl.BlockSpec((length, D_DIFF), lambda b: (b, COL_VD // D_DIFF)),
            pl.BlockSpec((length, D_DIFF), lambda b: (b, COL_GD // D_DIFF)),
            pl.BlockSpec((past * ROW_TILES, LANE), lambda b: (b_off + b, 0)),
            pl.BlockSpec((past * ROW_TILES, LANE), lambda b: (b_off + b, 0)),
            vec, vec, vec, vec,
            pl.BlockSpec((1, HEAD_W), lambda b: (0, 0)),
            pl.BlockSpec(memory_space=pl.ANY),
        ],
        out_specs=pl.BlockSpec((length, D_DIFF), lambda b: (b, D_SSM // D_DIFF)),
        out_shape=jax.ShapeDtypeStruct((m, D_MODEL), BF16),
        input_output_aliases={11: 0},
        compiler_params=_params(1),
        name="diff_attn_decode",
    )(main, main, main, main, k_past, v_past,
      lp["lambda_q1"], lp["lambda_k1"], lp["lambda_q2"], lp["lambda_k2"], lp["subln_w"], y)


def _mem_attn_kernel(q_ref, g_ref, mk_ref, mv_ref, y_hbm, o_ref, *, n_mem):
    del y_hbm
    scale = MEM_HEAD_DIM ** -0.5
    for h in range(N_MEM_HEADS):
        hs = slice(h * MEM_HEAD_DIM, (h + 1) * MEM_HEAD_DIM)
        mk = _load_flat(mk_ref, n_mem, h * MEM_HEAD_DIM, MEM_HEAD_DIM, halves_split=True).astype(BF16)
        mv = _load_flat(mv_ref, n_mem, h * MEM_HEAD_DIM, MEM_HEAD_DIM, halves_split=True).astype(BF16)
        s = lax.dot_general(q_ref[:, hs], mk, (((1,), (1,)), ((), ())), preferred_element_type=F32) * scale
        p = jnp.exp(s - jnp.max(s, axis=-1, keepdims=True))
        pr = p / jnp.sum(p, axis=-1, keepdims=True)
        o = jnp.dot(pr.astype(BF16), mv, preferred_element_type=F32)
        o_ref[:, hs] = (o * _silu(g_ref[:, hs].astype(F32))).astype(BF16)


def _mem_attn(main, y, mk, mv, b_off, bsz, length, n_mem, tq):
    nq = length // tq
    m = bsz * length
    return pl.pallas_call(
        functools.partial(_mem_attn_kernel, n_mem=n_mem),
        grid=(bsz, nq),
        in_specs=[
            pl.BlockSpec((tq, D_MEM), lambda b, qi: (b * nq + qi, COL_QM // D_MEM)),
            pl.BlockSpec((tq, D_MEM), lambda b, qi: (b * nq + qi, COL_GM // D_MEM)),
            pl.BlockSpec((n_mem * ROW_TILES, LANE), lambda b, qi: (b_off + b, 0)),
            pl.BlockSpec((n_mem * ROW_TILES, LANE), lambda b, qi: (b_off + b, 0)),
            pl.BlockSpec(memory_space=pl.ANY),
        ],
        out_specs=pl.BlockSpec((tq, D_MEM), lambda b, qi: (b * nq + qi, (D_SSM + D_DIFF) // D_MEM)),
        out_shape=jax.ShapeDtypeStruct((m, D_MODEL), BF16),
        input_output_aliases={4: 0},
        compiler_params=_params(2),
        name="mem_attn",
    )(main, main, mk, mv, y)


def _out_proj_kernel(y_ref, w_ref, x_ref, nw_ref, o_ref, *, nj, tn):
    j = pl.program_id(1)
    acc = jnp.dot(y_ref[...], w_ref[...], preferred_element_type=F32)
    o_ref[:, pl.ds(pl.multiple_of(j * tn, tn), tn)] = acc

    @pl.when(j == nj - 1)
    def _():
        tm = o_ref.shape[0]
        step = min(NORM_ROWS, tm)
        for r in range(0, tm, step):
            o = o_ref[r:r + step, :]
            o_ref[r:r + step, :] = x_ref[r:r + step, :] + o * _rms_scale(o, NORM_EPS) * nw_ref[...]


def _out_proj(y, w_out, x2d, norm_w, tm, tn):
    m = x2d.shape[0]
    nj = D_MODEL // tn
    return pl.pallas_call(
        functools.partial(_out_proj_kernel, nj=nj, tn=tn),
        grid=(m // tm, nj),
        in_specs=[
            pl.BlockSpec((tm, D_MODEL), lambda i, j: (i, 0)),
            pl.BlockSpec((D_MODEL, tn), lambda i, j: (0, j)),
            pl.BlockSpec((tm, D_MODEL), lambda i, j: (i, 0)),
            pl.BlockSpec((1, D_MODEL), lambda i, j: (0, 0)),
        ],
        out_specs=pl.BlockSpec((tm, D_MODEL), lambda i, j: (i, 0)),
        out_shape=jax.ShapeDtypeStruct((m, D_MODEL), F32),
        compiler_params=_params(2),
        name="out_proj",
    )(y, w_out, x2d, norm_w)


def _mixer_layer(x2d, bsz, length, layer, depth, kv_all, b_off, mem_off, conv_hist, h0, kv_past, past, mk,
                 mv, n_mem, lp, lambda_init, tiles):
    main, k32, v32, dt = _in_proj(x2d, lp["norm_pre_w"], lp["w_main"], lp["w_dt"], tiles["proj_tm"],
                                  layer, depth, kv_all)
    y, new_conv, h_new = _ssd(main, dt, conv_hist, h0, b_off, lp, bsz, length, tiles["ssd_q"])
    if kv_past is None:
        y = _diff_attn_prompt(main, y, lp, bsz, length, lambda_init, tiles["attn_tq"])
    else:
        y = _diff_attn_decode(main, y, kv_past[0], kv_past[1], b_off, lp, bsz, length, past, lambda_init)
    y = _mem_attn(main, y, mk, mv, mem_off, bsz, length, n_mem, tiles["mem_tq"])
    x_new = _out_proj(y, lp["w_out"], x2d, lp["norm_post_w"], tiles["out_tm"], tiles["out_tn"])
    return x_new, new_conv, h_new, (k32, v32)


def _layer_params(l, norm_pre_w, norm_post_w, w_in_t, conv_w, conv_b, dt_bias, a_log, d_skip,
                  ssm_norm_w, lambda_q1, lambda_k1, lambda_q2, lambda_k2, subln_w, mem_norm_w,
                  w_mem_kv, w_out):
    w_main, w_dt = _w_prep(w_in_t, l)
    pad_heads = lambda v: jnp.pad(v, (0, LANE - N_SSM_HEADS)).reshape(1, LANE)
    head_of_channel = jnp.arange(D_SSM) // SSM_HEAD_DIM
    expand = (jnp.arange(LANE)[:, None] == head_of_channel[None, :]).astype(BF16)
    return {
        "norm_pre_w": norm_pre_w[l].reshape(1, D_MODEL),
        "norm_post_w": norm_post_w[l].reshape(1, D_MODEL),
        "w_main": w_main,
        "w_dt": w_dt,
        "conv_w": conv_w[l],
        "conv_b": conv_b[l].reshape(1, D_CONV),
        "dt_bias": pad_heads(dt_bias[l]),
        "a_log": pad_heads(a_log[l]),
        "d_skip_e": jnp.repeat(d_skip[l], SSM_HEAD_DIM).reshape(1, D_SSM),
        "ssm_norm_w": ssm_norm_w[l].reshape(1, D_SSM),
        "expand": expand,
        "lambda_q1": lambda_q1[l].reshape(1, DIFF_HEAD_DIM),
        "lambda_k1": lambda_k1[l].reshape(1, DIFF_HEAD_DIM),
        "lambda_q2": lambda_q2[l].reshape(1, DIFF_HEAD_DIM),
        "lambda_k2": lambda_k2[l].reshape(1, DIFF_HEAD_DIM),
        "subln_w": subln_w[l].reshape(1, HEAD_W),
        "mem_norm_w": mem_norm_w[l].reshape(1, D_MODEL),
        "w_mem_kv": _layer_bf16(w_mem_kv, l),
        "w_out": _layer_bf16(w_out, l),
    }


def _tiles(bsz, length):
    m = bsz * length
    return {
        "proj_tm": min(512, m),
        "ssd_q": min(128, length),
        "attn_tq": min(1024, length),
        "mem_tq": min(1024, length),
        "out_tm": min(512, m),
        "out_tn": 512,
    }


def kernel(x_prompt, x_sample, mem_prompt, cache_conv, state_ssm, cache_k, cache_v, cache_mem_k,
           cache_mem_v, norm_pre_w, norm_post_w, w_in, conv_w, conv_b, dt_bias, a_log, d_skip,
           ssm_norm_w, lambda_q1, lambda_k1, lambda_q2, lambda_k2, subln_w, mem_norm_w, w_mem_kv,
           w_out):
    depth = w_in.shape[0]
    bp, lp_, _ = x_prompt.shape
    bs, ls, _ = x_sample.shape
    n_mem = mem_prompt.shape[1]
    past = cache_k.shape[2]
    xp = x_prompt.reshape(bp * lp_, D_MODEL)
    xs = x_sample.reshape(bs * ls, D_MODEL)
    mem2d = mem_prompt.reshape(bp * n_mem, D_MODEL)
    tiles_p = _tiles(bp, lp_)
    tiles_s = _tiles(bs, ls)
    conv0 = jnp.zeros((bp, CONV_W - 1, D_CONV), F32)
    h_zero = jnp.zeros((bp, N_SSM_HEADS, SSM_HEAD_DIM, D_STATE), F32)
    conv_all = cache_conv.reshape(depth * bs, CONV_W - 1, D_CONV)
    ssm_all = state_ssm.reshape(depth * bs, N_SSM_HEADS, SSM_HEAD_DIM, D_STATE)
    kv_all = (cache_k.reshape(-1, LANE), _to_flat_halves(cache_v))
    mem_k_all = _to_flat_halves(cache_mem_k)
    mem_v_all = _to_flat_halves(cache_mem_v)
    outs = [[] for _ in range(6)]
    kv_p = kv_s = None
    w_in_t = jnp.swapaxes(w_in, 1, 2)
    for l in range(depth):
        lp = _layer_params(l, norm_pre_w, norm_post_w, w_in_t, conv_w, conv_b, dt_bias, a_log, d_skip,
                           ssm_norm_w, lambda_q1, lambda_k1, lambda_q2, lambda_k2, subln_w,
                           mem_norm_w, w_mem_kv, w_out)
        lambda_init = 0.8 - 0.6 * math.exp(-0.3 * l)
        mk_p, mv_p = _mem_kv(mem2d, lp["mem_norm_w"], lp["w_mem_kv"], min(512, bp * n_mem))
        xp, c_p, h_p, kv_p = _mixer_layer(xp, bp, lp_, l, depth, kv_p, 0, 0, conv0, h_zero, None, 0, mk_p,
                                          mv_p, n_mem, lp, lambda_init, tiles_p)
        xs, c_s, h_s, kv_s = _mixer_layer(xs, bs, ls, l, depth, kv_s, l * bs, l * bs, conv_all, ssm_all,
                                          kv_all, past, mem_k_all, mem_v_all, n_mem, lp, lambda_init,
                                          tiles_s)
        for lst, val in zip(outs, (c_p, h_p, mk_p, mv_p, c_s, h_s)):
            lst.append(val)
    st = [jnp.stack(o) for o in outs]
    return (
        xp.reshape(bp, lp_, D_MODEL),
        xs.reshape(bs, ls, D_MODEL),
        st[0],
        st[1],
        kv_p[0].reshape(depth, bp, lp_, N_DIFF_HEADS, 2, DIFF_HEAD_DIM),
        _from_flat_halves(kv_p[1], (depth, bp, lp_, N_DIFF_HEADS, 2 * DIFF_HEAD_DIM)),
        _from_flat_halves(st[2], (depth, bp, n_mem, N_MEM_HEADS, MEM_HEAD_DIM)),
        _from_flat_halves(st[3], (depth, bp, n_mem, N_MEM_HEADS, MEM_HEAD_DIM)),
        st[4],
        st[5],
        kv_s[0].reshape(depth, bs, ls, N_DIFF_HEADS, 2, DIFF_HEAD_DIM),
        _from_flat_halves(kv_s[1], (depth, bs, ls, N_DIFF_HEADS, 2 * DIFF_HEAD_DIM)),
    )
```

```python
import functools
import math

import jax
import jax.numpy as jnp
from jax import lax
from jax.experimental import pallas as pl
from jax.experimental.pallas import tpu as pltpu

F32 = jnp.float32
BF16 = jnp.bfloat16

D_MODEL = 4096
D_SSM = 2048
SSM_HEAD_DIM = 64
N_SSM_HEADS = 32
N_SSM_GROUPS = 4
HEADS_PER_GROUP = N_SSM_HEADS // N_SSM_GROUPS
D_GROUP = D_SSM // N_SSM_GROUPS
D_STATE = 128
CONV_W = 4
D_BC = N_SSM_GROUPS * D_STATE
D_CONV = D_SSM + 2 * D_BC
D_DIFF = 1024
N_DIFF_HEADS = 4
DIFF_HEAD_DIM = 128
D_MEM = 1024
N_MEM_HEADS = 4
MEM_HEAD_DIM = 256
CHUNK = 64
NORM_EPS = 1e-6
SUBLN_EPS = 1e-5

LANE = 128
SUBLANE = 8
VMEM_LIMIT = 56 * 1024 * 1024

HEAD_W = 2 * DIFF_HEAD_DIM
COL_Z = 0
COL_X = D_SSM
COL_B = COL_X + D_SSM
COL_C = COL_B + D_BC
COL_QD = COL_C + D_BC
COL_GD = COL_QD + D_DIFF
COL_QM = COL_GD + D_DIFF
COL_GM = COL_QM + D_MEM
COL_KD = COL_GM + D_MEM
COL_VD = COL_KD + D_DIFF
D_MAIN = COL_VD + D_DIFF
PROJ_TN = 1024
K_TILE = COL_KD // PROJ_TN
V_TILE = COL_VD // PROJ_TN


def _params(n_axes):
    return pltpu.CompilerParams(dimension_semantics=("arbitrary",) * n_axes,
                                vmem_limit_bytes=VMEM_LIMIT)


def _silu(x):
    return x * jax.nn.sigmoid(x)


def _chunk_of(pos):
    assert CHUNK & (CHUNK - 1) == 0
    return lax.shift_right_logical(pos, jnp.int32(CHUNK.bit_length() - 1))


def _rms_scale(x, eps):
    return lax.rsqrt(jnp.mean(x * x, axis=-1, keepdims=True) + eps)


ROW_TILES = 1024 // LANE
N_HEADS_1024 = 1024 // HEAD_W


def _flat_row(c, halves_split):
    head, half = divmod(c, 2)
    return half * N_HEADS_1024 + head if halves_split else c


def _store_flat(ref, val, halves_split):
    rows = val.shape[0]
    for c in range(ROW_TILES):
        ref[pl.ds(_flat_row(c, halves_split), rows, stride=ROW_TILES), :] = val[:, c * LANE:(c + 1) * LANE]


def _load_flat(ref, rows, col0, width, halves_split):
    parts = [ref[pl.ds(_flat_row(col0 // LANE + c, halves_split), rows, stride=ROW_TILES), :]
             for c in range(width // LANE)]
    return parts[0] if len(parts) == 1 else jnp.concatenate(parts, axis=1)


def _to_flat_halves(a):
    return a.reshape(-1, N_HEADS_1024, 2, LANE).transpose(0, 2, 1, 3).reshape(-1, LANE)


def _from_flat_halves(flat, shape):
    return flat.reshape(-1, 2, N_HEADS_1024, LANE).transpose(0, 2, 1, 3).reshape(shape)


NORM_ROWS = 128


def _norm_rows(x_ref, nw_ref, h_scr):
    tm = x_ref.shape[0]
    step = min(NORM_ROWS, tm)
    for r in range(0, tm, step):
        x = x_ref[r:r + step, :]
        h_scr[r:r + step, :] = (x * _rms_scale(x, NORM_EPS) * nw_ref[...]).astype(BF16)


O_DT = D_SSM + D_CONV
O_Q = O_DT + N_SSM_HEADS
SRC_SEGMENT_OF = ((COL_QD, 0), (COL_KD, 1), (COL_VD, 2), (COL_GD, 3), (COL_QM, 4), (COL_GM, 5))
PREP_COLS = 256


def _prep_source_columns():
    src = list(range(0, O_DT, PREP_COLS))
    for dst, seg in sorted(SRC_SEGMENT_OF):
        assert dst == len(src) * PREP_COLS
        src += list(range(O_Q + seg * D_DIFF, O_Q + (seg + 1) * D_DIFF, PREP_COLS))
    return jnp.asarray(src, jnp.int32)


def _w_prep_kernel(src_ref, w_ref, wdt_ref, main_ref, dt_ref):
    del src_ref
    main_ref[...] = w_ref[0].T.astype(BF16)

    @pl.when(pl.program_id(0) == 0)
    def _():
        lane = lax.broadcasted_iota(jnp.int32, (1, LANE), 1)
        dt_ref[...] = jnp.where(lane < N_SSM_HEADS, wdt_ref[...].T, 0.0).astype(BF16)


def _w_prep(w_in_t, layer):
    return pl.pallas_call(
        _w_prep_kernel,
        grid_spec=pltpu.PrefetchScalarGridSpec(
            num_scalar_prefetch=1,
            grid=(D_MAIN // PREP_COLS,),
            in_specs=[
                pl.BlockSpec((pl.Element(1), pl.Element(PREP_COLS), pl.Element(D_MODEL)),
                             lambda j, src: (layer, pl.multiple_of(src[j], N_SSM_HEADS), 0)),
                pl.BlockSpec((None, LANE, D_MODEL), lambda j, src: (layer, O_DT // LANE, 0)),
            ],
            out_specs=[
                pl.BlockSpec((D_MODEL, PREP_COLS), lambda j, src: (0, j)),
                pl.BlockSpec((D_MODEL, LANE), lambda j, src: (0, 0)),
            ],
        ),
        out_shape=[
            jax.ShapeDtypeStruct((D_MODEL, D_MAIN), BF16),
            jax.ShapeDtypeStruct((D_MODEL, LANE), BF16),
        ],
        compiler_params=_params(1),
        name="w_prep",
    )(_prep_source_columns(), w_in_t, w_in_t)


CAST_ROWS = 512


def _cast_kernel(w_ref, o_ref):
    o_ref[...] = w_ref[...].astype(BF16)


def _layer_bf16(w, layer):
    _, rows, cols = w.shape
    return pl.pallas_call(
        _cast_kernel,
        grid=(rows // CAST_ROWS,),
        in_specs=[pl.BlockSpec((None, CAST_ROWS, cols), lambda i: (layer, i, 0))],
        out_specs=pl.BlockSpec((CAST_ROWS, cols), lambda i: (i, 0)),
        out_shape=jax.ShapeDtypeStruct((rows, cols), BF16),
        compiler_params=_params(1),
        name="cast_bf16",
    )(w)


def _in_proj_kernel(x_ref, nw_ref, w_ref, wdt_ref, *refs):
    main_ref, k_ref, v_ref, dt_ref, h_scr = refs[-5:]
    j = pl.program_id(1)

    @pl.when(j == 0)
    def _():
        _norm_rows(x_ref, nw_ref, h_scr)
        dt_ref[...] = jnp.dot(h_scr[...], wdt_ref[...], preferred_element_type=F32)

    acc = jnp.dot(h_scr[...], w_ref[...], preferred_element_type=F32)
    main_ref[...] = acc.astype(BF16)

    @pl.when(j == K_TILE)
    def _():
        _store_flat(k_ref, acc, halves_split=False)

    @pl.when(j == V_TILE)
    def _():
        _store_flat(v_ref, acc, halves_split=True)


def _in_proj(x2d, norm_w, w_main, w_dt, tm, layer, depth, kv_all):
    m = x2d.shape[0]
    ni = m // tm
    grid = (ni, D_MAIN // PROJ_TN)
    kv_spec = pl.BlockSpec((tm * ROW_TILES, LANE), lambda i, j: (layer * ni + i, 0))
    kv_shape = jax.ShapeDtypeStruct((depth * m * ROW_TILES, LANE), F32)
    in_specs = [
        pl.BlockSpec((tm, D_MODEL), lambda i, j: (i, 0)),
        pl.BlockSpec((1, D_MODEL), lambda i, j: (0, 0)),
        pl.BlockSpec((D_MODEL, PROJ_TN), lambda i, j: (0, j)),
        pl.BlockSpec((D_MODEL, LANE), lambda i, j: (0, 0)),
    ]
    operands = [x2d, norm_w, w_main, w_dt]
    aliases = {}
    if kv_all is not None:
        in_specs += [pl.BlockSpec(memory_space=pl.ANY)] * 2
        aliases = {len(operands): 1, len(operands) + 1: 2}
        operands += list(kv_all)
    return pl.pallas_call(
        _in_proj_kernel,
        grid=grid,
        in_specs=in_specs,
        out_specs=[
            pl.BlockSpec((tm, PROJ_TN), lambda i, j: (i, j)),
            kv_spec,
            kv_spec,
            pl.BlockSpec((tm, LANE), lambda i, j: (i, 0)),
        ],
        out_shape=[
            jax.ShapeDtypeStruct((m, D_MAIN), BF16),
            kv_shape,
            kv_shape,
            jax.ShapeDtypeStruct((m, LANE), F32),
        ],
        scratch_shapes=[pltpu.VMEM((tm, D_MODEL), BF16)],
        input_output_aliases=aliases,
        compiler_params=_params(2),
        name="in_proj",
    )(*operands)


def _mem_kv_kernel(x_ref, nw_ref, w_ref, mk_ref, mv_ref, h_scr):
    j = pl.program_id(1)

    @pl.when(j == 0)
    def _():
        _norm_rows(x_ref, nw_ref, h_scr)

    acc = jnp.dot(h_scr[...], w_ref[...], preferred_element_type=F32)

    @pl.when(j == 0)
    def _():
        _store_flat(mk_ref, acc, halves_split=True)

    @pl.when(j == 1)
    def _():
        _store_flat(mv_ref, acc, halves_split=True)


def _mem_kv(mem2d, norm_w, w_kv, tm):
    m = mem2d.shape[0]
    return pl.pallas_call(
        _mem_kv_kernel,
        grid=(m // tm, 2),
        in_specs=[
            pl.BlockSpec((tm, D_MODEL), lambda i, j: (i, 0)),
            pl.BlockSpec((1, D_MODEL), lambda i, j: (0, 0)),
            pl.BlockSpec((D_MODEL, D_MEM), lambda i, j: (0, j)),
        ],
        out_specs=[
            pl.BlockSpec((tm * ROW_TILES, LANE), lambda i, j: (i, 0)),
            pl.BlockSpec((tm * ROW_TILES, LANE), lambda i, j: (i, 0)),
        ],
        out_shape=[jax.ShapeDtypeStruct((m * ROW_TILES, LANE), F32)] * 2,
        scratch_shapes=[pltpu.VMEM((tm, D_MODEL), BF16)],
        compiler_params=_params(2),
        name="mem_kv",
    )(mem2d, norm_w, w_kv)


def _split_bf16(x, n):
    parts = []
    r = x
    for _ in range(n - 1):
        p = r.astype(BF16)
        parts.append(p)
        r = r - p.astype(F32)
    parts.append(r.astype(BF16))
    return parts


def _sum_rows(a, n, rows):
    out = a[(n - 1) * rows:n * rows]
    for i in range(n - 2, -1, -1):
        out = out + a[i * rows:(i + 1) * rows]
    return out


def _ssd_kernel(z_ref, xx_ref, xb_ref, xc_ref, dt_ref, hist_ref, h0_ref,
                cw_ref, cb_ref, dtb_ref, alog_ref, dskip_ref, nw_ref, e_ref,
                y_ref, nconv_ref, hnew_ref, xp_scr, st_scr, y_scr, *, q):
    c = pl.program_id(1)
    nc = pl.num_programs(1)
    hist_row = SUBLANE - (CONV_W - 1)

    @pl.when(c == 0)
    def _():
        xp_scr[hist_row:SUBLANE, :] = hist_ref[0]
        for g in range(N_SSM_GROUPS):
            hg = h0_ref[0, g * HEADS_PER_GROUP:(g + 1) * HEADS_PER_GROUP]
            st_scr[g] = hg.reshape(D_GROUP, D_STATE).T

    xp_scr[SUBLANE:SUBLANE + q, 0:D_SSM] = xx_ref[...].astype(F32)
    xp_scr[SUBLANE:SUBLANE + q, D_SSM:D_SSM + D_BC] = xb_ref[...].astype(F32)
    xp_scr[SUBLANE:SUBLANE + q, D_SSM + D_BC:D_CONV] = xc_ref[...].astype(F32)
    acc = cb_ref[...] + xp_scr[hist_row:hist_row + q, :] * cw_ref[0:1, :]
    for j in range(1, CONV_W):
        acc = acc + xp_scr[hist_row + j:hist_row + j + q, :] * cw_ref[j:j + 1, :]
    xbc = _silu(acc)
    xs = xbc[:, 0:D_SSM]
    xs_bf = xs.astype(BF16)
    bm = xbc[:, D_SSM:D_SSM + D_BC].astype(BF16)
    cm = xbc[:, D_SSM + D_BC:D_CONV].astype(BF16)

    @pl.when(c == nc - 1)
    def _():
        nconv_ref[0] = xp_scr[q + hist_row:q + SUBLANE, :]

    xp_scr[0:SUBLANE, :] = xp_scr[q:q + SUBLANE, :]

    dtv = jax.nn.softplus(dt_ref[...] + dtb_ref[...])
    a = -jnp.exp(alog_ref[...])
    da = dtv * a
    ri = lax.broadcasted_iota(jnp.int32, (q, q), 0)
    ci = lax.broadcasted_iota(jnp.int32, (q, q), 1)
    causal = ri >= ci
    tril = jnp.where(causal, 1.0, 0.0).astype(BF16)
    triu = jnp.where(ri <= ci, 1.0, 0.0).astype(BF16)
    eye = jnp.where(ri == ci, 1.0, 0.0).astype(BF16)
    da3 = jnp.concatenate(_split_bf16(da, 3), axis=1)
    dt3 = jnp.concatenate(_split_bf16(dtv, 3), axis=1)
    cum3 = jnp.dot(tril, da3, preferred_element_type=F32)
    cum = cum3[:, 2 * LANE:3 * LANE] + cum3[:, LANE:2 * LANE] + cum3[:, 0:LANE]
    tn_dims = (((0,), (0,)), ((), ()))
    cum_t = _sum_rows(lax.dot_general(da3, triu, tn_dims, preferred_element_type=F32), 3, LANE)
    dt_t = _sum_rows(lax.dot_general(dt3, eye, tn_dims, preferred_element_type=F32), 3, LANE)

    ecum = jnp.exp(cum)
    dd = dtv * jnp.exp(cum[q - 1:q, :] - cum)
    ex_in = jnp.concatenate(_split_bf16(ecum, 2) + _split_bf16(dd, 2), axis=0)
    ex = jnp.dot(ex_in, e_ref[...], preferred_element_type=F32)
    ecum_e = ex[q:2 * q] + ex[0:q]
    dd_e = ex[3 * q:4 * q] + ex[2 * q:3 * q]
    xdd = (xs * dd_e).astype(BF16)

    lane = lax.broadcasted_iota(jnp.int32, (q, LANE), 1)
    lo_half = lane < SSM_HEAD_DIM
    nt_dims = (((1,), (1,)), ((), ()))
    for g in range(N_SSM_GROUPS):
        bg = bm[:, g * D_STATE:(g + 1) * D_STATE]
        cg = cm[:, g * D_STATE:(g + 1) * D_STATE]
        gs = slice(g * D_GROUP, (g + 1) * D_GROUP)
        cbg = lax.dot_general(cg, bg, nt_dims, preferred_element_type=F32)
        st = st_scr[g]
        y_off = jnp.dot(cg, st.astype(BF16), preferred_element_type=F32) * ecum_e[:, gs]
        st_scr[g] = st * ecum_e[q - 1:q, gs] + lax.dot_general(
            bg, xdd[:, gs], tn_dims, preferred_element_type=F32)
        for t in range(HEADS_PER_GROUP // 2):
            ws = []
            for h in (g * HEADS_PER_GROUP + 2 * t, g * HEADS_PER_GROUP + 2 * t + 1):
                seg = cum[:, h:h + 1] - cum_t[h:h + 1, :]
                lmat = jnp.exp(jnp.where(causal, seg, -jnp.inf))
                ws.append((cbg * lmat * dt_t[h:h + 1, :]).astype(BF16))
            cs = slice(g * D_GROUP + t * LANE, g * D_GROUP + (t + 1) * LANE)
            xpair = xs_bf[:, cs]
            zero = jnp.zeros_like(xpair)
            y_diag = (jnp.dot(ws[0], jnp.where(lo_half, xpair, zero), preferred_element_type=F32)
                      + jnp.dot(ws[1], jnp.where(lo_half, zero, xpair), preferred_element_type=F32))
            y_scr[:, cs] = (y_diag + y_off[:, t * LANE:(t + 1) * LANE]) + xs[:, cs] * dskip_ref[:, cs]

    for g in range(N_SSM_GROUPS):
        gs = slice(g * D_GROUP, (g + 1) * D_GROUP)
        yg = y_scr[:, gs] * _silu(z_ref[:, gs].astype(F32))
        y_ref[:, gs] = (yg * _rms_scale(yg, NORM_EPS) * nw_ref[:, gs]).astype(BF16)

    @pl.when(c == nc - 1)
    def _():
        for g in range(N_SSM_GROUPS):
            hnew_ref[0, g * HEADS_PER_GROUP:(g + 1) * HEADS_PER_GROUP] = (
                st_scr[g].T.reshape(HEADS_PER_GROUP, SSM_HEAD_DIM, D_STATE))


def _ssd(main, dt, conv_hist, h0, b_off, lp, bsz, length, q):
    nc = length // q
    m = bsz * length
    row = lambda b, c: b * nc + c
    return pl.pallas_call(
        functools.partial(_ssd_kernel, q=q),
        grid=(bsz, nc),
        in_specs=[
            pl.BlockSpec((q, D_SSM), lambda b, c: (row(b, c), COL_Z // D_SSM)),
            pl.BlockSpec((q, D_SSM), lambda b, c: (row(b, c), COL_X // D_SSM)),
            pl.BlockSpec((q, D_BC), lambda b, c: (row(b, c), COL_B // D_BC)),
            pl.BlockSpec((q, D_BC), lambda b, c: (row(b, c), COL_C // D_BC)),
            pl.BlockSpec((q, LANE), lambda b, c: (row(b, c), 0)),
            pl.BlockSpec((1, CONV_W - 1, D_CONV), lambda b, c: (b_off + b, 0, 0)),
            pl.BlockSpec((1, N_SSM_HEADS, SSM_HEAD_DIM, D_STATE), lambda b, c: (b_off + b, 0, 0, 0)),
            pl.BlockSpec((CONV_W, D_CONV), lambda b, c: (0, 0)),
            pl.BlockSpec((1, D_CONV), lambda b, c: (0, 0)),
            pl.BlockSpec((1, LANE), lambda b, c: (0, 0)),
            pl.BlockSpec((1, LANE), lambda b, c: (0, 0)),
            pl.BlockSpec((1, D_SSM), lambda b, c: (0, 0)),
            pl.BlockSpec((1, D_SSM), lambda b, c: (0, 0)),
            pl.BlockSpec((LANE, D_SSM), lambda b, c: (0, 0)),
        ],
        out_specs=[
            pl.BlockSpec((q, D_SSM), lambda b, c: (row(b, c), 0)),
            pl.BlockSpec((1, CONV_W - 1, D_CONV), lambda b, c: (b, 0, 0)),
            pl.BlockSpec((1, N_SSM_HEADS, SSM_HEAD_DIM, D_STATE), lambda b, c: (b, 0, 0, 0)),
        ],
        out_shape=[
            jax.ShapeDtypeStruct((m, D_MODEL), BF16),
            jax.ShapeDtypeStruct((bsz, CONV_W - 1, D_CONV), F32),
            jax.ShapeDtypeStruct((bsz, N_SSM_HEADS, SSM_HEAD_DIM, D_STATE), F32),
        ],
        scratch_shapes=[
            pltpu.VMEM((q + SUBLANE, D_CONV), F32),
            pltpu.VMEM((N_SSM_GROUPS, D_STATE, D_GROUP), F32),
            pltpu.VMEM((q, D_SSM), F32),
        ],
        compiler_params=_params(2),
        name="ssd",
    )(main, main, main, main, dt, conv_hist, h0,
      lp["conv_w"], lp["conv_b"], lp["dt_bias"], lp["a_log"], lp["d_skip_e"], lp["ssm_norm_w"],
      lp["expand"])


def _lambda(lq1, lk1, lq2, lk2, lambda_init):
    return (jnp.exp(jnp.sum(lq1[...] * lk1[...], axis=-1, keepdims=True))
            - jnp.exp(jnp.sum(lq2[...] * lk2[...], axis=-1, keepdims=True)) + lambda_init)


def _diff_finish(o1, o2, lam, sw_ref, g, lambda_init):
    o = o1 - lam * o2
    on = (o * _rms_scale(o, SUBLN_EPS) * sw_ref[...]) * (1.0 - lambda_init)
    return (on * _silu(g.astype(F32))).astype(BF16)


ATTN_QT = 256


def _attn_scores(k_ref, q_ref, mp, nt, diagonal):
    cols = slice(mp * DIFF_HEAD_DIM, (mp + 1) * DIFF_HEAD_DIM)
    qs = slice(nt * ATTN_QT, (nt + 1) * ATTN_QT)
    kw = (nt + 1) * ATTN_QT if diagonal else k_ref.shape[0]
    s = lax.dot_general(k_ref[0:kw, cols], q_ref[qs, cols], (((1,), (1,)), ((), ())),
                        preferred_element_type=F32)
    if diagonal:
        r = lax.broadcasted_iota(jnp.int32, (ATTN_QT, ATTN_QT), 0)
        q = lax.broadcasted_iota(jnp.int32, (ATTN_QT, ATTN_QT), 1)
        tail = jnp.where(_chunk_of(r) <= _chunk_of(q), s[kw - ATTN_QT:kw], -jnp.inf)
        s = tail if kw == ATTN_QT else jnp.concatenate([s[0:kw - ATTN_QT], tail], axis=0)
    return s


def _attn_softmax(s, m_scr, l_scr, mp, nt):
    c = (DIFF_HEAD_DIM ** -0.5) * math.log2(math.e)
    qs = slice(nt * ATTN_QT, (nt + 1) * ATTN_QT)
    m_prev = m_scr[mp, :, qs]
    m_new = jnp.maximum(m_prev, jnp.max(s, axis=0, keepdims=True))
    p = jnp.exp2((s - m_new) * c)
    alpha = jnp.exp2((m_prev - m_new) * c)
    l_scr[mp, :, qs] = alpha * l_scr[mp, :, qs] + jnp.sum(p, axis=0, keepdims=True)
    m_scr[mp, :, qs] = m_new
    return p.astype(BF16), alpha


def _attn_value(vt, p, alpha, acc_scr, mp, nt):
    qs = slice(nt * ATTN_QT, (nt + 1) * ATTN_QT)
    kw = p.shape[0]
    acc_scr[mp, :, qs] = alpha * acc_scr[mp, :, qs] + jnp.dot(vt[:, 0:kw], p, preferred_element_type=F32)


def _diff_attn_kernel(q_ref, k_ref, v_ref, g_ref, lq1, lk1, lq2, lk2, sw_ref, y_hbm, o_ref,
                      m_scr, l_scr, acc_scr, *, tq, tk, lambda_init):
    del y_hbm
    qi = pl.program_id(2)
    ki = pl.program_id(3)
    assert tq == tk and tq % ATTN_QT == 0 and ATTN_QT % CHUNK == 0

    @pl.when(ki == 0)
    def _():
        m_scr[...] = jnp.full(m_scr.shape, -jnp.inf, F32)
        l_scr[...] = jnp.zeros(l_scr.shape, F32)
        acc_scr[...] = jnp.zeros(acc_scr.shape, F32)

    def block(diagonal):
        vt = v_ref[...].T
        chains = [(mp, nt) for mp in range(2) for nt in range(tq // ATTN_QT)]
        scores = [_attn_scores(k_ref, q_ref, mp, nt, diagonal) for mp, nt in chains]
        probs = [_attn_softmax(s, m_scr, l_scr, mp, nt) for s, (mp, nt) in zip(scores, chains)]
        for (p, alpha), (mp, nt) in zip(probs, chains):
            _attn_value(vt, p, alpha, acc_scr, mp, nt)

    @pl.when(ki < qi)
    def _():
        block(False)

    @pl.when(ki == qi)
    def _():
        block(True)
        lam = _lambda(lq1, lk1, lq2, lk2, lambda_init)
        o1 = (acc_scr[0] / l_scr[0]).T
        o2 = (acc_scr[1] / l_scr[1]).T
        o_ref[...] = _diff_finish(o1, o2, lam, sw_ref, g_ref[...], lambda_init)


def _diff_attn_prompt(main, y, lp, bsz, length, lambda_init, tq):
    tk = tq
    nq = length // tq
    m = bsz * length
    vec = pl.BlockSpec((1, DIFF_HEAD_DIM), lambda b, h, qi, ki: (0, 0))
    return pl.pallas_call(
        functools.partial(_diff_attn_kernel, tq=tq, tk=tk, lambda_init=lambda_init),
        grid=(bsz, N_DIFF_HEADS, nq, nq),
        in_specs=[
            pl.BlockSpec((tq, HEAD_W), lambda b, h, qi, ki: (b * nq + qi, COL_QD // HEAD_W + h)),
            pl.BlockSpec((tk, HEAD_W), lambda b, h, qi, ki: (b * nq + jnp.minimum(ki, qi), COL_KD // HEAD_W + h)),
            pl.BlockSpec((tk, HEAD_W), lambda b, h, qi, ki: (b * nq + jnp.minimum(ki, qi), COL_VD // HEAD_W + h)),
            pl.BlockSpec((tq, HEAD_W), lambda b, h, qi, ki: (b * nq + qi, COL_GD // HEAD_W + h)),
            vec, vec, vec, vec,
            pl.BlockSpec((1, HEAD_W), lambda b, h, qi, ki: (0, 0)),
            pl.BlockSpec(memory_space=pl.ANY),
        ],
        out_specs=pl.BlockSpec((tq, HEAD_W), lambda b, h, qi, ki: (b * nq + qi, D_SSM // HEAD_W + h)),
        out_shape=jax.ShapeDtypeStruct((m, D_MODEL), BF16),
        input_output_aliases={9: 0},
        scratch_shapes=[
            pltpu.VMEM((2, 1, tq), F32),
            pltpu.VMEM((2, 1, tq), F32),
            pltpu.VMEM((2, HEAD_W, tq), F32),
        ],
        compiler_params=_params(4),
        name="diff_attn_prompt",
    )(main, main, main, main, lp["lambda_q1"], lp["lambda_k1"], lp["lambda_q2"], lp["lambda_k2"],
      lp["subln_w"], y)


def _diff_attn_decode_kernel(q_ref, kn_ref, vn_ref, g_ref, kp_ref, vp_ref, lq1, lk1, lq2, lk2, sw_ref,
                             y_hbm, o_ref, *, past, lambda_init):
    del y_hbm
    scale = DIFF_HEAD_DIM ** -0.5
    lq = q_ref.shape[0]
    nt_dims = (((1,), (1,)), ((), ()))
    mask_p = (_chunk_of(lax.broadcasted_iota(jnp.int32, (lq, past), 1))
              <= _chunk_of(past + lax.broadcasted_iota(jnp.int32, (lq, past), 0)))
    mask_n = (_chunk_of(past + lax.broadcasted_iota(jnp.int32, (lq, lq), 1))
              <= _chunk_of(past + lax.broadcasted_iota(jnp.int32, (lq, lq), 0)))
    lam = _lambda(lq1, lk1, lq2, lk2, lambda_init)
    for h in range(N_DIFF_HEADS):
        hs = slice(h * HEAD_W, (h + 1) * HEAD_W)
        vp = _load_flat(vp_ref, past, h * HEAD_W, HEAD_W, halves_split=True).astype(BF16)
        vn = vn_ref[:, hs]
        outs = []
        for mp in range(2):
            c0 = h * HEAD_W + mp * DIFF_HEAD_DIM
            qm = q_ref[:, c0:c0 + DIFF_HEAD_DIM]
            kp = _load_flat(kp_ref, past, c0, DIFF_HEAD_DIM, halves_split=False).astype(BF16)
            sp = lax.dot_general(qm, kp, nt_dims, preferred_element_type=F32) * scale
            sn = lax.dot_general(qm, kn_ref[:, c0:c0 + DIFF_HEAD_DIM], nt_dims,
                                 preferred_element_type=F32) * scale
            sp = jnp.where(mask_p, sp, -jnp.inf)
            sn = jnp.where(mask_n, sn, -jnp.inf)
            mx = jnp.maximum(jnp.max(sp, axis=-1, keepdims=True), jnp.max(sn, axis=-1, keepdims=True))
            pp = jnp.exp(sp - mx)
            pn = jnp.exp(sn - mx)
            denom = jnp.sum(pp, axis=-1, keepdims=True) + jnp.sum(pn, axis=-1, keepdims=True)
            o = (jnp.dot(pp.astype(BF16), vp, preferred_element_type=F32)
                 + jnp.dot(pn.astype(BF16), vn, preferred_element_type=F32))
            outs.append(o / denom)
        o_ref[:, hs] = _diff_finish(outs[0], outs[1], lam, sw_ref, g_ref[:, hs], lambda_init)


def _diff_attn_decode(main, y, k_past, v_past, b_off, lp, bsz, length, past, lambda_init):
    m = bsz * length
    vec = pl.BlockSpec((1, DIFF_HEAD_DIM), lambda b: (0, 0))
    return pl.pallas_call(
        functools.partial(_diff_attn_decode_kernel, past=past, lambda_init=lambda_init),
        grid=(bsz,),
        in_specs=[
            pl.BlockSpec((length, D_DIFF), lambda b: (b, COL_QD // D_DIFF)),
            pl.BlockSpec((length, D_DIFF), lambda b: (b, COL_KD // D_DIFF)),
            pl.BlockSpec((length, D_DIFF), lambda b: (b, COL_VD // D_DIFF)),
            pl.BlockSpec((length, D_DIFF), lambda b: (b, COL_GD // D_DIFF)),
            pl.BlockSpec((past * ROW_TILES, LANE), lambda b: (b_off + b, 0)),
            pl.BlockSpec((past * ROW_TILES, LANE), lambda b: (b_off + b, 0)),
            vec, vec, vec, vec,
            pl.BlockSpec((1, HEAD_W), lambda b: (0, 0)),
            pl.BlockSpec(memory_space=pl.ANY),
        ],
        out_specs=pl.BlockSpec((length, D_DIFF), lambda b: (b, D_SSM // D_DIFF)),
        out_shape=jax.ShapeDtypeStruct((m, D_MODEL), BF16),
        input_output_aliases={11: 0},
        compiler_params=_params(1),
        name="diff_attn_decode",
    )(main, main, main, main, k_past, v_past,
      lp["lambda_q1"], lp["lambda_k1"], lp["lambda_q2"], lp["lambda_k2"], lp["subln_w"], y)


def _mem_attn_kernel(q_ref, g_ref, mk_ref, mv_ref, y_hbm, o_ref, *, n_mem):
    del y_hbm
    scale = MEM_HEAD_DIM ** -0.5
    for h in range(N_MEM_HEADS):
        hs = slice(h * MEM_HEAD_DIM, (h + 1) * MEM_HEAD_DIM)
        mk = _load_flat(mk_ref, n_mem, h * MEM_HEAD_DIM, MEM_HEAD_DIM, halves_split=True).astype(BF16)
        mv = _load_flat(mv_ref, n_mem, h * MEM_HEAD_DIM, MEM_HEAD_DIM, halves_split=True).astype(BF16)
        s = lax.dot_general(q_ref[:, hs], mk, (((1,), (1,)), ((), ())), preferred_element_type=F32) * scale
        p = jnp.exp(s - jnp.max(s, axis=-1, keepdims=True))
        pr = p / jnp.sum(p, axis=-1, keepdims=True)
        o = jnp.dot(pr.astype(BF16), mv, preferred_element_type=F32)
        o_ref[:, hs] = (o * _silu(g_ref[:, hs].astype(F32))).astype(BF16)


def _mem_attn(main, y, mk, mv, b_off, bsz, length, n_mem, tq):
    nq = length // tq
    m = bsz * length
    return pl.pallas_call(
        functools.partial(_mem_attn_kernel, n_mem=n_mem),
        grid=(bsz, nq),
        in_specs=[
            pl.BlockSpec((tq, D_MEM), lambda b, qi: (b * nq + qi, COL_QM // D_MEM)),
            pl.BlockSpec((tq, D_MEM), lambda b, qi: (b * nq + qi, COL_GM // D_MEM)),
            pl.BlockSpec((n_mem * ROW_TILES, LANE), lambda b, qi: (b_off + b, 0)),
            pl.BlockSpec((n_mem * ROW_TILES, LANE), lambda b, qi: (b_off + b, 0)),
            pl.BlockSpec(memory_space=pl.ANY),
        ],
        out_specs=pl.BlockSpec((tq, D_MEM), lambda b, qi: (b * nq + qi, (D_SSM + D_DIFF) // D_MEM)),
        out_shape=jax.ShapeDtypeStruct((m, D_MODEL), BF16),
        input_output_aliases={4: 0},
        compiler_params=_params(2),
        name="mem_attn",
    )(main, main, mk, mv, y)


def _out_proj_kernel(y_ref, w_ref, x_ref, nw_ref, o_ref, *, nj, tn):
    j = pl.program_id(1)
    acc = jnp.dot(y_ref[...], w_ref[...], preferred_element_type=F32)
    o_ref[:, pl.ds(pl.multiple_of(j * tn, tn), tn)] = acc

    @pl.when(j == nj - 1)
    def _():
        tm = o_ref.shape[0]
        step = min(NORM_ROWS, tm)
        for r in range(0, tm, step):
            o = o_ref[r:r + step, :]
            o_ref[r:r + step, :] = x_ref[r:r + step, :] + o * _rms_scale(o, NORM_EPS) * nw_ref[...]


def _out_proj(y, w_out, x2d, norm_w, tm, tn):
    m = x2d.shape[0]
    nj = D_MODEL // tn
    return pl.pallas_call(
        functools.partial(_out_proj_kernel, nj=nj, tn=tn),
        grid=(m // tm, nj),
        in_specs=[
            pl.BlockSpec((tm, D_MODEL), lambda i, j: (i, 0)),
            pl.BlockSpec((D_MODEL, tn), lambda i, j: (0, j)),
            pl.BlockSpec((tm, D_MODEL), lambda i, j: (i, 0)),
            pl.BlockSpec((1, D_MODEL), lambda i, j: (0, 0)),
        ],
        out_specs=pl.BlockSpec((tm, D_MODEL), lambda i, j: (i, 0)),
        out_shape=jax.ShapeDtypeStruct((m, D_MODEL), F32),
        compiler_params=_params(2),
        name="out_proj",
    )(y, w_out, x2d, norm_w)


def _mixer_layer(x2d, bsz, length, layer, depth, kv_all, b_off, mem_off, conv_hist, h0, kv_past, past, mk,
                 mv, n_mem, lp, lambda_init, tiles):
    main, k32, v32, dt = _in_proj(x2d, lp["norm_pre_w"], lp["w_main"], lp["w_dt"], tiles["proj_tm"],
                                  layer, depth, kv_all)
    y, new_conv, h_new = _ssd(main, dt, conv_hist, h0, b_off, lp, bsz, length, tiles["ssd_q"])
    if kv_past is None:
        y = _diff_attn_prompt(main, y, lp, bsz, length, lambda_init, tiles["attn_tq"])
    else:
        y = _diff_attn_decode(main, y, kv_past[0], kv_past[1], b_off, lp, bsz, length, past, lambda_init)
    y = _mem_attn(main, y, mk, mv, mem_off, bsz, length, n_mem, tiles["mem_tq"])
    x_new = _out_proj(y, lp["w_out"], x2d, lp["norm_post_w"], tiles["out_tm"], tiles["out_tn"])
    return x_new, new_conv, h_new, (k32, v32)


def _layer_params(l, norm_pre_w, norm_post_w, w_in_t, conv_w, conv_b, dt_bias, a_log, d_skip,
                  ssm_norm_w, lambda_q1, lambda_k1, lambda_q2, lambda_k2, subln_w, mem_norm_w,
                  w_mem_kv, w_out):
    w_main, w_dt = _w_prep(w_in_t, l)
    pad_heads = lambda v: jnp.pad(v, (0, LANE - N_SSM_HEADS)).reshape(1, LANE)
    head_of_channel = jnp.arange(D_SSM) // SSM_HEAD_DIM
    expand = (jnp.arange(LANE)[:, None] == head_of_channel[None, :]).astype(BF16)
    return {
        "norm_pre_w": norm_pre_w[l].reshape(1, D_MODEL),
        "norm_post_w": norm_post_w[l].reshape(1, D_MODEL),
        "w_main": w_main,
        "w_dt": w_dt,
        "conv_w": conv_w[l],
        "conv_b": conv_b[l].reshape(1, D_CONV),
        "dt_bias": pad_heads(dt_bias[l]),
        "a_log": pad_heads(a_log[l]),
        "d_skip_e": jnp.repeat(d_skip[l], SSM_HEAD_DIM).reshape(1, D_SSM),
        "ssm_norm_w": ssm_norm_w[l].reshape(1, D_SSM),
        "expand": expand,
        "lambda_q1": lambda_q1[l].reshape(1, DIFF_HEAD_DIM),
        "lambda_k1": lambda_k1[l].reshape(1, DIFF_HEAD_DIM),
        "lambda_q2": lambda_q2[l].reshape(1, DIFF_HEAD_DIM),
        "lambda_k2": lambda_k2[l].reshape(1, DIFF_HEAD_DIM),
        "subln_w": subln_w[l].reshape(1, HEAD_W),
        "mem_norm_w": mem_norm_w[l].reshape(1, D_MODEL),
        "w_mem_kv": _layer_bf16(w_mem_kv, l),
        "w_out": _layer_bf16(w_out, l),
    }


def _tiles(bsz, length):
    m = bsz * length
    return {
        "proj_tm": min(512, m),
        "ssd_q": min(128, length),
        "attn_tq": min(1024, length),
        "mem_tq": min(1024, length),
        "out_tm": min(512, m),
        "out_tn": 512,
    }


def kernel(x_prompt, x_sample, mem_prompt, cache_conv, state_ssm, cache_k, cache_v, cache_mem_k,
           cache_mem_v, norm_pre_w, norm_post_w, w_in, conv_w, conv_b, dt_bias, a_log, d_skip,
           ssm_norm_w, lambda_q1, lambda_k1, lambda_q2, lambda_k2, subln_w, mem_norm_w, w_mem_kv,
           w_out):
    depth = w_in.shape[0]
    bp, lp_, _ = x_prompt.shape
    bs, ls, _ = x_sample.shape
    n_mem = mem_prompt.shape[1]
    past = cache_k.shape[2]
    xp = x_prompt.reshape(bp * lp_, D_MODEL)
    xs = x_sample.reshape(bs * ls, D_MODEL)
    mem2d = mem_prompt.reshape(bp * n_mem, D_MODEL)
    tiles_p = _tiles(bp, lp_)
    tiles_s = _tiles(bs, ls)
    conv0 = jnp.zeros((bp, CONV_W - 1, D_CONV), F32)
    h_zero = jnp.zeros((bp, N_SSM_HEADS, SSM_HEAD_DIM, D_STATE), F32)
    conv_all = cache_conv.reshape(depth * bs, CONV_W - 1, D_CONV)
    ssm_all = state_ssm.reshape(depth * bs, N_SSM_HEADS, SSM_HEAD_DIM, D_STATE)
    kv_all = (cache_k.reshape(-1, LANE), _to_flat_halves(cache_v))
    mem_k_all = _to_flat_halves(cache_mem_k)
    mem_v_all = _to_flat_halves(cache_mem_v)
    outs = [[] for _ in range(6)]
    kv_p = kv_s = None
    w_in_t = jnp.swapaxes(w_in, 1, 2)
    for l in range(depth):
        lp = _layer_params(l, norm_pre_w, norm_post_w, w_in_t, conv_w, conv_b, dt_bias, a_log, d_skip,
                           ssm_norm_w, lambda_q1, lambda_k1, lambda_q2, lambda_k2, subln_w,
                           mem_norm_w, w_mem_kv, w_out)
        lambda_init = 0.8 - 0.6 * math.exp(-0.3 * l)
        mk_p, mv_p = _mem_kv(mem2d, lp["mem_norm_w"], lp["w_mem_kv"], min(512, bp * n_mem))
        xp, c_p, h_p, kv_p = _mixer_layer(xp, bp, lp_, l, depth, kv_p, 0, 0, conv0, h_zero, None, 0, mk_p,
                                          mv_p, n_mem, lp, lambda_init, tiles_p)
        xs, c_s, h_s, kv_s = _mixer_layer(xs, bs, ls, l, depth, kv_s, l * bs, l * bs, conv_all, ssm_all,
                                          kv_all, past, mem_k_all, mem_v_all, n_mem, lp, lambda_init,
                                          tiles_s)
        for lst, val in zip(outs, (c_p, h_p, mk_p, mv_p, c_s, h_s)):
            lst.append(val)
    st = [jnp.stack(o) for o in outs]
    return (
        xp.reshape(bp, lp_, D_MODEL),
        xs.reshape(bs, ls, D_MODEL),
        st[0],
        st[1],
        kv_p[0].reshape(depth, bp, lp_, N_DIFF_HEADS, 2, DIFF_HEAD_DIM),
        _from_flat_halves(kv_p[1], (depth, bp, lp_, N_DIFF_HEADS, 2 * DIFF_HEAD_DIM)),
        _from_flat_halves(st[2], (depth, bp, n_mem, N_MEM_HEADS, MEM_HEAD_DIM)),
        _from_flat_halves(st[3], (depth, bp, n_mem, N_MEM_HEADS, MEM_HEAD_DIM)),
        st[4],
        st[5],
        kv_s[0].reshape(depth, bs, ls, N_DIFF_HEADS, 2, DIFF_HEAD_DIM),
        _from_flat_halves(kv_s[1], (depth, bs, ls, N_DIFF_HEADS, 2 * DIFF_HEAD_DIM)),
    )
```

```python
import functools
import math

import jax
import jax.numpy as jnp
from jax import lax
from jax.experimental import pallas as pl
from jax.experimental.pallas import tpu as pltpu

F32 = jnp.float32
BF16 = jnp.bfloat16

D_MODEL = 4096
D_SSM = 2048
SSM_HEAD_DIM = 64
N_SSM_HEADS = 32
N_SSM_GROUPS = 4
HEADS_PER_GROUP = N_SSM_HEADS // N_SSM_GROUPS
D_GROUP = D_SSM // N_SSM_GROUPS
D_STATE = 128
CONV_W = 4
D_BC = N_SSM_GROUPS * D_STATE
D_CONV = D_SSM + 2 * D_BC
D_DIFF = 1024
N_DIFF_HEADS = 4
DIFF_HEAD_DIM = 128
D_MEM = 1024
N_MEM_HEADS = 4
MEM_HEAD_DIM = 256
CHUNK = 64
NORM_EPS = 1e-6
SUBLN_EPS = 1e-5

LANE = 128
SUBLANE = 8
VMEM_LIMIT = 56 * 1024 * 1024

HEAD_W = 2 * DIFF_HEAD_DIM
COL_KD = 0
COL_VD = COL_KD + D_DIFF
COL_Z = COL_VD + D_DIFF
COL_X = COL_Z + D_SSM
COL_B = COL_X + D_SSM
COL_C = COL_B + D_BC
COL_QD = COL_C + D_BC
COL_GD = COL_QD + D_DIFF
COL_QM = COL_GD + D_DIFF
COL_GM = COL_QM + D_MEM
D_MAIN = COL_GM + D_MEM
PROJ_TN = 1024
K_TILE = COL_KD // PROJ_TN
V_TILE = COL_VD // PROJ_TN


def _params(n_axes):
    return pltpu.CompilerParams(dimension_semantics=("arbitrary",) * n_axes,
                                vmem_limit_bytes=VMEM_LIMIT)


def _silu(x):
    return x * jax.nn.sigmoid(x)


def _chunk_of(pos):
    assert CHUNK & (CHUNK - 1) == 0
    return lax.shift_right_logical(pos, jnp.int32(CHUNK.bit_length() - 1))


def _rms_scale(x, eps):
    return lax.rsqrt(jnp.mean(x * x, axis=-1, keepdims=True) + eps)


ROW_TILES = 1024 // LANE
N_HEADS_1024 = 1024 // HEAD_W


def _flat_row(c, halves_split):
    head, half = divmod(c, 2)
    return half * N_HEADS_1024 + head if halves_split else c


def _store_flat(ref, val, halves_split):
    rows = val.shape[0]
    for c in range(ROW_TILES):
        ref[pl.ds(_flat_row(c, halves_split), rows, stride=ROW_TILES), :] = val[:, c * LANE:(c + 1) * LANE]


def _load_flat(ref, rows, col0, width, halves_split):
    parts = [ref[pl.ds(_flat_row(col0 // LANE + c, halves_split), rows, stride=ROW_TILES), :]
             for c in range(width // LANE)]
    return parts[0] if len(parts) == 1 else jnp.concatenate(parts, axis=1)


def _to_flat_halves(a):
    return a.reshape(-1, N_HEADS_1024, 2, LANE).transpose(0, 2, 1, 3).reshape(-1, LANE)


def _from_flat_halves(flat, shape):
    return flat.reshape(-1, 2, N_HEADS_1024, LANE).transpose(0, 2, 1, 3).reshape(shape)


NORM_ROWS = 128


def _norm_rows(x_ref, nw_ref, h_scr):
    tm = x_ref.shape[0]
    step = min(NORM_ROWS, tm)
    for r in range(0, tm, step):
        x = x_ref[r:r + step, :]
        h_scr[r:r + step, :] = (x * _rms_scale(x, NORM_EPS) * nw_ref[...]).astype(BF16)


O_DT = D_SSM + D_CONV
O_Q = O_DT + N_SSM_HEADS
MAIN_PIECES = ((COL_Z, 0, O_DT), (COL_QD, O_Q, D_DIFF), (COL_KD, O_Q + D_DIFF, D_DIFF),
               (COL_VD, O_Q + 2 * D_DIFF, D_DIFF), (COL_GD, O_Q + 3 * D_DIFF, D_DIFF),
               (COL_QM, O_Q + 4 * D_DIFF, D_MEM), (COL_GM, O_Q + 4 * D_DIFF + D_MEM, D_MEM))
PREP_COLS = 256


def _prep_source_columns():
    src = []
    for dst, first, width in sorted(MAIN_PIECES):
        assert dst == len(src) * PREP_COLS
        src += list(range(first, first + width, PREP_COLS))
    assert len(src) * PREP_COLS == D_MAIN
    return jnp.asarray(src, jnp.int32)


def _w_prep_kernel(src_ref, w_ref, wdt_ref, main_ref, dt_ref):
    del src_ref
    main_ref[...] = w_ref[0].T.astype(BF16)

    @pl.when(pl.program_id(0) == 0)
    def _():
        lane = lax.broadcasted_iota(jnp.int32, (1, LANE), 1)
        dt_ref[...] = jnp.where(lane < N_SSM_HEADS, wdt_ref[...].T, 0.0).astype(BF16)


def _w_prep(w_in_t, layer):
    return pl.pallas_call(
        _w_prep_kernel,
        grid_spec=pltpu.PrefetchScalarGridSpec(
            num_scalar_prefetch=1,
            grid=(D_MAIN // PREP_COLS,),
            in_specs=[
                pl.BlockSpec((pl.Element(1), pl.Element(PREP_COLS), pl.Element(D_MODEL)),
                             lambda j, src: (layer, pl.multiple_of(src[j], N_SSM_HEADS), 0)),
                pl.BlockSpec((None, LANE, D_MODEL), lambda j, src: (layer, O_DT // LANE, 0)),
            ],
            out_specs=[
                pl.BlockSpec((D_MODEL, PREP_COLS), lambda j, src: (0, j)),
                pl.BlockSpec((D_MODEL, LANE), lambda j, src: (0, 0)),
            ],
        ),
        out_shape=[
            jax.ShapeDtypeStruct((D_MODEL, D_MAIN), BF16),
            jax.ShapeDtypeStruct((D_MODEL, LANE), BF16),
        ],
        compiler_params=_params(1),
        name="w_prep",
    )(_prep_source_columns(), w_in_t, w_in_t)


CAST_ROWS = 512


def _cast_kernel(w_ref, o_ref):
    o_ref[...] = w_ref[...].astype(BF16)


def _layer_bf16(w, layer):
    _, rows, cols = w.shape
    return pl.pallas_call(
        _cast_kernel,
        grid=(rows // CAST_ROWS,),
        in_specs=[pl.BlockSpec((None, CAST_ROWS, cols), lambda i: (layer, i, 0))],
        out_specs=pl.BlockSpec((CAST_ROWS, cols), lambda i: (i, 0)),
        out_shape=jax.ShapeDtypeStruct((rows, cols), BF16),
        compiler_params=_params(1),
        name="cast_bf16",
    )(w)


def _in_proj_kernel(x_ref, nw_ref, w_ref, wdt_ref, *refs):
    main_ref, k_ref, v_ref, dt_ref, h_scr = refs[-5:]
    j = pl.program_id(1)

    @pl.when(j == 0)
    def _():
        _norm_rows(x_ref, nw_ref, h_scr)
        dt_ref[...] = jnp.dot(h_scr[...], wdt_ref[...], preferred_element_type=F32)

    acc = jnp.dot(h_scr[...], w_ref[...], preferred_element_type=F32)
    main_ref[...] = acc.astype(BF16)

    @pl.when(j == K_TILE)
    def _():
        _store_flat(k_ref, acc, halves_split=False)

    @pl.when(j == V_TILE)
    def _():
        _store_flat(v_ref, acc, halves_split=True)


def _in_proj(x2d, norm_w, w_main, w_dt, tm, layer, depth, kv_all):
    m = x2d.shape[0]
    ni = m // tm
    grid = (ni, D_MAIN // PROJ_TN)

    def kv_spec(tile):
        return pl.BlockSpec((tm * ROW_TILES, LANE),
                            lambda i, j: (layer * ni + jnp.minimum(i + (j > tile), ni - 1), 0))

    kv_shape = jax.ShapeDtypeStruct((depth * m * ROW_TILES, LANE), F32)
    in_specs = [
        pl.BlockSpec((tm, D_MODEL), lambda i, j: (i, 0)),
        pl.BlockSpec((1, D_MODEL), lambda i, j: (0, 0)),
        pl.BlockSpec((D_MODEL, PROJ_TN), lambda i, j: (0, j)),
        pl.BlockSpec((D_MODEL, LANE), lambda i, j: (0, 0)),
    ]
    operands = [x2d, norm_w, w_main, w_dt]
    aliases = {}
    if kv_all is not None:
        in_specs += [pl.BlockSpec(memory_space=pl.ANY)] * 2
        aliases = {len(operands): 1, len(operands) + 1: 2}
        operands += list(kv_all)
    return pl.pallas_call(
        _in_proj_kernel,
        grid=grid,
        in_specs=in_specs,
        out_specs=[
            pl.BlockSpec((tm, PROJ_TN), lambda i, j: (i, j)),
            kv_spec(K_TILE),
            kv_spec(V_TILE),
            pl.BlockSpec((tm, LANE), lambda i, j: (i, 0)),
        ],
        out_shape=[
            jax.ShapeDtypeStruct((m, D_MAIN), BF16),
            kv_shape,
            kv_shape,
            jax.ShapeDtypeStruct((m, LANE), F32),
        ],
        scratch_shapes=[pltpu.VMEM((tm, D_MODEL), BF16)],
        input_output_aliases=aliases,
        compiler_params=_params(2),
        name="in_proj",
    )(*operands)


def _mem_kv_kernel(x_ref, nw_ref, w_ref, mk_ref, mv_ref, h_scr):
    j = pl.program_id(1)

    @pl.when(j == 0)
    def _():
        _norm_rows(x_ref, nw_ref, h_scr)

    acc = jnp.dot(h_scr[...], w_ref[...], preferred_element_type=F32)

    @pl.when(j == 0)
    def _():
        _store_flat(mk_ref, acc, halves_split=True)

    @pl.when(j == 1)
    def _():
        _store_flat(mv_ref, acc, halves_split=True)


def _mem_kv(mem2d, norm_w, w_kv, tm):
    m = mem2d.shape[0]
    return pl.pallas_call(
        _mem_kv_kernel,
        grid=(m // tm, 2),
        in_specs=[
            pl.BlockSpec((tm, D_MODEL), lambda i, j: (i, 0)),
            pl.BlockSpec((1, D_MODEL), lambda i, j: (0, 0)),
            pl.BlockSpec((D_MODEL, D_MEM), lambda i, j: (0, j)),
        ],
        out_specs=[
            pl.BlockSpec((tm * ROW_TILES, LANE), lambda i, j: (i, 0)),
            pl.BlockSpec((tm * ROW_TILES, LANE), lambda i, j: (i, 0)),
        ],
        out_shape=[jax.ShapeDtypeStruct((m * ROW_TILES, LANE), F32)] * 2,
        scratch_shapes=[pltpu.VMEM((tm, D_MODEL), BF16)],
        compiler_params=_params(2),
        name="mem_kv",
    )(mem2d, norm_w, w_kv)


def _split_bf16(x, n):
    parts = []
    r = x
    for _ in range(n - 1):
        p = r.astype(BF16)
        parts.append(p)
        r = r - p.astype(F32)
    parts.append(r.astype(BF16))
    return parts


def _sum_rows(a, n, rows):
    out = a[(n - 1) * rows:n * rows]
    for i in range(n - 2, -1, -1):
        out = out + a[i * rows:(i + 1) * rows]
    return out


def _ssd_kernel(z_ref, xx_ref, xb_ref, xc_ref, dt_ref, hist_ref, h0_ref,
                cw_ref, cb_ref, dtb_ref, alog_ref, dskip_ref, nw_ref, e_ref,
                y_ref, nconv_ref, hnew_ref, xp_scr, st_scr, y_scr, *, q):
    c = pl.program_id(1)
    nc = pl.num_programs(1)
    hist_row = SUBLANE - (CONV_W - 1)

    @pl.when(c == 0)
    def _():
        xp_scr[hist_row:SUBLANE, :] = hist_ref[0]
        for g in range(N_SSM_GROUPS):
            hg = h0_ref[0, g * HEADS_PER_GROUP:(g + 1) * HEADS_PER_GROUP]
            st_scr[g] = hg.reshape(D_GROUP, D_STATE).T

    xp_scr[SUBLANE:SUBLANE + q, 0:D_SSM] = xx_ref[...].astype(F32)
    xp_scr[SUBLANE:SUBLANE + q, D_SSM:D_SSM + D_BC] = xb_ref[...].astype(F32)
    xp_scr[SUBLANE:SUBLANE + q, D_SSM + D_BC:D_CONV] = xc_ref[...].astype(F32)
    acc = cb_ref[...] + xp_scr[hist_row:hist_row + q, :] * cw_ref[0:1, :]
    for j in range(1, CONV_W):
        acc = acc + xp_scr[hist_row + j:hist_row + j + q, :] * cw_ref[j:j + 1, :]
    xbc = _silu(acc)
    xs = xbc[:, 0:D_SSM]
    xs_bf = xs.astype(BF16)
    bm = xbc[:, D_SSM:D_SSM + D_BC].astype(BF16)
    cm = xbc[:, D_SSM + D_BC:D_CONV].astype(BF16)

    @pl.when(c == nc - 1)
    def _():
        nconv_ref[0] = xp_scr[q + hist_row:q + SUBLANE, :]

    xp_scr[0:SUBLANE, :] = xp_scr[q:q + SUBLANE, :]

    dtv = jax.nn.softplus(dt_ref[...] + dtb_ref[...])
    a = -jnp.exp(alog_ref[...])
    da = dtv * a
    ri = lax.broadcasted_iota(jnp.int32, (q, q), 0)
    ci = lax.broadcasted_iota(jnp.int32, (q, q), 1)
    causal = ri >= ci
    tril = jnp.where(causal, 1.0, 0.0).astype(BF16)
    triu = jnp.where(ri <= ci, 1.0, 0.0).astype(BF16)
    eye = jnp.where(ri == ci, 1.0, 0.0).astype(BF16)
    da3 = jnp.concatenate(_split_bf16(da, 3), axis=1)
    dt3 = jnp.concatenate(_split_bf16(dtv, 3), axis=1)
    cum3 = jnp.dot(tril, da3, preferred_element_type=F32)
    cum = cum3[:, 2 * LANE:3 * LANE] + cum3[:, LANE:2 * LANE] + cum3[:, 0:LANE]
    tn_dims = (((0,), (0,)), ((), ()))
    cum_t = _sum_rows(lax.dot_general(da3, triu, tn_dims, preferred_element_type=F32), 3, LANE)
    dt_t = _sum_rows(lax.dot_general(dt3, eye, tn_dims, preferred_element_type=F32), 3, LANE)

    ecum = jnp.exp(cum)
    dd = dtv * jnp.exp(cum[q - 1:q, :] - cum)
    ex_in = jnp.concatenate(_split_bf16(ecum, 2) + _split_bf16(dd, 2), axis=0)
    ex = jnp.dot(ex_in, e_ref[...], preferred_element_type=F32)
    ecum_e = ex[q:2 * q] + ex[0:q]
    dd_e = ex[3 * q:4 * q] + ex[2 * q:3 * q]
    xdd = (xs * dd_e).astype(BF16)

    lane = lax.broadcasted_iota(jnp.int32, (q, LANE), 1)
    lo_half = lane < SSM_HEAD_DIM
    nt_dims = (((1,), (1,)), ((), ()))
    for g in range(N_SSM_GROUPS):
        bg = bm[:, g * D_STATE:(g + 1) * D_STATE]
        cg = cm[:, g * D_STATE:(g + 1) * D_STATE]
        gs = slice(g * D_GROUP, (g + 1) * D_GROUP)
        cbg = lax.dot_general(cg, bg, nt_dims, preferred_element_type=F32)
        st = st_scr[g]
        y_off = jnp.dot(cg, st.astype(BF16), preferred_element_type=F32) * ecum_e[:, gs]
        st_scr[g] = st * ecum_e[q - 1:q, gs] + lax.dot_general(
            bg, xdd[:, gs], tn_dims, preferred_element_type=F32)
        for t in range(HEADS_PER_GROUP // 2):
            ws = []
            for h in (g * HEADS_PER_GROUP + 2 * t, g * HEADS_PER_GROUP + 2 * t + 1):
                seg = cum[:, h:h + 1] - cum_t[h:h + 1, :]
                lmat = jnp.exp(jnp.where(causal, seg, -jnp.inf))
                ws.append((cbg * lmat * dt_t[h:h + 1, :]).astype(BF16))
            cs = slice(g * D_GROUP + t * LANE, g * D_GROUP + (t + 1) * LANE)
            xpair = xs_bf[:, cs]
            zero = jnp.zeros_like(xpair)
            y_diag = (jnp.dot(ws[0], jnp.where(lo_half, xpair, zero), preferred_element_type=F32)
                      + jnp.dot(ws[1], jnp.where(lo_half, zero, xpair), preferred_element_type=F32))
            y_scr[:, cs] = (y_diag + y_off[:, t * LANE:(t + 1) * LANE]) + xs[:, cs] * dskip_ref[:, cs]

    for g in range(N_SSM_GROUPS):
        gs = slice(g * D_GROUP, (g + 1) * D_GROUP)
        yg = y_scr[:, gs] * _silu(z_ref[:, gs].astype(F32))
        y_ref[:, gs] = (yg * _rms_scale(yg, NORM_EPS) * nw_ref[:, gs]).astype(BF16)

    @pl.when(c == nc - 1)
    def _():
        for g in range(N_SSM_GROUPS):
            hnew_ref[0, g * HEADS_PER_GROUP:(g + 1) * HEADS_PER_GROUP] = (
                st_scr[g].T.reshape(HEADS_PER_GROUP, SSM_HEAD_DIM, D_STATE))


def _ssd(main, dt, conv_hist, h0, b_off, lp, bsz, length, q):
    nc = length // q
    m = bsz * length
    row = lambda b, c: b * nc + c
    return pl.pallas_call(
        functools.partial(_ssd_kernel, q=q),
        grid=(bsz, nc),
        in_specs=[
            pl.BlockSpec((q, D_SSM), lambda b, c: (row(b, c), COL_Z // D_SSM)),
            pl.BlockSpec((q, D_SSM), lambda b, c: (row(b, c), COL_X // D_SSM)),
            pl.BlockSpec((q, D_BC), lambda b, c: (row(b, c), COL_B // D_BC)),
            pl.BlockSpec((q, D_BC), lambda b, c: (row(b, c), COL_C // D_BC)),
            pl.BlockSpec((q, LANE), lambda b, c: (row(b, c), 0)),
            pl.BlockSpec((1, CONV_W - 1, D_CONV), lambda b, c: (b_off + b, 0, 0)),
            pl.BlockSpec((1, N_SSM_HEADS, SSM_HEAD_DIM, D_STATE), lambda b, c: (b_off + b, 0, 0, 0)),
            pl.BlockSpec((CONV_W, D_CONV), lambda b, c: (0, 0)),
            pl.BlockSpec((1, D_CONV), lambda b, c: (0, 0)),
            pl.BlockSpec((1, LANE), lambda b, c: (0, 0)),
            pl.BlockSpec((1, LANE), lambda b, c: (0, 0)),
            pl.BlockSpec((1, D_SSM), lambda b, c: (0, 0)),
            pl.BlockSpec((1, D_SSM), lambda b, c: (0, 0)),
            pl.BlockSpec((LANE, D_SSM), lambda b, c: (0, 0)),
        ],
        out_specs=[
            pl.BlockSpec((q, D_SSM), lambda b, c: (row(b, c), 0)),
            pl.BlockSpec((1, CONV_W - 1, D_CONV), lambda b, c: (b, 0, 0)),
            pl.BlockSpec((1, N_SSM_HEADS, SSM_HEAD_DIM, D_STATE), lambda b, c: (b, 0, 0, 0)),
        ],
        out_shape=[
            jax.ShapeDtypeStruct((m, D_MODEL), BF16),
            jax.ShapeDtypeStruct((bsz, CONV_W - 1, D_CONV), F32),
            jax.ShapeDtypeStruct((bsz, N_SSM_HEADS, SSM_HEAD_DIM, D_STATE), F32),
        ],
        scratch_shapes=[
            pltpu.VMEM((q + SUBLANE, D_CONV), F32),
            pltpu.VMEM((N_SSM_GROUPS, D_STATE, D_GROUP), F32),
            pltpu.VMEM((q, D_SSM), F32),
        ],
        compiler_params=_params(2),
        name="ssd",
    )(main, main, main, main, dt, conv_hist, h0,
      lp["conv_w"], lp["conv_b"], lp["dt_bias"], lp["a_log"], lp["d_skip_e"], lp["ssm_norm_w"],
      lp["expand"])


def _lambda(lq1, lk1, lq2, lk2, lambda_init):
    return (jnp.exp(jnp.sum(lq1[...] * lk1[...], axis=-1, keepdims=True))
            - jnp.exp(jnp.sum(lq2[...] * lk2[...], axis=-1, keepdims=True)) + lambda_init)


def _diff_finish(o1, o2, lam, sw_ref, g, lambda_init):
    o = o1 - lam * o2
    on = (o * _rms_scale(o, SUBLN_EPS) * sw_ref[...]) * (1.0 - lambda_init)
    return (on * _silu(g.astype(F32))).astype(BF16)


ATTN_QT = 256


def _attn_scores(k_ref, q_ref, mp, nt, diagonal):
    cols = slice(mp * DIFF_HEAD_DIM, (mp + 1) * DIFF_HEAD_DIM)
    qs = slice(nt * ATTN_QT, (nt + 1) * ATTN_QT)
    kw = (nt + 1) * ATTN_QT if diagonal else k_ref.shape[0]
    s = lax.dot_general(k_ref[0:kw, cols], q_ref[qs, cols], (((1,), (1,)), ((), ())),
                        preferred_element_type=F32)
    if diagonal:
        r = lax.broadcasted_iota(jnp.int32, (ATTN_QT, ATTN_QT), 0)
        q = lax.broadcasted_iota(jnp.int32, (ATTN_QT, ATTN_QT), 1)
        tail = jnp.where(_chunk_of(r) <= _chunk_of(q), s[kw - ATTN_QT:kw], -jnp.inf)
        s = tail if kw == ATTN_QT else jnp.concatenate([s[0:kw - ATTN_QT], tail], axis=0)
    return s


def _attn_softmax(s, m_scr, l_scr, mp, nt):
    c = (DIFF_HEAD_DIM ** -0.5) * math.log2(math.e)
    qs = slice(nt * ATTN_QT, (nt + 1) * ATTN_QT)
    m_prev = m_scr[mp, :, qs]
    m_new = jnp.maximum(m_prev, jnp.max(s, axis=0, keepdims=True))
    p = jnp.exp2((s - m_new) * c)
    alpha = jnp.exp2((m_prev - m_new) * c)
    l_scr[mp, :, qs] = alpha * l_scr[mp, :, qs] + jnp.sum(p, axis=0, keepdims=True)
    m_scr[mp, :, qs] = m_new
    return p.astype(BF16), alpha


def _attn_value(vt, p, alpha, acc_scr, mp, nt):
    qs = slice(nt * ATTN_QT, (nt + 1) * ATTN_QT)
    kw = p.shape[0]
    acc_scr[mp, :, qs] = alpha * acc_scr[mp, :, qs] + jnp.dot(vt[:, 0:kw], p, preferred_element_type=F32)


def _diff_attn_kernel(qi_ref, ki_ref, q_ref, k_ref, v_ref, g_ref, lq1, lk1, lq2, lk2, sw_ref, y_hbm,
                      o_ref, m_scr, l_scr, acc_scr, *, tq, tk, lambda_init):
    del y_hbm
    qi = qi_ref[pl.program_id(2)]
    ki = ki_ref[pl.program_id(2)]
    assert tq == tk and tq % ATTN_QT == 0 and ATTN_QT % CHUNK == 0

    @pl.when(ki == 0)
    def _():
        m_scr[...] = jnp.full(m_scr.shape, -jnp.inf, F32)
        l_scr[...] = jnp.zeros(l_scr.shape, F32)
        acc_scr[...] = jnp.zeros(acc_scr.shape, F32)

    def block(diagonal):
        vt = v_ref[...].T
        chains = [(mp, nt) for mp in range(2) for nt in range(tq // ATTN_QT)]
        scores = [_attn_scores(k_ref, q_ref, mp, nt, diagonal) for mp, nt in chains]
        probs = [_attn_softmax(s, m_scr, l_scr, mp, nt) for s, (mp, nt) in zip(scores, chains)]
        for (p, alpha), (mp, nt) in zip(probs, chains):
            _attn_value(vt, p, alpha, acc_scr, mp, nt)

    @pl.when(ki < qi)
    def _():
        block(False)

    @pl.when(ki == qi)
    def _():
        block(True)
        lam = _lambda(lq1, lk1, lq2, lk2, lambda_init)
        o1 = (acc_scr[0] / l_scr[0]).T
        o2 = (acc_scr[1] / l_scr[1]).T
        o_ref[...] = _diff_finish(o1, o2, lam, sw_ref, g_ref[...], lambda_init)


def _diff_attn_prompt(main, y, lp, bsz, length, lambda_init, tq):
    tk = tq
    nq = length // tq
    m = bsz * length
    pairs = [(qi, ki) for qi in range(nq) for ki in range(qi + 1)]
    qi_tab = jnp.asarray([p[0] for p in pairs], jnp.int32)
    ki_tab = jnp.asarray([p[1] for p in pairs], jnp.int32)
    vec = pl.BlockSpec((1, DIFF_HEAD_DIM), lambda b, h, t, qt, kt: (0, 0))
    return pl.pallas_call(
        functools.partial(_diff_attn_kernel, tq=tq, tk=tk, lambda_init=lambda_init),
        grid_spec=pltpu.PrefetchScalarGridSpec(
            num_scalar_prefetch=2,
            grid=(bsz, N_DIFF_HEADS, len(pairs)),
            in_specs=[
                pl.BlockSpec((tq, HEAD_W), lambda b, h, t, qt, kt: (b * nq + qt[t], COL_QD // HEAD_W + h)),
                pl.BlockSpec((tk, HEAD_W), lambda b, h, t, qt, kt: (b * nq + kt[t], COL_KD // HEAD_W + h)),
                pl.BlockSpec((tk, HEAD_W), lambda b, h, t, qt, kt: (b * nq + kt[t], COL_VD // HEAD_W + h)),
                pl.BlockSpec((tq, HEAD_W), lambda b, h, t, qt, kt: (b * nq + qt[t], COL_GD // HEAD_W + h)),
                vec, vec, vec, vec,
                pl.BlockSpec((1, HEAD_W), lambda b, h, t, qt, kt: (0, 0)),
                pl.BlockSpec(memory_space=pl.ANY),
            ],
            out_specs=pl.BlockSpec((tq, HEAD_W),
                                   lambda b, h, t, qt, kt: (b * nq + qt[t], D_SSM // HEAD_W + h)),
            scratch_shapes=[
                pltpu.VMEM((2, 1, tq), F32),
                pltpu.VMEM((2, 1, tq), F32),
                pltpu.VMEM((2, HEAD_W, tq), F32),
            ],
        ),
        out_shape=jax.ShapeDtypeStruct((m, D_MODEL), BF16),
        input_output_aliases={11: 0},
        compiler_params=_params(3),
        name="diff_attn_prompt",
    )(qi_tab, ki_tab, main, main, main, main, lp["lambda_q1"], lp["lambda_k1"], lp["lambda_q2"],
      lp["lambda_k2"], lp["subln_w"], y)


def _diff_attn_decode_kernel(q_ref, kn_ref, vn_ref, g_ref, kp_ref, vp_ref, lq1, lk1, lq2, lk2, sw_ref,
                             y_hbm, o_ref, *, past, lambda_init):
    del y_hbm
    scale = DIFF_HEAD_DIM ** -0.5
    lq = q_ref.shape[0]
    nt_dims = (((1,), (1,)), ((), ()))
    mask_p = (_chunk_of(lax.broadcasted_iota(jnp.int32, (lq, past), 1))
              <= _chunk_of(past + lax.broadcasted_iota(jnp.int32, (lq, past), 0)))
    mask_n = (_chunk_of(past + lax.broadcasted_iota(jnp.int32, (lq, lq), 1))
              <= _chunk_of(past + lax.broadcasted_iota(jnp.int32, (lq, lq), 0)))
    lam = _lambda(lq1, lk1, lq2, lk2, lambda_init)
    for h in range(N_DIFF_HEADS):
        hs = slice(h * HEAD_W, (h + 1) * HEAD_W)
        vp = _load_flat(vp_ref, past, h * HEAD_W, HEAD_W, halves_split=True).astype(BF16)
        vn = vn_ref[:, hs]
        outs = []
        for mp in range(2):
            c0 = h * HEAD_W + mp * DIFF_HEAD_DIM
            qm = q_ref[:, c0:c0 + DIFF_HEAD_DIM]
            kp = _load_flat(kp_ref, past, c0, DIFF_HEAD_DIM, halves_split=False).astype(BF16)
            sp = lax.dot_general(qm, kp, nt_dims, preferred_element_type=F32) * scale
            sn = lax.dot_general(qm, kn_ref[:, c0:c0 + DIFF_HEAD_DIM], nt_dims,
                                 preferred_element_type=F32) * scale
            sp = jnp.where(mask_p, sp, -jnp.inf)
            sn = jnp.where(mask_n, sn, -jnp.inf)
            mx = jnp.maximum(jnp.max(sp, axis=-1, keepdims=True), jnp.max(sn, axis=-1, keepdims=True))
            pp = jnp.exp(sp - mx)
            pn = jnp.exp(sn - mx)
            denom = jnp.sum(pp, axis=-1, keepdims=True) + jnp.sum(pn, axis=-1, keepdims=True)
            o = (jnp.dot(pp.astype(BF16), vp, preferred_element_type=F32)
                 + jnp.dot(pn.astype(BF16), vn, preferred_element_type=F32))
            outs.append(o / denom)
        o_ref[:, hs] = _diff_finish(outs[0], outs[1], lam, sw_ref, g_ref[:, hs], lambda_init)


def _diff_attn_decode(main, y, k_past, v_past, b_off, lp, bsz, length, past, lambda_init):
    m = bsz * length
    vec = pl.BlockSpec((1, DIFF_HEAD_DIM), lambda b: (0, 0))
    return pl.pallas_call(
        functools.partial(_diff_attn_decode_kernel, past=past, lambda_init=lambda_init),
        grid=(bsz,),
        in_specs=[
            pl.BlockSpec((length, D_DIFF), lambda b: (b, COL_QD // D_DIFF)),
            pl.BlockSpec((length, D_DIFF), lambda b: (b, COL_KD // D_DIFF)),
            pl.BlockSpec((length, D_DIFF), lambda b: (b, COL_VD // D_DIFF)),
            pl.BlockSpec((length, D_DIFF), lambda b: (b, COL_GD // D_DIFF)),
            pl.BlockSpec((past * ROW_TILES, LANE), lambda b: (b_off + b, 0)),
            pl.BlockSpec((past * ROW_TILES, LANE), lambda b: (b_off + b, 0)),
            vec, vec, vec, vec,
            pl.BlockSpec((1, HEAD_W), lambda b: (0, 0)),
            pl.BlockSpec(memory_space=pl.ANY),
        ],
        out_specs=pl.BlockSpec((length, D_DIFF), lambda b: (b, D_SSM // D_DIFF)),
        out_shape=jax.ShapeDtypeStruct((m, D_MODEL), BF16),
        input_output_aliases={11: 0},
        compiler_params=_params(1),
        name="diff_attn_decode",
    )(main, main, main, main, k_past, v_past,
      lp["lambda_q1"], lp["lambda_k1"], lp["lambda_q2"], lp["lambda_k2"], lp["subln_w"], y)


def _mem_attn_kernel(q_ref, g_ref, mk_ref, mv_ref, y_hbm, o_ref, *, n_mem):
    del y_hbm
    scale = MEM_HEAD_DIM ** -0.5
    for h in range(N_MEM_HEADS):
        hs = slice(h * MEM_HEAD_DIM, (h + 1) * MEM_HEAD_DIM)
        mk = _load_flat(mk_ref, n_mem, h * MEM_HEAD_DIM, MEM_HEAD_DIM, halves_split=True).astype(BF16)
        mv = _load_flat(mv_ref, n_mem, h * MEM_HEAD_DIM, MEM_HEAD_DIM, halves_split=True).astype(BF16)
        s = lax.dot_general(q_ref[:, hs], mk, (((1,), (1,)), ((), ())), preferred_element_type=F32) * scale
        p = jnp.exp(s - jnp.max(s, axis=-1, keepdims=True))
        pr = p / jnp.sum(p, axis=-1, keepdims=True)
        o = jnp.dot(pr.astype(BF16), mv, preferred_element_type=F32)
        o_ref[:, hs] = (o * _silu(g_ref[:, hs].astype(F32))).astype(BF16)


def _mem_attn(main, y, mk, mv, b_off, bsz, length, n_mem, tq):
    nq = length // tq
    m = bsz * length
    return pl.pallas_call(
        functools.partial(_mem_attn_kernel, n_mem=n_mem),
        grid=(bsz, nq),
        in_specs=[
            pl.BlockSpec((tq, D_MEM), lambda b, qi: (b * nq + qi, COL_QM // D_MEM)),
            pl.BlockSpec((tq, D_MEM), lambda b, qi: (b * nq + qi, COL_GM // D_MEM)),
            pl.BlockSpec((n_mem * ROW_TILES, LANE), lambda b, qi: (b_off + b, 0)),
            pl.BlockSpec((n_mem * ROW_TILES, LANE), lambda b, qi: (b_off + b, 0)),
            pl.BlockSpec(memory_space=pl.ANY),
        ],
        out_specs=pl.BlockSpec((tq, D_MEM), lambda b, qi: (b * nq + qi, (D_SSM + D_DIFF) // D_MEM)),
        out_shape=jax.ShapeDtypeStruct((m, D_MODEL), BF16),
        input_output_aliases={4: 0},
        compiler_params=_params(2),
        name="mem_attn",
    )(main, main, mk, mv, y)


def _out_proj_kernel(y_ref, w_ref, x_ref, nw_ref, o_ref, *, nj, tn):
    j = pl.program_id(1)
    acc = jnp.dot(y_ref[...], w_ref[...], preferred_element_type=F32)
    o_ref[:, pl.ds(pl.multiple_of(j * tn, tn), tn)] = acc

    @pl.when(j == nj - 1)
    def _():
        tm = o_ref.shape[0]
        step = min(NORM_ROWS, tm)
        for r in range(0, tm, step):
            o = o_ref[r:r + step, :]
            o_ref[r:r + step, :] = x_ref[r:r + step, :] + o * _rms_scale(o, NORM_EPS) * nw_ref[...]


def _out_proj(y, w_out, x2d, norm_w, tm, tn):
    m = x2d.shape[0]
    nj = D_MODEL // tn
    return pl.pallas_call(
        functools.partial(_out_proj_kernel, nj=nj, tn=tn),
        grid=(m // tm, nj),
        in_specs=[
            pl.BlockSpec((tm, D_MODEL), lambda i, j: (i, 0)),
            pl.BlockSpec((D_MODEL, tn), lambda i, j: (0, j)),
            pl.BlockSpec((tm, D_MODEL), lambda i, j: (jnp.maximum(i - (j == 0), 0), 0)),
            pl.BlockSpec((1, D_MODEL), lambda i, j: (0, 0)),
        ],
        out_specs=pl.BlockSpec((tm, D_MODEL), lambda i, j: (i, 0)),
        out_shape=jax.ShapeDtypeStruct((m, D_MODEL), F32),
        compiler_params=_params(2),
        name="out_proj",
    )(y, w_out, x2d, norm_w)


def _mixer_layer(x2d, bsz, length, layer, depth, kv_all, b_off, mem_off, conv_hist, h0, kv_past, past, mk,
                 mv, n_mem, lp, lambda_init, tiles):
    main, k32, v32, dt = _in_proj(x2d, lp["norm_pre_w"], lp["w_main"], lp["w_dt"], tiles["proj_tm"],
                                  layer, depth, kv_all)
    y, new_conv, h_new = _ssd(main, dt, conv_hist, h0, b_off, lp, bsz, length, tiles["ssd_q"])
    if kv_past is None:
        y = _diff_attn_prompt(main, y, lp, bsz, length, lambda_init, tiles["attn_tq"])
    else:
        y = _diff_attn_decode(main, y, kv_past[0], kv_past[1], b_off, lp, bsz, length, past, lambda_init)
    y = _mem_attn(main, y, mk, mv, mem_off, bsz, length, n_mem, tiles["mem_tq"])
    x_new = _out_proj(y, lp["w_out"], x2d, lp["norm_post_w"], tiles["out_tm"], tiles["out_tn"])
    return x_new, new_conv, h_new, (k32, v32)


def _layer_params(l, norm_pre_w, norm_post_w, w_in_t, conv_w, conv_b, dt_bias, a_log, d_skip,
                  ssm_norm_w, lambda_q1, lambda_k1, lambda_q2, lambda_k2, subln_w, mem_norm_w,
                  w_mem_kv, w_out):
    w_main, w_dt = _w_prep(w_in_t, l)
    pad_heads = lambda v: jnp.pad(v, (0, LANE - N_SSM_HEADS)).reshape(1, LANE)
    head_of_channel = jnp.arange(D_SSM) // SSM_HEAD_DIM
    expand = (jnp.arange(LANE)[:, None] == head_of_channel[None, :]).astype(BF16)
    return {
        "norm_pre_w": norm_pre_w[l].reshape(1, D_MODEL),
        "norm_post_w": norm_post_w[l].reshape(1, D_MODEL),
        "w_main": w_main,
        "w_dt": w_dt,
        "conv_w": conv_w[l],
        "conv_b": conv_b[l].reshape(1, D_CONV),
        "dt_bias": pad_heads(dt_bias[l]),
        "a_log": pad_heads(a_log[l]),
        "d_skip_e": jnp.repeat(d_skip[l], SSM_HEAD_DIM).reshape(1, D_SSM),
        "ssm_norm_w": ssm_norm_w[l].reshape(1, D_SSM),
        "expand": expand,
        "lambda_q1": lambda_q1[l].reshape(1, DIFF_HEAD_DIM),
        "lambda_k1": lambda_k1[l].reshape(1, DIFF_HEAD_DIM),
        "lambda_q2": lambda_q2[l].reshape(1, DIFF_HEAD_DIM),
        "lambda_k2": lambda_k2[l].reshape(1, DIFF_HEAD_DIM),
        "subln_w": subln_w[l].reshape(1, HEAD_W),
        "mem_norm_w": mem_norm_w[l].reshape(1, D_MODEL),
        "w_mem_kv": _layer_bf16(w_mem_kv, l),
        "w_out": _layer_bf16(w_out, l),
    }


def _tiles(bsz, length):
    m = bsz * length
    return {
        "proj_tm": min(512, m),
        "ssd_q": min(128, length),
        "attn_tq": min(1024, length),
        "mem_tq": min(1024, length),
        "out_tm": min(512, m),
        "out_tn": 512,
    }


def kernel(x_prompt, x_sample, mem_prompt, cache_conv, state_ssm, cache_k, cache_v, cache_mem_k,
           cache_mem_v, norm_pre_w, norm_post_w, w_in, conv_w, conv_b, dt_bias, a_log, d_skip,
           ssm_norm_w, lambda_q1, lambda_k1, lambda_q2, lambda_k2, subln_w, mem_norm_w, w_mem_kv,
           w_out):
    depth = w_in.shape[0]
    bp, lp_, _ = x_prompt.shape
    bs, ls, _ = x_sample.shape
    n_mem = mem_prompt.shape[1]
    past = cache_k.shape[2]
    xp = x_prompt.reshape(bp * lp_, D_MODEL)
    xs = x_sample.reshape(bs * ls, D_MODEL)
    mem2d = mem_prompt.reshape(bp * n_mem, D_MODEL)
    tiles_p = _tiles(bp, lp_)
    tiles_s = _tiles(bs, ls)
    conv0 = jnp.zeros((bp, CONV_W - 1, D_CONV), F32)
    h_zero = jnp.zeros((bp, N_SSM_HEADS, SSM_HEAD_DIM, D_STATE), F32)
    conv_all = cache_conv.reshape(depth * bs, CONV_W - 1, D_CONV)
    ssm_all = state_ssm.reshape(depth * bs, N_SSM_HEADS, SSM_HEAD_DIM, D_STATE)
    kv_all = (cache_k.reshape(-1, LANE), _to_flat_halves(cache_v))
    mem_k_all = _to_flat_halves(cache_mem_k)
    mem_v_all = _to_flat_halves(cache_mem_v)
    outs = [[] for _ in range(6)]
    kv_p = kv_s = None
    w_in_t = jnp.swapaxes(w_in, 1, 2)
    for l in range(depth):
        lp = _layer_params(l, norm_pre_w, norm_post_w, w_in_t, conv_w, conv_b, dt_bias, a_log, d_skip,
                           ssm_norm_w, lambda_q1, lambda_k1, lambda_q2, lambda_k2, subln_w,
                           mem_norm_w, w_mem_kv, w_out)
        lambda_init = 0.8 - 0.6 * math.exp(-0.3 * l)
        mk_p, mv_p = _mem_kv(mem2d, lp["mem_norm_w"], lp["w_mem_kv"], min(512, bp * n_mem))
        xp, c_p, h_p, kv_p = _mixer_layer(xp, bp, lp_, l, depth, kv_p, 0, 0, conv0, h_zero, None, 0, mk_p,
                                          mv_p, n_mem, lp, lambda_init, tiles_p)
        xs, c_s, h_s, kv_s = _mixer_layer(xs, bs, ls, l, depth, kv_s, l * bs, l * bs, conv_all, ssm_all,
                                          kv_all, past, mem_k_all, mem_v_all, n_mem, lp, lambda_init,
                                          tiles_s)
        for lst, val in zip(outs, (c_p, h_p, mk_p, mv_p, c_s, h_s)):
            lst.append(val)
    st = [jnp.stack(o) for o in outs]
    return (
        xp.reshape(bp, lp_, D_MODEL),
        xs.reshape(bs, ls, D_MODEL),
        st[0],
        st[1],
        kv_p[0].reshape(depth, bp, lp_, N_DIFF_HEADS, 2, DIFF_HEAD_DIM),
        _from_flat_halves(kv_p[1], (depth, bp, lp_, N_DIFF_HEADS, 2 * DIFF_HEAD_DIM)),
        _from_flat_halves(st[2], (depth, bp, n_mem, N_MEM_HEADS, MEM_HEAD_DIM)),
        _from_flat_halves(st[3], (depth, bp, n_mem, N_MEM_HEADS, MEM_HEAD_DIM)),
        st[4],
        st[5],
        kv_s[0].reshape(depth, bs, ls, N_DIFF_HEADS, 2, DIFF_HEAD_DIM),
        _from_flat_halves(kv_s[1], (depth, bs, ls, N_DIFF_HEADS, 2 * DIFF_HEAD_DIM)),
    )
```

```python
import functools
import math

import jax
import jax.numpy as jnp
from jax import lax
from jax.experimental import pallas as pl
from jax.experimental.pallas import tpu as pltpu

F32 = jnp.float32
BF16 = jnp.bfloat16

D_MODEL = 4096
D_SSM = 2048
SSM_HEAD_DIM = 64
N_SSM_HEADS = 32
N_SSM_GROUPS = 4
HEADS_PER_GROUP = N_SSM_HEADS // N_SSM_GROUPS
D_GROUP = D_SSM // N_SSM_GROUPS
D_STATE = 128
CONV_W = 4
D_BC = N_SSM_GROUPS * D_STATE
D_CONV = D_SSM + 2 * D_BC
D_DIFF = 1024
N_DIFF_HEADS = 4
DIFF_HEAD_DIM = 128
D_MEM = 1024
N_MEM_HEADS = 4
MEM_HEAD_DIM = 256
CHUNK = 64
NORM_EPS = 1e-6
SUBLN_EPS = 1e-5

LANE = 128
SUBLANE = 8
VMEM_LIMIT = 56 * 1024 * 1024

HEAD_W = 2 * DIFF_HEAD_DIM
COL_KD = 0
COL_VD = COL_KD + D_DIFF
COL_Z = COL_VD + D_DIFF
COL_X = COL_Z + D_SSM
COL_B = COL_X + D_SSM
COL_C = COL_B + D_BC
COL_QD = COL_C + D_BC
COL_GD = COL_QD + D_DIFF
COL_QM = COL_GD + D_DIFF
COL_GM = COL_QM + D_MEM
D_MAIN = COL_GM + D_MEM
PROJ_TN = 1024
K_TILE = COL_KD // PROJ_TN
V_TILE = COL_VD // PROJ_TN


def _params(n_axes):
    return pltpu.CompilerParams(dimension_semantics=("arbitrary",) * n_axes,
                                vmem_limit_bytes=VMEM_LIMIT)


def _silu(x):
    return x * jax.nn.sigmoid(x)


def _chunk_of(pos):
    assert CHUNK & (CHUNK - 1) == 0
    return lax.shift_right_logical(pos, jnp.int32(CHUNK.bit_length() - 1))


def _rms_scale(x, eps):
    return lax.rsqrt(jnp.mean(x * x, axis=-1, keepdims=True) + eps)


ROW_TILES = 1024 // LANE
N_HEADS_1024 = 1024 // HEAD_W


def _flat_row(c, halves_split):
    head, half = divmod(c, 2)
    return half * N_HEADS_1024 + head if halves_split else c


def _store_flat(ref, val, halves_split):
    rows = val.shape[0]
    for c in range(ROW_TILES):
        ref[pl.ds(_flat_row(c, halves_split), rows, stride=ROW_TILES), :] = val[:, c * LANE:(c + 1) * LANE]


def _load_flat(ref, rows, col0, width, halves_split):
    parts = [ref[pl.ds(_flat_row(col0 // LANE + c, halves_split), rows, stride=ROW_TILES), :]
             for c in range(width // LANE)]
    return parts[0] if len(parts) == 1 else jnp.concatenate(parts, axis=1)


def _to_flat_halves(a):
    return a.reshape(-1, N_HEADS_1024, 2, LANE).transpose(0, 2, 1, 3).reshape(-1, LANE)


def _from_flat_halves(flat, shape):
    return flat.reshape(-1, 2, N_HEADS_1024, LANE).transpose(0, 2, 1, 3).reshape(shape)


NORM_ROWS = 128


def _norm_rows(x_ref, nw_ref, h_scr):
    tm = x_ref.shape[0]
    step = min(NORM_ROWS, tm)
    for r in range(0, tm, step):
        x = x_ref[r:r + step, :]
        h_scr[r:r + step, :] = (x * _rms_scale(x, NORM_EPS) * nw_ref[...]).astype(BF16)


O_DT = D_SSM + D_CONV
O_Q = O_DT + N_SSM_HEADS
MAIN_PIECES = ((COL_Z, 0, O_DT), (COL_QD, O_Q, D_DIFF), (COL_KD, O_Q + D_DIFF, D_DIFF),
               (COL_VD, O_Q + 2 * D_DIFF, D_DIFF), (COL_GD, O_Q + 3 * D_DIFF, D_DIFF),
               (COL_QM, O_Q + 4 * D_DIFF, D_MEM), (COL_GM, O_Q + 4 * D_DIFF + D_MEM, D_MEM))
PREP_COLS = 256
PREP_PER_TILE = PROJ_TN // PREP_COLS


def _prep_source_columns():
    src = []
    for dst, first, width in sorted(MAIN_PIECES):
        assert dst == len(src) * PREP_COLS
        src += list(range(first, first + width, PREP_COLS))
    assert len(src) * PREP_COLS == D_MAIN
    return jnp.asarray(src, jnp.int32)


def _w_prep_kernel(src_ref, w_ref, wdt_ref, main_ref, dt_ref):
    del src_ref
    main_ref[...] = w_ref[0].T.astype(BF16)

    @pl.when(pl.program_id(0) == 0)
    def _():
        lane = lax.broadcasted_iota(jnp.int32, (1, LANE), 1)
        dt_ref[...] = jnp.where(lane < N_SSM_HEADS, wdt_ref[...].T, 0.0).astype(BF16)


def _w_prep(w_in_t, layer):
    return pl.pallas_call(
        _w_prep_kernel,
        grid_spec=pltpu.PrefetchScalarGridSpec(
            num_scalar_prefetch=1,
            grid=(D_MAIN // PREP_COLS,),
            in_specs=[
                pl.BlockSpec((pl.Element(1), pl.Element(PREP_COLS), pl.Element(D_MODEL)),
                             lambda j, src: (layer, pl.multiple_of(src[j], N_SSM_HEADS), 0)),
                pl.BlockSpec((None, LANE, D_MODEL), lambda j, src: (layer, O_DT // LANE, 0)),
            ],
            out_specs=[
                pl.BlockSpec((None, D_MODEL, PREP_COLS),
                             lambda j, src: (j // PREP_PER_TILE, 0, j % PREP_PER_TILE)),
                pl.BlockSpec((D_MODEL, LANE), lambda j, src: (0, 0)),
            ],
        ),
        out_shape=[
            jax.ShapeDtypeStruct((D_MAIN // PROJ_TN, D_MODEL, PROJ_TN), BF16),
            jax.ShapeDtypeStruct((D_MODEL, LANE), BF16),
        ],
        compiler_params=_params(1),
        name="w_prep",
    )(_prep_source_columns(), w_in_t, w_in_t)


CAST_ROWS = 1024


def _cast_kernel(w_ref, o_ref):
    o_ref[...] = w_ref[...].astype(BF16)


def _layer_bf16_tiles(w, layer, tn):
    _, rows, cols = w.shape
    return pl.pallas_call(
        _cast_kernel,
        grid=(rows // CAST_ROWS, cols // tn),
        in_specs=[pl.BlockSpec((None, CAST_ROWS, tn), lambda i, t: (layer, i, t))],
        out_specs=pl.BlockSpec((None, CAST_ROWS, tn), lambda i, t: (t, i, 0)),
        out_shape=jax.ShapeDtypeStruct((cols // tn, rows, tn), BF16),
        compiler_params=_params(2),
        name="cast_bf16",
    )(w)


def _in_proj_kernel(x_ref, nw_ref, w_ref, wdt_ref, *refs):
    main_ref, k_ref, v_ref, dt_ref, h_scr = refs[-5:]
    j = pl.program_id(1)

    @pl.when(j == 0)
    def _():
        _norm_rows(x_ref, nw_ref, h_scr)
        dt_ref[...] = jnp.dot(h_scr[...], wdt_ref[...], preferred_element_type=F32)

    acc = jnp.dot(h_scr[...], w_ref[...], preferred_element_type=F32)
    main_ref[...] = acc.astype(BF16)

    @pl.when(j == K_TILE)
    def _():
        _store_flat(k_ref, acc, halves_split=False)

    @pl.when(j == V_TILE)
    def _():
        _store_flat(v_ref, acc, halves_split=True)


def _in_proj(x2d, norm_w, w_main, w_dt, tm, layer, depth, kv_all):
    m = x2d.shape[0]
    ni = m // tm
    grid = (ni, D_MAIN // PROJ_TN)
    kv_spec = pl.BlockSpec((tm * ROW_TILES, LANE), lambda i, j: (layer * ni + i, 0))
    kv_shape = jax.ShapeDtypeStruct((depth * m * ROW_TILES, LANE), F32)
    in_specs = [
        pl.BlockSpec((tm, D_MODEL), lambda i, j: (i, 0)),
        pl.BlockSpec((1, D_MODEL), lambda i, j: (0, 0)),
        pl.BlockSpec((None, D_MODEL, PROJ_TN), lambda i, j: (j, 0, 0)),
        pl.BlockSpec((D_MODEL, LANE), lambda i, j: (0, 0)),
    ]
    operands = [x2d, norm_w, w_main, w_dt]
    aliases = {}
    if kv_all is not None:
        in_specs += [pl.BlockSpec(memory_space=pl.ANY)] * 2
        aliases = {len(operands): 1, len(operands) + 1: 2}
        operands += list(kv_all)
    return pl.pallas_call(
        _in_proj_kernel,
        grid=grid,
        in_specs=in_specs,
        out_specs=[
            pl.BlockSpec((tm, PROJ_TN), lambda i, j: (i, j)),
            kv_spec,
            kv_spec,
            pl.BlockSpec((tm, LANE), lambda i, j: (i, 0)),
        ],
        out_shape=[
            jax.ShapeDtypeStruct((m, D_MAIN), BF16),
            kv_shape,
            kv_shape,
            jax.ShapeDtypeStruct((m, LANE), F32),
        ],
        scratch_shapes=[pltpu.VMEM((tm, D_MODEL), BF16)],
        input_output_aliases=aliases,
        compiler_params=_params(2),
        name="in_proj",
    )(*operands)


def _mem_kv_kernel(x_ref, nw_ref, w_ref, mk_ref, mv_ref, h_scr):
    j = pl.program_id(1)

    @pl.when(j == 0)
    def _():
        _norm_rows(x_ref, nw_ref, h_scr)

    acc = jnp.dot(h_scr[...], w_ref[...], preferred_element_type=F32)

    @pl.when(j == 0)
    def _():
        _store_flat(mk_ref, acc, halves_split=True)

    @pl.when(j == 1)
    def _():
        _store_flat(mv_ref, acc, halves_split=True)


def _mem_kv(mem2d, norm_w, w_kv, tm):
    m = mem2d.shape[0]
    return pl.pallas_call(
        _mem_kv_kernel,
        grid=(m // tm, 2),
        in_specs=[
            pl.BlockSpec((tm, D_MODEL), lambda i, j: (i, 0)),
            pl.BlockSpec((1, D_MODEL), lambda i, j: (0, 0)),
            pl.BlockSpec((None, D_MODEL, D_MEM), lambda i, j: (j, 0, 0)),
        ],
        out_specs=[
            pl.BlockSpec((tm * ROW_TILES, LANE), lambda i, j: (i, 0)),
            pl.BlockSpec((tm * ROW_TILES, LANE), lambda i, j: (i, 0)),
        ],
        out_shape=[jax.ShapeDtypeStruct((m * ROW_TILES, LANE), F32)] * 2,
        scratch_shapes=[pltpu.VMEM((tm, D_MODEL), BF16)],
        compiler_params=_params(2),
        name="mem_kv",
    )(mem2d, norm_w, w_kv)


def _split_bf16(x, n):
    parts = []
    r = x
    for _ in range(n - 1):
        p = r.astype(BF16)
        parts.append(p)
        r = r - p.astype(F32)
    parts.append(r.astype(BF16))
    return parts


def _sum_rows(a, n, rows):
    out = a[(n - 1) * rows:n * rows]
    for i in range(n - 2, -1, -1):
        out = out + a[i * rows:(i + 1) * rows]
    return out


def _ssd_kernel(z_ref, xx_ref, xb_ref, xc_ref, dt_ref, hist_ref, h0_ref,
                cw_ref, cb_ref, dtb_ref, alog_ref, dskip_ref, nw_ref, e_ref,
                y_ref, nconv_ref, hnew_ref, xp_scr, st_scr, y_scr, *, q):
    c = pl.program_id(1)
    nc = pl.num_programs(1)
    hist_row = SUBLANE - (CONV_W - 1)

    @pl.when(c == 0)
    def _():
        xp_scr[hist_row:SUBLANE, :] = hist_ref[0]
        for g in range(N_SSM_GROUPS):
            hg = h0_ref[0, g * HEADS_PER_GROUP:(g + 1) * HEADS_PER_GROUP]
            st_scr[g] = hg.reshape(D_GROUP, D_STATE).T

    xp_scr[SUBLANE:SUBLANE + q, 0:D_SSM] = xx_ref[...].astype(F32)
    xp_scr[SUBLANE:SUBLANE + q, D_SSM:D_SSM + D_BC] = xb_ref[...].astype(F32)
    xp_scr[SUBLANE:SUBLANE + q, D_SSM + D_BC:D_CONV] = xc_ref[...].astype(F32)
    acc = cb_ref[...] + xp_scr[hist_row:hist_row + q, :] * cw_ref[0:1, :]
    for j in range(1, CONV_W):
        acc = acc + xp_scr[hist_row + j:hist_row + j + q, :] * cw_ref[j:j + 1, :]
    xbc = _silu(acc)
    xs = xbc[:, 0:D_SSM]
    xs_bf = xs.astype(BF16)
    bm = xbc[:, D_SSM:D_SSM + D_BC].astype(BF16)
    cm = xbc[:, D_SSM + D_BC:D_CONV].astype(BF16)

    @pl.when(c == nc - 1)
    def _():
        nconv_ref[0] = xp_scr[q + hist_row:q + SUBLANE, :]

    xp_scr[0:SUBLANE, :] = xp_scr[q:q + SUBLANE, :]

    dtv = jax.nn.softplus(dt_ref[...] + dtb_ref[...])
    a = -jnp.exp(alog_ref[...])
    da = dtv * a
    ri = lax.broadcasted_iota(jnp.int32, (q, q), 0)
    ci = lax.broadcasted_iota(jnp.int32, (q, q), 1)
    causal = ri >= ci
    tril = jnp.where(causal, 1.0, 0.0).astype(BF16)
    triu = jnp.where(ri <= ci, 1.0, 0.0).astype(BF16)
    eye = jnp.where(ri == ci, 1.0, 0.0).astype(BF16)
    da3 = jnp.concatenate(_split_bf16(da, 3), axis=1)
    dt3 = jnp.concatenate(_split_bf16(dtv, 3), axis=1)
    cum3 = jnp.dot(tril, da3, preferred_element_type=F32)
    cum = cum3[:, 2 * LANE:3 * LANE] + cum3[:, LANE:2 * LANE] + cum3[:, 0:LANE]
    tn_dims = (((0,), (0,)), ((), ()))
    cum_t = _sum_rows(lax.dot_general(da3, triu, tn_dims, preferred_element_type=F32), 3, LANE)
    dt_t = _sum_rows(lax.dot_general(dt3, eye, tn_dims, preferred_element_type=F32), 3, LANE)

    ecum = jnp.exp(cum)
    dd = dtv * jnp.exp(cum[q - 1:q, :] - cum)
    ex_in = jnp.concatenate(_split_bf16(ecum, 2) + _split_bf16(dd, 2), axis=0)
    ex = jnp.dot(ex_in, e_ref[...], preferred_element_type=F32)
    ecum_e = ex[q:2 * q] + ex[0:q]
    dd_e = ex[3 * q:4 * q] + ex[2 * q:3 * q]
    xdd = (xs * dd_e).astype(BF16)

    lane = lax.broadcasted_iota(jnp.int32, (q, LANE), 1)
    lo_half = lane < SSM_HEAD_DIM
    nt_dims = (((1,), (1,)), ((), ()))
    for g in range(N_SSM_GROUPS):
        bg = bm[:, g * D_STATE:(g + 1) * D_STATE]
        cg = cm[:, g * D_STATE:(g + 1) * D_STATE]
        gs = slice(g * D_GROUP, (g + 1) * D_GROUP)
        cbg = lax.dot_general(cg, bg, nt_dims, preferred_element_type=F32)
        st = st_scr[g]
        y_off = jnp.dot(cg, st.astype(BF16), preferred_element_type=F32) * ecum_e[:, gs]
        st_scr[g] = st * ecum_e[q - 1:q, gs] + lax.dot_general(
            bg, xdd[:, gs], tn_dims, preferred_element_type=F32)
        for t in range(HEADS_PER_GROUP // 2):
            ws = []
            for h in (g * HEADS_PER_GROUP + 2 * t, g * HEADS_PER_GROUP + 2 * t + 1):
                seg = cum[:, h:h + 1] - cum_t[h:h + 1, :]
                lmat = jnp.exp(jnp.where(causal, seg, -jnp.inf))
                ws.append((cbg * lmat * dt_t[h:h + 1, :]).astype(BF16))
            cs = slice(g * D_GROUP + t * LANE, g * D_GROUP + (t + 1) * LANE)
            xpair = xs_bf[:, cs]
            zero = jnp.zeros_like(xpair)
            y_diag = (jnp.dot(ws[0], jnp.where(lo_half, xpair, zero), preferred_element_type=F32)
                      + jnp.dot(ws[1], jnp.where(lo_half, zero, xpair), preferred_element_type=F32))
            y_scr[:, cs] = (y_diag + y_off[:, t * LANE:(t + 1) * LANE]) + xs[:, cs] * dskip_ref[:, cs]

    for g in range(N_SSM_GROUPS):
        gs = slice(g * D_GROUP, (g + 1) * D_GROUP)
        yg = y_scr[:, gs] * _silu(z_ref[:, gs].astype(F32))
        y_ref[:, gs] = (yg * _rms_scale(yg, NORM_EPS) * nw_ref[:, gs]).astype(BF16)

    @pl.when(c == nc - 1)
    def _():
        for g in range(N_SSM_GROUPS):
            hnew_ref[0, g * HEADS_PER_GROUP:(g + 1) * HEADS_PER_GROUP] = (
                st_scr[g].T.reshape(HEADS_PER_GROUP, SSM_HEAD_DIM, D_STATE))


def _ssd(main, dt, conv_hist, h0, b_off, lp, bsz, length, q):
    nc = length // q
    m = bsz * length
    row = lambda b, c: b * nc + c
    return pl.pallas_call(
        functools.partial(_ssd_kernel, q=q),
        grid=(bsz, nc),
        in_specs=[
            pl.BlockSpec((q, D_SSM), lambda b, c: (row(b, c), COL_Z // D_SSM)),
            pl.BlockSpec((q, D_SSM), lambda b, c: (row(b, c), COL_X // D_SSM)),
            pl.BlockSpec((q, D_BC), lambda b, c: (row(b, c), COL_B // D_BC)),
            pl.BlockSpec((q, D_BC), lambda b, c: (row(b, c), COL_C // D_BC)),
            pl.BlockSpec((q, LANE), lambda b, c: (row(b, c), 0)),
            pl.BlockSpec((1, CONV_W - 1, D_CONV), lambda b, c: (b_off + b, 0, 0)),
            pl.BlockSpec((1, N_SSM_HEADS, SSM_HEAD_DIM, D_STATE), lambda b, c: (b_off + b, 0, 0, 0)),
            pl.BlockSpec((CONV_W, D_CONV), lambda b, c: (0, 0)),
            pl.BlockSpec((1, D_CONV), lambda b, c: (0, 0)),
            pl.BlockSpec((1, LANE), lambda b, c: (0, 0)),
            pl.BlockSpec((1, LANE), lambda b, c: (0, 0)),
            pl.BlockSpec((1, D_SSM), lambda b, c: (0, 0)),
            pl.BlockSpec((1, D_SSM), lambda b, c: (0, 0)),
            pl.BlockSpec((LANE, D_SSM), lambda b, c: (0, 0)),
        ],
        out_specs=[
            pl.BlockSpec((q, D_SSM), lambda b, c: (row(b, c), 0)),
            pl.BlockSpec((1, CONV_W - 1, D_CONV), lambda b, c: (b, 0, 0)),
            pl.BlockSpec((1, N_SSM_HEADS, SSM_HEAD_DIM, D_STATE), lambda b, c: (b, 0, 0, 0)),
        ],
        out_shape=[
            jax.ShapeDtypeStruct((m, D_MODEL), BF16),
            jax.ShapeDtypeStruct((bsz, CONV_W - 1, D_CONV), F32),
            jax.ShapeDtypeStruct((bsz, N_SSM_HEADS, SSM_HEAD_DIM, D_STATE), F32),
        ],
        scratch_shapes=[
            pltpu.VMEM((q + SUBLANE, D_CONV), F32),
            pltpu.VMEM((N_SSM_GROUPS, D_STATE, D_GROUP), F32),
            pltpu.VMEM((q, D_SSM), F32),
        ],
        compiler_params=_params(2),
        name="ssd",
    )(main, main, main, main, dt, conv_hist, h0,
      lp["conv_w"], lp["conv_b"], lp["dt_bias"], lp["a_log"], lp["d_skip_e"], lp["ssm_norm_w"],
      lp["expand"])


def _lambda(lq1, lk1, lq2, lk2, lambda_init):
    return (jnp.exp(jnp.sum(lq1[...] * lk1[...], axis=-1, keepdims=True))
            - jnp.exp(jnp.sum(lq2[...] * lk2[...], axis=-1, keepdims=True)) + lambda_init)


def _diff_finish(o1, o2, lam, sw_ref, g, lambda_init):
    o = o1 - lam * o2
    on = (o * _rms_scale(o, SUBLN_EPS) * sw_ref[...]) * (1.0 - lambda_init)
    return (on * _silu(g.astype(F32))).astype(BF16)


ATTN_QT = 256


def _attn_scores(k_ref, q_ref, mp, nt, diagonal):
    cols = slice(mp * DIFF_HEAD_DIM, (mp + 1) * DIFF_HEAD_DIM)
    qs = slice(nt * ATTN_QT, (nt + 1) * ATTN_QT)
    kw = (nt + 1) * ATTN_QT if diagonal else k_ref.shape[0]
    s = lax.dot_general(k_ref[0:kw, cols], q_ref[qs, cols], (((1,), (1,)), ((), ())),
                        preferred_element_type=F32)
    if diagonal:
        r = lax.broadcasted_iota(jnp.int32, (ATTN_QT, ATTN_QT), 0)
        q = lax.broadcasted_iota(jnp.int32, (ATTN_QT, ATTN_QT), 1)
        tail = jnp.where(_chunk_of(r) <= _chunk_of(q), s[kw - ATTN_QT:kw], -jnp.inf)
        s = tail if kw == ATTN_QT else jnp.concatenate([s[0:kw - ATTN_QT], tail], axis=0)
    return s


def _attn_softmax(s, m_scr, l_scr, mp, nt):
    c = (DIFF_HEAD_DIM ** -0.5) * math.log2(math.e)
    qs = slice(nt * ATTN_QT, (nt + 1) * ATTN_QT)
    m_prev = m_scr[mp, :, qs]
    m_new = jnp.maximum(m_prev, jnp.max(s, axis=0, keepdims=True))
    p = jnp.exp2((s - m_new) * c)
    alpha = jnp.exp2((m_prev - m_new) * c)
    l_scr[mp, :, qs] = alpha * l_scr[mp, :, qs] + jnp.sum(p, axis=0, keepdims=True)
    m_scr[mp, :, qs] = m_new
    return p.astype(BF16), alpha


def _attn_value(vt, p, alpha, acc_scr, mp, nt):
    qs = slice(nt * ATTN_QT, (nt + 1) * ATTN_QT)
    kw = p.shape[0]
    acc_scr[mp, :, qs] = alpha * acc_scr[mp, :, qs] + jnp.dot(vt[:, 0:kw], p, preferred_element_type=F32)


def _diff_attn_kernel(qi_ref, ki_ref, q_ref, k_ref, v_ref, g_ref, lq1, lk1, lq2, lk2, sw_ref, y_hbm,
                      o_ref, m_scr, l_scr, acc_scr, *, tq, tk, lambda_init):
    del y_hbm
    qi = qi_ref[pl.program_id(2)]
    ki = ki_ref[pl.program_id(2)]
    assert tq == tk and tq % ATTN_QT == 0 and ATTN_QT % CHUNK == 0

    @pl.when(ki == 0)
    def _():
        m_scr[...] = jnp.full(m_scr.shape, -jnp.inf, F32)
        l_scr[...] = jnp.zeros(l_scr.shape, F32)
        acc_scr[...] = jnp.zeros(acc_scr.shape, F32)

    def block(diagonal):
        vt = v_ref[...].T
        chains = [(mp, nt) for mp in range(2) for nt in range(tq // ATTN_QT)]
        scores = [_attn_scores(k_ref, q_ref, mp, nt, diagonal) for mp, nt in chains]
        probs = [_attn_softmax(s, m_scr, l_scr, mp, nt) for s, (mp, nt) in zip(scores, chains)]
        for (p, alpha), (mp, nt) in zip(probs, chains):
            _attn_value(vt, p, alpha, acc_scr, mp, nt)

    @pl.when(ki < qi)
    def _():
        block(False)

    @pl.when(ki == qi)
    def _():
        block(True)
        lam = _lambda(lq1, lk1, lq2, lk2, lambda_init)
        o1 = (acc_scr[0] / l_scr[0]).T
        o2 = (acc_scr[1] / l_scr[1]).T
        o_ref[...] = _diff_finish(o1, o2, lam, sw_ref, g_ref[...], lambda_init)


def _diff_attn_prompt(main, y, lp, bsz, length, lambda_init, tq):
    tk = tq
    nq = length // tq
    m = bsz * length
    pairs = [(qi, ki) for qi in range(nq) for ki in range(qi + 1)]
    qi_tab = jnp.asarray([p[0] for p in pairs], jnp.int32)
    ki_tab = jnp.asarray([p[1] for p in pairs], jnp.int32)
    vec = pl.BlockSpec((1, DIFF_HEAD_DIM), lambda b, h, t, qt, kt: (0, 0))
    return pl.pallas_call(
        functools.partial(_diff_attn_kernel, tq=tq, tk=tk, lambda_init=lambda_init),
        grid_spec=pltpu.PrefetchScalarGridSpec(
            num_scalar_prefetch=2,
            grid=(bsz, N_DIFF_HEADS, len(pairs)),
            in_specs=[
                pl.BlockSpec((tq, HEAD_W), lambda b, h, t, qt, kt: (b * nq + qt[t], COL_QD // HEAD_W + h)),
                pl.BlockSpec((tk, HEAD_W), lambda b, h, t, qt, kt: (b * nq + kt[t], COL_KD // HEAD_W + h)),
                pl.BlockSpec((tk, HEAD_W), lambda b, h, t, qt, kt: (b * nq + kt[t], COL_VD // HEAD_W + h)),
                pl.BlockSpec((tq, HEAD_W), lambda b, h, t, qt, kt: (b * nq + qt[t], COL_GD // HEAD_W + h)),
                vec, vec, vec, vec,
                pl.BlockSpec((1, HEAD_W), lambda b, h, t, qt, kt: (0, 0)),
                pl.BlockSpec(memory_space=pl.ANY),
            ],
            out_specs=pl.BlockSpec((tq, HEAD_W),
                                   lambda b, h, t, qt, kt: (b * nq + qt[t], D_SSM // HEAD_W + h)),
            scratch_shapes=[
                pltpu.VMEM((2, 1, tq), F32),
                pltpu.VMEM((2, 1, tq), F32),
                pltpu.VMEM((2, HEAD_W, tq), F32),
            ],
        ),
        out_shape=jax.ShapeDtypeStruct((m, D_MODEL), BF16),
        input_output_aliases={11: 0},
        compiler_params=_params(3),
        name="diff_attn_prompt",
    )(qi_tab, ki_tab, main, main, main, main, lp["lambda_q1"], lp["lambda_k1"], lp["lambda_q2"],
      lp["lambda_k2"], lp["subln_w"], y)


def _diff_attn_decode_kernel(q_ref, kn_ref, vn_ref, g_ref, kp_ref, vp_ref, lq1, lk1, lq2, lk2, sw_ref,
                             y_hbm, o_ref, *, past, lambda_init):
    del y_hbm
    scale = DIFF_HEAD_DIM ** -0.5
    lq = q_ref.shape[0]
    nt_dims = (((1,), (1,)), ((), ()))
    mask_p = (_chunk_of(lax.broadcasted_iota(jnp.int32, (lq, past), 1))
              <= _chunk_of(past + lax.broadcasted_iota(jnp.int32, (lq, past), 0)))
    mask_n = (_chunk_of(past + lax.broadcasted_iota(jnp.int32, (lq, lq), 1))
              <= _chunk_of(past + lax.broadcasted_iota(jnp.int32, (lq, lq), 0)))
    lam = _lambda(lq1, lk1, lq2, lk2, lambda_init)
    for h in range(N_DIFF_HEADS):
        hs = slice(h * HEAD_W, (h + 1) * HEAD_W)
        vp = _load_flat(vp_ref, past, h * HEAD_W, HEAD_W, halves_split=True).astype(BF16)
        vn = vn_ref[:, hs]
        outs = []
        for mp in range(2):
            c0 = h * HEAD_W + mp * DIFF_HEAD_DIM
            qm = q_ref[:, c0:c0 + DIFF_HEAD_DIM]
            kp = _load_flat(kp_ref, past, c0, DIFF_HEAD_DIM, halves_split=False).astype(BF16)
            sp = lax.dot_general(qm, kp, nt_dims, preferred_element_type=F32) * scale
            sn = lax.dot_general(qm, kn_ref[:, c0:c0 + DIFF_HEAD_DIM], nt_dims,
                                 preferred_element_type=F32) * scale
            sp = jnp.where(mask_p, sp, -jnp.inf)
            sn = jnp.where(mask_n, sn, -jnp.inf)
            mx = jnp.maximum(jnp.max(sp, axis=-1, keepdims=True), jnp.max(sn, axis=-1, keepdims=True))
            pp = jnp.exp(sp - mx)
            pn = jnp.exp(sn - mx)
            denom = jnp.sum(pp, axis=-1, keepdims=True) + jnp.sum(pn, axis=-1, keepdims=True)
            o = (jnp.dot(pp.astype(BF16), vp, preferred_element_type=F32)
                 + jnp.dot(pn.astype(BF16), vn, preferred_element_type=F32))
            outs.append(o / denom)
        o_ref[:, hs] = _diff_finish(outs[0], outs[1], lam, sw_ref, g_ref[:, hs], lambda_init)


def _diff_attn_decode(main, y, k_past, v_past, b_off, lp, bsz, length, past, lambda_init):
    m = bsz * length
    vec = pl.BlockSpec((1, DIFF_HEAD_DIM), lambda b: (0, 0))
    return pl.pallas_call(
        functools.partial(_diff_attn_decode_kernel, past=past, lambda_init=lambda_init),
        grid=(bsz,),
        in_specs=[
            pl.BlockSpec((length, D_DIFF), lambda b: (b, COL_QD // D_DIFF)),
            pl.BlockSpec((length, D_DIFF), lambda b: (b, COL_KD // D_DIFF)),
            pl.BlockSpec((length, D_DIFF), lambda b: (b, COL_VD // D_DIFF)),
            pl.BlockSpec((length, D_DIFF), lambda b: (b, COL_GD // D_DIFF)),
            pl.BlockSpec((past * ROW_TILES, LANE), lambda b: (b_off + b, 0)),
            pl.BlockSpec((past * ROW_TILES, LANE), lambda b: (b_off + b, 0)),
            vec, vec, vec, vec,
            pl.BlockSpec((1, HEAD_W), lambda b: (0, 0)),
            pl.BlockSpec(memory_space=pl.ANY),
        ],
        out_specs=pl.BlockSpec((length, D_DIFF), lambda b: (b, D_SSM // D_DIFF)),
        out_shape=jax.ShapeDtypeStruct((m, D_MODEL), BF16),
        input_output_aliases={11: 0},
        compiler_params=_params(1),
        name="diff_attn_decode",
    )(main, main, main, main, k_past, v_past,
      lp["lambda_q1"], lp["lambda_k1"], lp["lambda_q2"], lp["lambda_k2"], lp["subln_w"], y)


def _mem_attn_kernel(q_ref, g_ref, mk_ref, mv_ref, y_hbm, o_ref, *, n_mem):
    del y_hbm
    scale = MEM_HEAD_DIM ** -0.5
    for h in range(N_MEM_HEADS):
        hs = slice(h * MEM_HEAD_DIM, (h + 1) * MEM_HEAD_DIM)
        mk = _load_flat(mk_ref, n_mem, h * MEM_HEAD_DIM, MEM_HEAD_DIM, halves_split=True).astype(BF16)
        mv = _load_flat(mv_ref, n_mem, h * MEM_HEAD_DIM, MEM_HEAD_DIM, halves_split=True).astype(BF16)
        s = lax.dot_general(q_ref[:, hs], mk, (((1,), (1,)), ((), ())), preferred_element_type=F32) * scale
        p = jnp.exp(s - jnp.max(s, axis=-1, keepdims=True))
        pr = p / jnp.sum(p, axis=-1, keepdims=True)
        o = jnp.dot(pr.astype(BF16), mv, preferred_element_type=F32)
        o_ref[:, hs] = (o * _silu(g_ref[:, hs].astype(F32))).astype(BF16)


def _mem_attn(main, y, mk, mv, b_off, bsz, length, n_mem, tq):
    nq = length // tq
    m = bsz * length
    return pl.pallas_call(
        functools.partial(_mem_attn_kernel, n_mem=n_mem),
        grid=(bsz, nq),
        in_specs=[
            pl.BlockSpec((tq, D_MEM), lambda b, qi: (b * nq + qi, COL_QM // D_MEM)),
            pl.BlockSpec((tq, D_MEM), lambda b, qi: (b * nq + qi, COL_GM // D_MEM)),
            pl.BlockSpec((n_mem * ROW_TILES, LANE), lambda b, qi: (b_off + b, 0)),
            pl.BlockSpec((n_mem * ROW_TILES, LANE), lambda b, qi: (b_off + b, 0)),
            pl.BlockSpec(memory_space=pl.ANY),
        ],
        out_specs=pl.BlockSpec((tq, D_MEM), lambda b, qi: (b * nq + qi, (D_SSM + D_DIFF) // D_MEM)),
        out_shape=jax.ShapeDtypeStruct((m, D_MODEL), BF16),
        input_output_aliases={4: 0},
        compiler_params=_params(2),
        name="mem_attn",
    )(main, main, mk, mv, y)


def _out_proj_kernel(y_ref, w_ref, x_ref, nw_ref, o_ref, *, nj, tn):
    j = pl.program_id(1)
    acc = jnp.dot(y_ref[...], w_ref[...], preferred_element_type=F32)
    o_ref[:, pl.ds(pl.multiple_of(j * tn, tn), tn)] = acc

    @pl.when(j == nj - 1)
    def _():
        tm = o_ref.shape[0]
        step = min(NORM_ROWS, tm)
        for r in range(0, tm, step):
            o = o_ref[r:r + step, :]
            o_ref[r:r + step, :] = x_ref[r:r + step, :] + o * _rms_scale(o, NORM_EPS) * nw_ref[...]


def _out_proj(y, w_out, x2d, norm_w, tm, tn):
    m = x2d.shape[0]
    nj = D_MODEL // tn
    return pl.pallas_call(
        functools.partial(_out_proj_kernel, nj=nj, tn=tn),
        grid=(m // tm, nj),
        in_specs=[
            pl.BlockSpec((tm, D_MODEL), lambda i, j: (i, 0)),
            pl.BlockSpec((None, D_MODEL, tn), lambda i, j: (j, 0, 0)),
            pl.BlockSpec((tm, D_MODEL), lambda i, j: (i, 0)),
            pl.BlockSpec((1, D_MODEL), lambda i, j: (0, 0)),
        ],
        out_specs=pl.BlockSpec((tm, D_MODEL), lambda i, j: (i, 0)),
        out_shape=jax.ShapeDtypeStruct((m, D_MODEL), F32),
        compiler_params=_params(2),
        name="out_proj",
    )(y, w_out, x2d, norm_w)


def _mixer_layer(x2d, bsz, length, layer, depth, kv_all, b_off, mem_off, conv_hist, h0, kv_past, past, mk,
                 mv, n_mem, lp, lambda_init, tiles):
    main, k32, v32, dt = _in_proj(x2d, lp["norm_pre_w"], lp["w_main"], lp["w_dt"], tiles["proj_tm"],
                                  layer, depth, kv_all)
    y, new_conv, h_new = _ssd(main, dt, conv_hist, h0, b_off, lp, bsz, length, tiles["ssd_q"])
    if kv_past is None:
        y = _diff_attn_prompt(main, y, lp, bsz, length, lambda_init, tiles["attn_tq"])
    else:
        y = _diff_attn_decode(main, y, kv_past[0], kv_past[1], b_off, lp, bsz, length, past, lambda_init)
    y = _mem_attn(main, y, mk, mv, mem_off, bsz, length, n_mem, tiles["mem_tq"])
    x_new = _out_proj(y, lp["w_out"], x2d, lp["norm_post_w"], tiles["out_tm"], tiles["out_tn"])
    return x_new, new_conv, h_new, (k32, v32)


def _layer_params(l, norm_pre_w, norm_post_w, w_in_t, conv_w, conv_b, dt_bias, a_log, d_skip,
                  ssm_norm_w, lambda_q1, lambda_k1, lambda_q2, lambda_k2, subln_w, mem_norm_w,
                  w_mem_kv, w_out):
    w_main, w_dt = _w_prep(w_in_t, l)
    pad_heads = lambda v: jnp.pad(v, (0, LANE - N_SSM_HEADS)).reshape(1, LANE)
    head_of_channel = jnp.arange(D_SSM) // SSM_HEAD_DIM
    expand = (jnp.arange(LANE)[:, None] == head_of_channel[None, :]).astype(BF16)
    return {
        "norm_pre_w": norm_pre_w[l].reshape(1, D_MODEL),
        "norm_post_w": norm_post_w[l].reshape(1, D_MODEL),
        "w_main": w_main,
        "w_dt": w_dt,
        "conv_w": conv_w[l],
        "conv_b": conv_b[l].reshape(1, D_CONV),
        "dt_bias": pad_heads(dt_bias[l]),
        "a_log": pad_heads(a_log[l]),
        "d_skip_e": jnp.repeat(d_skip[l], SSM_HEAD_DIM).reshape(1, D_SSM),
        "ssm_norm_w": ssm_norm_w[l].reshape(1, D_SSM),
        "expand": expand,
        "lambda_q1": lambda_q1[l].reshape(1, DIFF_HEAD_DIM),
        "lambda_k1": lambda_k1[l].reshape(1, DIFF_HEAD_DIM),
        "lambda_q2": lambda_q2[l].reshape(1, DIFF_HEAD_DIM),
        "lambda_k2": lambda_k2[l].reshape(1, DIFF_HEAD_DIM),
        "subln_w": subln_w[l].reshape(1, HEAD_W),
        "mem_norm_w": mem_norm_w[l].reshape(1, D_MODEL),
        "w_mem_kv": _layer_bf16_tiles(w_mem_kv, l, D_MEM),
        "w_out": _layer_bf16_tiles(w_out, l, OUT_TN),
    }


OUT_TN = 512


def _tiles(bsz, length):
    m = bsz * length
    return {
        "proj_tm": min(512, m),
        "ssd_q": min(128, length),
        "attn_tq": min(1024, length),
        "mem_tq": min(1024, length),
        "out_tm": min(512, m),
        "out_tn": OUT_TN,
    }


def kernel(x_prompt, x_sample, mem_prompt, cache_conv, state_ssm, cache_k, cache_v, cache_mem_k,
           cache_mem_v, norm_pre_w, norm_post_w, w_in, conv_w, conv_b, dt_bias, a_log, d_skip,
           ssm_norm_w, lambda_q1, lambda_k1, lambda_q2, lambda_k2, subln_w, mem_norm_w, w_mem_kv,
           w_out):
    depth = w_in.shape[0]
    bp, lp_, _ = x_prompt.shape
    bs, ls, _ = x_sample.shape
    n_mem = mem_prompt.shape[1]
    past = cache_k.shape[2]
    xp = x_prompt.reshape(bp * lp_, D_MODEL)
    xs = x_sample.reshape(bs * ls, D_MODEL)
    mem2d = mem_prompt.reshape(bp * n_mem, D_MODEL)
    tiles_p = _tiles(bp, lp_)
    tiles_s = _tiles(bs, ls)
    conv0 = jnp.zeros((bp, CONV_W - 1, D_CONV), F32)
    h_zero = jnp.zeros((bp, N_SSM_HEADS, SSM_HEAD_DIM, D_STATE), F32)
    conv_all = cache_conv.reshape(depth * bs, CONV_W - 1, D_CONV)
    ssm_all = state_ssm.reshape(depth * bs, N_SSM_HEADS, SSM_HEAD_DIM, D_STATE)
    kv_all = (cache_k.reshape(-1, LANE), _to_flat_halves(cache_v))
    mem_k_all = _to_flat_halves(cache_mem_k)
    mem_v_all = _to_flat_halves(cache_mem_v)
    outs = [[] for _ in range(6)]
    kv_p = kv_s = None
    w_in_t = jnp.swapaxes(w_in, 1, 2)
    for l in range(depth):
        lp = _layer_params(l, norm_pre_w, norm_post_w, w_in_t, conv_w, conv_b, dt_bias, a_log, d_skip,
                           ssm_norm_w, lambda_q1, lambda_k1, lambda_q2, lambda_k2, subln_w,
                           mem_norm_w, w_mem_kv, w_out)
        lambda_init = 0.8 - 0.6 * math.exp(-0.3 * l)
        mk_p, mv_p = _mem_kv(mem2d, lp["mem_norm_w"], lp["w_mem_kv"], min(512, bp * n_mem))
        xp, c_p, h_p, kv_p = _mixer_layer(xp, bp, lp_, l, depth, kv_p, 0, 0, conv0, h_zero, None, 0, mk_p,
                                          mv_p, n_mem, lp, lambda_init, tiles_p)
        xs, c_s, h_s, kv_s = _mixer_layer(xs, bs, ls, l, depth, kv_s, l * bs, l * bs, conv_all, ssm_all,
                                          kv_all, past, mem_k_all, mem_v_all, n_mem, lp, lambda_init,
                                          tiles_s)
        for lst, val in zip(outs, (c_p, h_p, mk_p, mv_p, c_s, h_s)):
            lst.append(val)
    st = [jnp.stack(o) for o in outs]
    return (
        xp.reshape(bp, lp_, D_MODEL),
        xs.reshape(bs, ls, D_MODEL),
        st[0],
        st[1],
        kv_p[0].reshape(depth, bp, lp_, N_DIFF_HEADS, 2, DIFF_HEAD_DIM),
        _from_flat_halves(kv_p[1], (depth, bp, lp_, N_DIFF_HEADS, 2 * DIFF_HEAD_DIM)),
        _from_flat_halves(st[2], (depth, bp, n_mem, N_MEM_HEADS, MEM_HEAD_DIM)),
        _from_flat_halves(st[3], (depth, bp, n_mem, N_MEM_HEADS, MEM_HEAD_DIM)),
        st[4],
        st[5],
        kv_s[0].reshape(depth, bs, ls, N_DIFF_HEADS, 2, DIFF_HEAD_DIM),
        _from_flat_halves(kv_s[1], (depth, bs, ls, N_DIFF_HEADS, 2 * DIFF_HEAD_DIM)),
    )
```

```python
import functools
import math

import jax
import jax.numpy as jnp
from jax import lax
from jax.experimental import pallas as pl
from jax.experimental.pallas import tpu as pltpu

F32 = jnp.float32
BF16 = jnp.bfloat16

D_MODEL = 4096
D_SSM = 2048
SSM_HEAD_DIM = 64
N_SSM_HEADS = 32
N_SSM_GROUPS = 4
HEADS_PER_GROUP = N_SSM_HEADS // N_SSM_GROUPS
D_GROUP = D_SSM // N_SSM_GROUPS
D_STATE = 128
CONV_W = 4
D_BC = N_SSM_GROUPS * D_STATE
D_CONV = D_SSM + 2 * D_BC
D_DIFF = 1024
N_DIFF_HEADS = 4
DIFF_HEAD_DIM = 128
D_MEM = 1024
N_MEM_HEADS = 4
MEM_HEAD_DIM = 256
CHUNK = 64
NORM_EPS = 1e-6
SUBLN_EPS = 1e-5

LANE = 128
SUBLANE = 8
VMEM_LIMIT = 56 * 1024 * 1024
OUT_PROJ_VMEM_LIMIT = 60 * 1024 * 1024

HEAD_W = 2 * DIFF_HEAD_DIM
COL_KD = 0
COL_VD = COL_KD + D_DIFF
COL_Z = COL_VD + D_DIFF
COL_X = COL_Z + D_SSM
COL_B = COL_X + D_SSM
COL_C = COL_B + D_BC
COL_QD = COL_C + D_BC
COL_GD = COL_QD + D_DIFF
COL_QM = COL_GD + D_DIFF
COL_GM = COL_QM + D_MEM
D_MAIN = COL_GM + D_MEM
PROJ_TN = 1024
K_TILE = COL_KD // PROJ_TN
V_TILE = COL_VD // PROJ_TN


def _params(n_axes, vmem_limit=VMEM_LIMIT):
    return pltpu.CompilerParams(dimension_semantics=("arbitrary",) * n_axes,
                                vmem_limit_bytes=vmem_limit)


def _silu(x):
    return x * jax.nn.sigmoid(x)


def _chunk_of(pos):
    assert CHUNK & (CHUNK - 1) == 0
    return lax.shift_right_logical(pos, jnp.int32(CHUNK.bit_length() - 1))


def _rms_scale(x, eps):
    return lax.rsqrt(jnp.mean(x * x, axis=-1, keepdims=True) + eps)


ROW_TILES = 1024 // LANE
N_HEADS_1024 = 1024 // HEAD_W


def _flat_row(c, halves_split):
    head, half = divmod(c, 2)
    return half * N_HEADS_1024 + head if halves_split else c


def _store_flat(ref, val, halves_split):
    rows = val.shape[0]
    for c in range(ROW_TILES):
        ref[pl.ds(_flat_row(c, halves_split), rows, stride=ROW_TILES), :] = val[:, c * LANE:(c + 1) * LANE]


def _load_flat(ref, rows, col0, width, halves_split):
    parts = [ref[pl.ds(_flat_row(col0 // LANE + c, halves_split), rows, stride=ROW_TILES), :]
             for c in range(width // LANE)]
    return parts[0] if len(parts) == 1 else jnp.concatenate(parts, axis=1)


def _to_flat_halves(a):
    return a.reshape(-1, N_HEADS_1024, 2, LANE).transpose(0, 2, 1, 3).reshape(-1, LANE)


def _from_flat_halves(flat, shape):
    return flat.reshape(-1, 2, N_HEADS_1024, LANE).transpose(0, 2, 1, 3).reshape(shape)


NORM_ROWS = 128


def _norm_rows(x_ref, nw_ref, h_scr):
    tm = x_ref.shape[0]
    step = min(NORM_ROWS, tm)
    for r in range(0, tm, step):
        x = x_ref[r:r + step, :]
        h_scr[r:r + step, :] = (x * _rms_scale(x, NORM_EPS) * nw_ref[...]).astype(BF16)


O_DT = D_SSM + D_CONV
O_Q = O_DT + N_SSM_HEADS
MAIN_PIECES = ((COL_Z, 0, O_DT), (COL_QD, O_Q, D_DIFF), (COL_KD, O_Q + D_DIFF, D_DIFF),
               (COL_VD, O_Q + 2 * D_DIFF, D_DIFF), (COL_GD, O_Q + 3 * D_DIFF, D_DIFF),
               (COL_QM, O_Q + 4 * D_DIFF, D_MEM), (COL_GM, O_Q + 4 * D_DIFF + D_MEM, D_MEM))
PREP_COLS = 256
PREP_PER_TILE = PROJ_TN // PREP_COLS


def _prep_source_columns():
    src = []
    for dst, first, width in sorted(MAIN_PIECES):
        assert dst == len(src) * PREP_COLS
        src += list(range(first, first + width, PREP_COLS))
    assert len(src) * PREP_COLS == D_MAIN
    return jnp.asarray(src, jnp.int32)


def _w_prep_kernel(src_ref, w_ref, wdt_ref, main_ref, dt_ref):
    del src_ref
    main_ref[...] = w_ref[0].T.astype(BF16)

    @pl.when(pl.program_id(0) == 0)
    def _():
        lane = lax.broadcasted_iota(jnp.int32, (1, LANE), 1)
        dt_ref[...] = jnp.where(lane < N_SSM_HEADS, wdt_ref[...].T, 0.0).astype(BF16)


def _w_prep(w_in_t, layer):
    return pl.pallas_call(
        _w_prep_kernel,
        grid_spec=pltpu.PrefetchScalarGridSpec(
            num_scalar_prefetch=1,
            grid=(D_MAIN // PREP_COLS,),
            in_specs=[
                pl.BlockSpec((pl.Element(1), pl.Element(PREP_COLS), pl.Element(D_MODEL)),
                             lambda j, src: (layer, pl.multiple_of(src[j], N_SSM_HEADS), 0)),
                pl.BlockSpec((None, LANE, D_MODEL), lambda j, src: (layer, O_DT // LANE, 0)),
            ],
            out_specs=[
                pl.BlockSpec((None, D_MODEL, PREP_COLS),
                             lambda j, src: (j // PREP_PER_TILE, 0, j % PREP_PER_TILE)),
                pl.BlockSpec((D_MODEL, LANE), lambda j, src: (0, 0)),
            ],
        ),
        out_shape=[
            jax.ShapeDtypeStruct((D_MAIN // PROJ_TN, D_MODEL, PROJ_TN), BF16),
            jax.ShapeDtypeStruct((D_MODEL, LANE), BF16),
        ],
        compiler_params=_params(1),
        name="w_prep",
    )(_prep_source_columns(), w_in_t, w_in_t)


CAST_ROWS = 1024


def _cast_kernel(w_ref, o_ref):
    o_ref[...] = w_ref[...].astype(BF16)


def _layer_bf16_tiles(w, layer, tn):
    _, rows, cols = w.shape
    return pl.pallas_call(
        _cast_kernel,
        grid=(rows // CAST_ROWS, cols // tn),
        in_specs=[pl.BlockSpec((None, CAST_ROWS, tn), lambda i, t: (layer, i, t))],
        out_specs=pl.BlockSpec((None, CAST_ROWS, tn), lambda i, t: (t, i, 0)),
        out_shape=jax.ShapeDtypeStruct((cols // tn, rows, tn), BF16),
        compiler_params=_params(2),
        name="cast_bf16",
    )(w)


def _in_proj_kernel(x_ref, nw_ref, w_ref, wdt_ref, *refs):
    main_ref, k_ref, v_ref, dt_ref, h_scr = refs[-5:]
    j = pl.program_id(1)

    @pl.when(j == 0)
    def _():
        _norm_rows(x_ref, nw_ref, h_scr)
        dt_ref[...] = jnp.dot(h_scr[...], wdt_ref[...], preferred_element_type=F32)

    acc = jnp.dot(h_scr[...], w_ref[...], preferred_element_type=F32)
    main_ref[...] = acc.astype(BF16)

    @pl.when(j == K_TILE)
    def _():
        _store_flat(k_ref, acc, halves_split=False)

    @pl.when(j == V_TILE)
    def _():
        _store_flat(v_ref, acc, halves_split=True)


def _in_proj(x2d, norm_w, w_main, w_dt, tm, layer, depth, kv_all):
    m = x2d.shape[0]
    ni = m // tm
    grid = (ni, D_MAIN // PROJ_TN)
    kv_spec = pl.BlockSpec((tm * ROW_TILES, LANE), lambda i, j: (layer * ni + i, 0))
    kv_shape = jax.ShapeDtypeStruct((depth * m * ROW_TILES, LANE), F32)
    in_specs = [
        pl.BlockSpec((tm, D_MODEL), lambda i, j: (i, 0)),
        pl.BlockSpec((1, D_MODEL), lambda i, j: (0, 0)),
        pl.BlockSpec((None, D_MODEL, PROJ_TN), lambda i, j: (j, 0, 0)),
        pl.BlockSpec((D_MODEL, LANE), lambda i, j: (0, 0)),
    ]
    operands = [x2d, norm_w, w_main, w_dt]
    aliases = {}
    if kv_all is not None:
        in_specs += [pl.BlockSpec(memory_space=pl.ANY)] * 2
        aliases = {len(operands): 1, len(operands) + 1: 2}
        operands += list(kv_all)
    return pl.pallas_call(
        _in_proj_kernel,
        grid=grid,
        in_specs=in_specs,
        out_specs=[
            pl.BlockSpec((tm, PROJ_TN), lambda i, j: (i, j)),
            kv_spec,
            kv_spec,
            pl.BlockSpec((tm, LANE), lambda i, j: (i, 0)),
        ],
        out_shape=[
            jax.ShapeDtypeStruct((m, D_MAIN), BF16),
            kv_shape,
            kv_shape,
            jax.ShapeDtypeStruct((m, LANE), F32),
        ],
        scratch_shapes=[pltpu.VMEM((tm, D_MODEL), BF16)],
        input_output_aliases=aliases,
        compiler_params=_params(2),
        name="in_proj",
    )(*operands)


def _mem_kv_kernel(x_ref, nw_ref, w_ref, mk_ref, mv_ref, h_scr):
    j = pl.program_id(1)

    @pl.when(j == 0)
    def _():
        _norm_rows(x_ref, nw_ref, h_scr)

    acc = jnp.dot(h_scr[...], w_ref[...], preferred_element_type=F32)

    @pl.when(j == 0)
    def _():
        _store_flat(mk_ref, acc, halves_split=True)

    @pl.when(j == 1)
    def _():
        _store_flat(mv_ref, acc, halves_split=True)


def _mem_kv(mem2d, norm_w, w_kv, tm):
    m = mem2d.shape[0]
    return pl.pallas_call(
        _mem_kv_kernel,
        grid=(m // tm, 2),
        in_specs=[
            pl.BlockSpec((tm, D_MODEL), lambda i, j: (i, 0)),
            pl.BlockSpec((1, D_MODEL), lambda i, j: (0, 0)),
            pl.BlockSpec((None, D_MODEL, D_MEM), lambda i, j: (j, 0, 0)),
        ],
        out_specs=[
            pl.BlockSpec((tm * ROW_TILES, LANE), lambda i, j: (i, 0)),
            pl.BlockSpec((tm * ROW_TILES, LANE), lambda i, j: (i, 0)),
        ],
        out_shape=[jax.ShapeDtypeStruct((m * ROW_TILES, LANE), F32)] * 2,
        scratch_shapes=[pltpu.VMEM((tm, D_MODEL), BF16)],
        compiler_params=_params(2),
        name="mem_kv",
    )(mem2d, norm_w, w_kv)


def _split_bf16(x, n):
    parts = []
    r = x
    for _ in range(n - 1):
        p = r.astype(BF16)
        parts.append(p)
        r = r - p.astype(F32)
    parts.append(r.astype(BF16))
    return parts


def _sum_rows(a, n, rows):
    out = a[(n - 1) * rows:n * rows]
    for i in range(n - 2, -1, -1):
        out = out + a[i * rows:(i + 1) * rows]
    return out


def _ssd_kernel(z_ref, xx_ref, xb_ref, xc_ref, dt_ref, hist_ref, h0_ref,
                cw_ref, cb_ref, dtb_ref, alog_ref, dskip_ref, nw_ref, e_ref,
                y_ref, nconv_ref, hnew_ref, xp_scr, st_scr, y_scr, *, q):
    c = pl.program_id(1)
    nc = pl.num_programs(1)
    hist_row = SUBLANE - (CONV_W - 1)

    @pl.when(c == 0)
    def _():
        xp_scr[hist_row:SUBLANE, :] = hist_ref[0]
        for g in range(N_SSM_GROUPS):
            hg = h0_ref[0, g * HEADS_PER_GROUP:(g + 1) * HEADS_PER_GROUP]
            st_scr[g] = hg.reshape(D_GROUP, D_STATE).T

    xp_scr[SUBLANE:SUBLANE + q, 0:D_SSM] = xx_ref[...].astype(F32)
    xp_scr[SUBLANE:SUBLANE + q, D_SSM:D_SSM + D_BC] = xb_ref[...].astype(F32)
    xp_scr[SUBLANE:SUBLANE + q, D_SSM + D_BC:D_CONV] = xc_ref[...].astype(F32)
    acc = cb_ref[...] + xp_scr[hist_row:hist_row + q, :] * cw_ref[0:1, :]
    for j in range(1, CONV_W):
        acc = acc + xp_scr[hist_row + j:hist_row + j + q, :] * cw_ref[j:j + 1, :]
    xbc = _silu(acc)
    xs = xbc[:, 0:D_SSM]
    xs_bf = xs.astype(BF16)
    bm = xbc[:, D_SSM:D_SSM + D_BC].astype(BF16)
    cm = xbc[:, D_SSM + D_BC:D_CONV].astype(BF16)

    @pl.when(c == nc - 1)
    def _():
        nconv_ref[0] = xp_scr[q + hist_row:q + SUBLANE, :]

    xp_scr[0:SUBLANE, :] = xp_scr[q:q + SUBLANE, :]

    dtv = jax.nn.softplus(dt_ref[...] + dtb_ref[...])
    a = -jnp.exp(alog_ref[...])
    da = dtv * a
    ri = lax.broadcasted_iota(jnp.int32, (q, q), 0)
    ci = lax.broadcasted_iota(jnp.int32, (q, q), 1)
    causal = ri >= ci
    tril = jnp.where(causal, 1.0, 0.0).astype(BF16)
    triu = jnp.where(ri <= ci, 1.0, 0.0).astype(BF16)
    eye = jnp.where(ri == ci, 1.0, 0.0).astype(BF16)
    da3 = jnp.concatenate(_split_bf16(da, 3), axis=1)
    dt3 = jnp.concatenate(_split_bf16(dtv, 3), axis=1)
    cum3 = jnp.dot(tril, da3, preferred_element_type=F32)
    cum = cum3[:, 2 * LANE:3 * LANE] + cum3[:, LANE:2 * LANE] + cum3[:, 0:LANE]
    tn_dims = (((0,), (0,)), ((), ()))
    cum_t = _sum_rows(lax.dot_general(da3, triu, tn_dims, preferred_element_type=F32), 3, LANE)
    dt_t = _sum_rows(lax.dot_general(dt3, eye, tn_dims, preferred_element_type=F32), 3, LANE)

    ecum = jnp.exp(cum)
    dd = dtv * jnp.exp(cum[q - 1:q, :] - cum)
    ex_in = jnp.concatenate(_split_bf16(ecum, 2) + _split_bf16(dd, 2), axis=0)
    ex = jnp.dot(ex_in, e_ref[...], preferred_element_type=F32)
    ecum_e = ex[q:2 * q] + ex[0:q]
    dd_e = ex[3 * q:4 * q] + ex[2 * q:3 * q]
    xdd = (xs * dd_e).astype(BF16)

    lane = lax.broadcasted_iota(jnp.int32, (q, LANE), 1)
    lo_half = lane < SSM_HEAD_DIM
    nt_dims = (((1,), (1,)), ((), ()))
    for g in range(N_SSM_GROUPS):
        bg = bm[:, g * D_STATE:(g + 1) * D_STATE]
        cg = cm[:, g * D_STATE:(g + 1) * D_STATE]
        gs = slice(g * D_GROUP, (g + 1) * D_GROUP)
        cbg = lax.dot_general(cg, bg, nt_dims, preferred_element_type=F32)
        st = st_scr[g]
        y_off = jnp.dot(cg, st.astype(BF16), preferred_element_type=F32) * ecum_e[:, gs]
        st_scr[g] = st * ecum_e[q - 1:q, gs] + lax.dot_general(
            bg, xdd[:, gs], tn_dims, preferred_element_type=F32)
        for t in range(HEADS_PER_GROUP // 2):
            ws = []
            for h in (g * HEADS_PER_GROUP + 2 * t, g * HEADS_PER_GROUP + 2 * t + 1):
                seg = cum[:, h:h + 1] - cum_t[h:h + 1, :]
                lmat = jnp.exp(jnp.where(causal, seg, -jnp.inf))
                ws.append((cbg * lmat * dt_t[h:h + 1, :]).astype(BF16))
            cs = slice(g * D_GROUP + t * LANE, g * D_GROUP + (t + 1) * LANE)
            xpair = xs_bf[:, cs]
            zero = jnp.zeros_like(xpair)
            y_diag = (jnp.dot(ws[0], jnp.where(lo_half, xpair, zero), preferred_element_type=F32)
                      + jnp.dot(ws[1], jnp.where(lo_half, zero, xpair), preferred_element_type=F32))
            y_scr[:, cs] = (y_diag + y_off[:, t * LANE:(t + 1) * LANE]) + xs[:, cs] * dskip_ref[:, cs]

    for g in range(N_SSM_GROUPS):
        gs = slice(g * D_GROUP, (g + 1) * D_GROUP)
        yg = y_scr[:, gs] * _silu(z_ref[:, gs].astype(F32))
        y_ref[:, gs] = (yg * _rms_scale(yg, NORM_EPS) * nw_ref[:, gs]).astype(BF16)

    @pl.when(c == nc - 1)
    def _():
        for g in range(N_SSM_GROUPS):
            hnew_ref[0, g * HEADS_PER_GROUP:(g + 1) * HEADS_PER_GROUP] = (
                st_scr[g].T.reshape(HEADS_PER_GROUP, SSM_HEAD_DIM, D_STATE))


def _ssd(main, dt, conv_hist, h0, b_off, lp, bsz, length, q):
    nc = length // q
    m = bsz * length
    row = lambda b, c: b * nc + c
    return pl.pallas_call(
        functools.partial(_ssd_kernel, q=q),
        grid=(bsz, nc),
        in_specs=[
            pl.BlockSpec((q, D_SSM), lambda b, c: (row(b, c), COL_Z // D_SSM)),
            pl.BlockSpec((q, D_SSM), lambda b, c: (row(b, c), COL_X // D_SSM)),
            pl.BlockSpec((q, D_BC), lambda b, c: (row(b, c), COL_B // D_BC)),
            pl.BlockSpec((q, D_BC), lambda b, c: (row(b, c), COL_C // D_BC)),
            pl.BlockSpec((q, LANE), lambda b, c: (row(b, c), 0)),
            pl.BlockSpec((1, CONV_W - 1, D_CONV), lambda b, c: (b_off + b, 0, 0)),
            pl.BlockSpec((1, N_SSM_HEADS, SSM_HEAD_DIM, D_STATE), lambda b, c: (b_off + b, 0, 0, 0)),
            pl.BlockSpec((CONV_W, D_CONV), lambda b, c: (0, 0)),
            pl.BlockSpec((1, D_CONV), lambda b, c: (0, 0)),
            pl.BlockSpec((1, LANE), lambda b, c: (0, 0)),
            pl.BlockSpec((1, LANE), lambda b, c: (0, 0)),
            pl.BlockSpec((1, D_SSM), lambda b, c: (0, 0)),
            pl.BlockSpec((1, D_SSM), lambda b, c: (0, 0)),
            pl.BlockSpec((LANE, D_SSM), lambda b, c: (0, 0)),
        ],
        out_specs=[
            pl.BlockSpec((q, D_SSM), lambda b, c: (row(b, c), 0)),
            pl.BlockSpec((1, CONV_W - 1, D_CONV), lambda b, c: (b, 0, 0)),
            pl.BlockSpec((1, N_SSM_HEADS, SSM_HEAD_DIM, D_STATE), lambda b, c: (b, 0, 0, 0)),
        ],
        out_shape=[
            jax.ShapeDtypeStruct((m, D_MODEL), BF16),
            jax.ShapeDtypeStruct((bsz, CONV_W - 1, D_CONV), F32),
            jax.ShapeDtypeStruct((bsz, N_SSM_HEADS, SSM_HEAD_DIM, D_STATE), F32),
        ],
        scratch_shapes=[
            pltpu.VMEM((q + SUBLANE, D_CONV), F32),
            pltpu.VMEM((N_SSM_GROUPS, D_STATE, D_GROUP), F32),
            pltpu.VMEM((q, D_SSM), F32),
        ],
        compiler_params=_params(2),
        name="ssd",
    )(main, main, main, main, dt, conv_hist, h0,
      lp["conv_w"], lp["conv_b"], lp["dt_bias"], lp["a_log"], lp["d_skip_e"], lp["ssm_norm_w"],
      lp["expand"])


def _lambda(lq1, lk1, lq2, lk2, lambda_init):
    return (jnp.exp(jnp.sum(lq1[...] * lk1[...], axis=-1, keepdims=True))
            - jnp.exp(jnp.sum(lq2[...] * lk2[...], axis=-1, keepdims=True)) + lambda_init)


def _diff_finish(o1, o2, lam, sw_ref, g, lambda_init):
    o = o1 - lam * o2
    on = (o * _rms_scale(o, SUBLN_EPS) * sw_ref[...]) * (1.0 - lambda_init)
    return (on * _silu(g.astype(F32))).astype(BF16)


ATTN_QT = 256


def _attn_scores(k_ref, q_ref, mp, nt, diagonal):
    cols = slice(mp * DIFF_HEAD_DIM, (mp + 1) * DIFF_HEAD_DIM)
    qs = slice(nt * ATTN_QT, (nt + 1) * ATTN_QT)
    kw = (nt + 1) * ATTN_QT if diagonal else k_ref.shape[0]
    s = lax.dot_general(k_ref[0:kw, cols], q_ref[qs, cols], (((1,), (1,)), ((), ())),
                        preferred_element_type=F32)
    if diagonal:
        r = lax.broadcasted_iota(jnp.int32, (ATTN_QT, ATTN_QT), 0)
        q = lax.broadcasted_iota(jnp.int32, (ATTN_QT, ATTN_QT), 1)
        tail = jnp.where(_chunk_of(r) <= _chunk_of(q), s[kw - ATTN_QT:kw], -jnp.inf)
        s = tail if kw == ATTN_QT else jnp.concatenate([s[0:kw - ATTN_QT], tail], axis=0)
    return s


def _attn_softmax(s, m_scr, l_scr, mp, nt):
    c = (DIFF_HEAD_DIM ** -0.5) * math.log2(math.e)
    qs = slice(nt * ATTN_QT, (nt + 1) * ATTN_QT)
    m_prev = m_scr[mp, :, qs]
    m_new = jnp.maximum(m_prev, jnp.max(s, axis=0, keepdims=True))
    p = jnp.exp2((s - m_new) * c)
    alpha = jnp.exp2((m_prev - m_new) * c)
    l_scr[mp, :, qs] = alpha * l_scr[mp, :, qs] + jnp.sum(p, axis=0, keepdims=True)
    m_scr[mp, :, qs] = m_new
    return p.astype(BF16), alpha


def _attn_value(vt, p, alpha, acc_scr, mp, nt):
    qs = slice(nt * ATTN_QT, (nt + 1) * ATTN_QT)
    kw = p.shape[0]
    acc_scr[mp, :, qs] = alpha * acc_scr[mp, :, qs] + jnp.dot(vt[:, 0:kw], p, preferred_element_type=F32)


def _diff_attn_kernel(qi_ref, ki_ref, q_ref, k_ref, v_ref, g_ref, lq1, lk1, lq2, lk2, sw_ref, y_hbm,
                      o_ref, m_scr, l_scr, acc_scr, *, tq, tk, lambda_init):
    del y_hbm
    qi = qi_ref[pl.program_id(2)]
    ki = ki_ref[pl.program_id(2)]
    assert tq == tk and tq % ATTN_QT == 0 and ATTN_QT % CHUNK == 0

    @pl.when(ki == 0)
    def _():
        m_scr[...] = jnp.full(m_scr.shape, -jnp.inf, F32)
        l_scr[...] = jnp.zeros(l_scr.shape, F32)
        acc_scr[...] = jnp.zeros(acc_scr.shape, F32)

    def block(diagonal):
        vt = v_ref[...].T
        chains = [(mp, nt) for mp in range(2) for nt in range(tq // ATTN_QT)]
        scores = [_attn_scores(k_ref, q_ref, mp, nt, diagonal) for mp, nt in chains]
        probs = [_attn_softmax(s, m_scr, l_scr, mp, nt) for s, (mp, nt) in zip(scores, chains)]
        for (p, alpha), (mp, nt) in zip(probs, chains):
            _attn_value(vt, p, alpha, acc_scr, mp, nt)

    @pl.when(ki < qi)
    def _():
        block(False)

    @pl.when(ki == qi)
    def _():
        block(True)
        lam = _lambda(lq1, lk1, lq2, lk2, lambda_init)
        o1 = (acc_scr[0] / l_scr[0]).T
        o2 = (acc_scr[1] / l_scr[1]).T
        o_ref[...] = _diff_finish(o1, o2, lam, sw_ref, g_ref[...], lambda_init)


def _diff_attn_prompt(main, y, lp, bsz, length, lambda_init, tq):
    tk = tq
    nq = length // tq
    m = bsz * length
    pairs = [(qi, ki) for qi in range(nq) for ki in range(qi + 1)]
    qi_tab = jnp.asarray([p[0] for p in pairs], jnp.int32)
    ki_tab = jnp.asarray([p[1] for p in pairs], jnp.int32)
    vec = pl.BlockSpec((1, DIFF_HEAD_DIM), lambda b, h, t, qt, kt: (0, 0))
    return pl.pallas_call(
        functools.partial(_diff_attn_kernel, tq=tq, tk=tk, lambda_init=lambda_init),
        grid_spec=pltpu.PrefetchScalarGridSpec(
            num_scalar_prefetch=2,
            grid=(bsz, N_DIFF_HEADS, len(pairs)),
            in_specs=[
                pl.BlockSpec((tq, HEAD_W), lambda b, h, t, qt, kt: (b * nq + qt[t], COL_QD // HEAD_W + h)),
                pl.BlockSpec((tk, HEAD_W), lambda b, h, t, qt, kt: (b * nq + kt[t], COL_KD // HEAD_W + h)),
                pl.BlockSpec((tk, HEAD_W), lambda b, h, t, qt, kt: (b * nq + kt[t], COL_VD // HEAD_W + h)),
                pl.BlockSpec((tq, HEAD_W), lambda b, h, t, qt, kt: (b * nq + qt[t], COL_GD // HEAD_W + h)),
                vec, vec, vec, vec,
                pl.BlockSpec((1, HEAD_W), lambda b, h, t, qt, kt: (0, 0)),
                pl.BlockSpec(memory_space=pl.ANY),
            ],
            out_specs=pl.BlockSpec((tq, HEAD_W),
                                   lambda b, h, t, qt, kt: (b * nq + qt[t], D_SSM // HEAD_W + h)),
            scratch_shapes=[
                pltpu.VMEM((2, 1, tq), F32),
                pltpu.VMEM((2, 1, tq), F32),
                pltpu.VMEM((2, HEAD_W, tq), F32),
            ],
        ),
        out_shape=jax.ShapeDtypeStruct((m, D_MODEL), BF16),
        input_output_aliases={11: 0},
        compiler_params=_params(3),
        name="diff_attn_prompt",
    )(qi_tab, ki_tab, main, main, main, main, lp["lambda_q1"], lp["lambda_k1"], lp["lambda_q2"],
      lp["lambda_k2"], lp["subln_w"], y)


def _diff_attn_decode_kernel(q_ref, kn_ref, vn_ref, g_ref, kp_ref, vp_ref, lq1, lk1, lq2, lk2, sw_ref,
                             y_hbm, o_ref, *, past, lambda_init):
    del y_hbm
    scale = DIFF_HEAD_DIM ** -0.5
    lq = q_ref.shape[0]
    nt_dims = (((1,), (1,)), ((), ()))
    mask_p = (_chunk_of(lax.broadcasted_iota(jnp.int32, (lq, past), 1))
              <= _chunk_of(past + lax.broadcasted_iota(jnp.int32, (lq, past), 0)))
    mask_n = (_chunk_of(past + lax.broadcasted_iota(jnp.int32, (lq, lq), 1))
              <= _chunk_of(past + lax.broadcasted_iota(jnp.int32, (lq, lq), 0)))
    lam = _lambda(lq1, lk1, lq2, lk2, lambda_init)
    for h in range(N_DIFF_HEADS):
        hs = slice(h * HEAD_W, (h + 1) * HEAD_W)
        vp = _load_flat(vp_ref, past, h * HEAD_W, HEAD_W, halves_split=True).astype(BF16)
        vn = vn_ref[:, hs]
        outs = []
        for mp in range(2):
            c0 = h * HEAD_W + mp * DIFF_HEAD_DIM
            qm = q_ref[:, c0:c0 + DIFF_HEAD_DIM]
            kp = _load_flat(kp_ref, past, c0, DIFF_HEAD_DIM, halves_split=False).astype(BF16)
            sp = lax.dot_general(qm, kp, nt_dims, preferred_element_type=F32) * scale
            sn = lax.dot_general(qm, kn_ref[:, c0:c0 + DIFF_HEAD_DIM], nt_dims,
                                 preferred_element_type=F32) * scale
            sp = jnp.where(mask_p, sp, -jnp.inf)
            sn = jnp.where(mask_n, sn, -jnp.inf)
            mx = jnp.maximum(jnp.max(sp, axis=-1, keepdims=True), jnp.max(sn, axis=-1, keepdims=True))
            pp = jnp.exp(sp - mx)
            pn = jnp.exp(sn - mx)
            denom = jnp.sum(pp, axis=-1, keepdims=True) + jnp.sum(pn, axis=-1, keepdims=True)
            o = (jnp.dot(pp.astype(BF16), vp, preferred_element_type=F32)
                 + jnp.dot(pn.astype(BF16), vn, preferred_element_type=F32))
            outs.append(o / denom)
        o_ref[:, hs] = _diff_finish(outs[0], outs[1], lam, sw_ref, g_ref[:, hs], lambda_init)


def _diff_attn_decode(main, y, k_past, v_past, b_off, lp, bsz, length, past, lambda_init):
    m = bsz * length
    vec = pl.BlockSpec((1, DIFF_HEAD_DIM), lambda b: (0, 0))
    return pl.pallas_call(
        functools.partial(_diff_attn_decode_kernel, past=past, lambda_init=lambda_init),
        grid=(bsz,),
        in_specs=[
            pl.BlockSpec((length, D_DIFF), lambda b: (b, COL_QD // D_DIFF)),
            pl.BlockSpec((length, D_DIFF), lambda b: (b, COL_KD // D_DIFF)),
            pl.BlockSpec((length, D_DIFF), lambda b: (b, COL_VD // D_DIFF)),
            pl.BlockSpec((length, D_DIFF), lambda b: (b, COL_GD // D_DIFF)),
            pl.BlockSpec((past * ROW_TILES, LANE), lambda b: (b_off + b, 0)),
            pl.BlockSpec((past * ROW_TILES, LANE), lambda b: (b_off + b, 0)),
            vec, vec, vec, vec,
            pl.BlockSpec((1, HEAD_W), lambda b: (0, 0)),
            pl.BlockSpec(memory_space=pl.ANY),
        ],
        out_specs=pl.BlockSpec((length, D_DIFF), lambda b: (b, D_SSM // D_DIFF)),
        out_shape=jax.ShapeDtypeStruct((m, D_MODEL), BF16),
        input_output_aliases={11: 0},
        compiler_params=_params(1),
        name="diff_attn_decode",
    )(main, main, main, main, k_past, v_past,
      lp["lambda_q1"], lp["lambda_k1"], lp["lambda_q2"], lp["lambda_k2"], lp["subln_w"], y)


def _mem_attn_kernel(q_ref, g_ref, mk_ref, mv_ref, y_hbm, o_ref, *, n_mem):
    del y_hbm
    scale = MEM_HEAD_DIM ** -0.5
    for h in range(N_MEM_HEADS):
        hs = slice(h * MEM_HEAD_DIM, (h + 1) * MEM_HEAD_DIM)
        mk = _load_flat(mk_ref, n_mem, h * MEM_HEAD_DIM, MEM_HEAD_DIM, halves_split=True).astype(BF16)
        mv = _load_flat(mv_ref, n_mem, h * MEM_HEAD_DIM, MEM_HEAD_DIM, halves_split=True).astype(BF16)
        s = lax.dot_general(q_ref[:, hs], mk, (((1,), (1,)), ((), ())), preferred_element_type=F32) * scale
        p = jnp.exp(s - jnp.max(s, axis=-1, keepdims=True))
        pr = p / jnp.sum(p, axis=-1, keepdims=True)
        o = jnp.dot(pr.astype(BF16), mv, preferred_element_type=F32)
        o_ref[:, hs] = (o * _silu(g_ref[:, hs].astype(F32))).astype(BF16)


def _mem_attn(main, y, mk, mv, b_off, bsz, length, n_mem, tq):
    nq = length // tq
    m = bsz * length
    return pl.pallas_call(
        functools.partial(_mem_attn_kernel, n_mem=n_mem),
        grid=(bsz, nq),
        in_specs=[
            pl.BlockSpec((tq, D_MEM), lambda b, qi: (b * nq + qi, COL_QM // D_MEM)),
            pl.BlockSpec((tq, D_MEM), lambda b, qi: (b * nq + qi, COL_GM // D_MEM)),
            pl.BlockSpec((n_mem * ROW_TILES, LANE), lambda b, qi: (b_off + b, 0)),
            pl.BlockSpec((n_mem * ROW_TILES, LANE), lambda b, qi: (b_off + b, 0)),
            pl.BlockSpec(memory_space=pl.ANY),
        ],
        out_specs=pl.BlockSpec((tq, D_MEM), lambda b, qi: (b * nq + qi, (D_SSM + D_DIFF) // D_MEM)),
        out_shape=jax.ShapeDtypeStruct((m, D_MODEL), BF16),
        input_output_aliases={4: 0},
        compiler_params=_params(2),
        name="mem_attn",
    )(main, main, mk, mv, y)


def _out_proj_kernel(y_ref, w_ref, x_ref, nw_ref, o_ref):
    n_tiles = w_ref.shape[0]
    tn = w_ref.shape[2]
    y = y_ref[...]
    for t in range(n_tiles):
        o_ref[:, t * tn:(t + 1) * tn] = jnp.dot(y, w_ref[t], preferred_element_type=F32)
    tm = o_ref.shape[0]
    step = min(NORM_ROWS, tm)
    for r in range(0, tm, step):
        o = o_ref[r:r + step, :]
        o_ref[r:r + step, :] = x_ref[r:r + step, :] + o * _rms_scale(o, NORM_EPS) * nw_ref[...]


def _out_proj(y, w_out, x2d, norm_w, tm):
    m = x2d.shape[0]
    return pl.pallas_call(
        _out_proj_kernel,
        grid=(m // tm,),
        in_specs=[
            pl.BlockSpec((tm, D_MODEL), lambda i: (i, 0)),
            pl.BlockSpec(w_out.shape, lambda i: (0, 0, 0), pipeline_mode=pl.Buffered(1)),
            pl.BlockSpec((tm, D_MODEL), lambda i: (i, 0)),
            pl.BlockSpec((1, D_MODEL), lambda i: (0, 0)),
        ],
        out_specs=pl.BlockSpec((tm, D_MODEL), lambda i: (i, 0)),
        out_shape=jax.ShapeDtypeStruct((m, D_MODEL), F32),
        compiler_params=_params(1, OUT_PROJ_VMEM_LIMIT),
        name="out_proj",
    )(y, w_out, x2d, norm_w)


def _mixer_layer(x2d, bsz, length, layer, depth, kv_all, b_off, mem_off, conv_hist, h0, kv_past, past, mk,
                 mv, n_mem, lp, lambda_init, tiles):
    main, k32, v32, dt = _in_proj(x2d, lp["norm_pre_w"], lp["w_main"], lp["w_dt"], tiles["proj_tm"],
                                  layer, depth, kv_all)
    y, new_conv, h_new = _ssd(main, dt, conv_hist, h0, b_off, lp, bsz, length, tiles["ssd_q"])
    if kv_past is None:
        y = _diff_attn_prompt(main, y, lp, bsz, length, lambda_init, tiles["attn_tq"])
    else:
        y = _diff_attn_decode(main, y, kv_past[0], kv_past[1], b_off, lp, bsz, length, past, lambda_init)
    y = _mem_attn(main, y, mk, mv, mem_off, bsz, length, n_mem, tiles["mem_tq"])
    x_new = _out_proj(y, lp["w_out"], x2d, lp["norm_post_w"], tiles["out_tm"])
    return x_new, new_conv, h_new, (k32, v32)


def _layer_params(l, norm_pre_w, norm_post_w, w_in_t, conv_w, conv_b, dt_bias, a_log, d_skip,
                  ssm_norm_w, lambda_q1, lambda_k1, lambda_q2, lambda_k2, subln_w, mem_norm_w,
                  w_mem_kv, w_out):
    w_main, w_dt = _w_prep(w_in_t, l)
    pad_heads = lambda v: jnp.pad(v, (0, LANE - N_SSM_HEADS)).reshape(1, LANE)
    head_of_channel = jnp.arange(D_SSM) // SSM_HEAD_DIM
    expand = (jnp.arange(LANE)[:, None] == head_of_channel[None, :]).astype(BF16)
    return {
        "norm_pre_w": norm_pre_w[l].reshape(1, D_MODEL),
        "norm_post_w": norm_post_w[l].reshape(1, D_MODEL),
        "w_main": w_main,
        "w_dt": w_dt,
        "conv_w": conv_w[l],
        "conv_b": conv_b[l].reshape(1, D_CONV),
        "dt_bias": pad_heads(dt_bias[l]),
        "a_log": pad_heads(a_log[l]),
        "d_skip_e": jnp.repeat(d_skip[l], SSM_HEAD_DIM).reshape(1, D_SSM),
        "ssm_norm_w": ssm_norm_w[l].reshape(1, D_SSM),
        "expand": expand,
        "lambda_q1": lambda_q1[l].reshape(1, DIFF_HEAD_DIM),
        "lambda_k1": lambda_k1[l].reshape(1, DIFF_HEAD_DIM),
        "lambda_q2": lambda_q2[l].reshape(1, DIFF_HEAD_DIM),
        "lambda_k2": lambda_k2[l].reshape(1, DIFF_HEAD_DIM),
        "subln_w": subln_w[l].reshape(1, HEAD_W),
        "mem_norm_w": mem_norm_w[l].reshape(1, D_MODEL),
        "w_mem_kv": _layer_bf16_tiles(w_mem_kv, l, D_MEM),
        "w_out": _layer_bf16_tiles(w_out, l, OUT_TN),
    }


OUT_TN = 512


def _tiles(bsz, length):
    m = bsz * length
    return {
        "proj_tm": min(512, m),
        "ssd_q": min(128, length),
        "attn_tq": min(1024, length),
        "mem_tq": min(1024, length),
        "out_tm": min(256, m),
    }


def kernel(x_prompt, x_sample, mem_prompt, cache_conv, state_ssm, cache_k, cache_v, cache_mem_k,
           cache_mem_v, norm_pre_w, norm_post_w, w_in, conv_w, conv_b, dt_bias, a_log, d_skip,
           ssm_norm_w, lambda_q1, lambda_k1, lambda_q2, lambda_k2, subln_w, mem_norm_w, w_mem_kv,
           w_out):
    depth = w_in.shape[0]
    bp, lp_, _ = x_prompt.shape
    bs, ls, _ = x_sample.shape
    n_mem = mem_prompt.shape[1]
    past = cache_k.shape[2]
    xp = x_prompt.reshape(bp * lp_, D_MODEL)
    xs = x_sample.reshape(bs * ls, D_MODEL)
    mem2d = mem_prompt.reshape(bp * n_mem, D_MODEL)
    tiles_p = _tiles(bp, lp_)
    tiles_s = _tiles(bs, ls)
    conv0 = jnp.zeros((bp, CONV_W - 1, D_CONV), F32)
    h_zero = jnp.zeros((bp, N_SSM_HEADS, SSM_HEAD_DIM, D_STATE), F32)
    conv_all = cache_conv.reshape(depth * bs, CONV_W - 1, D_CONV)
    ssm_all = state_ssm.reshape(depth * bs, N_SSM_HEADS, SSM_HEAD_DIM, D_STATE)
    kv_all = (cache_k.reshape(-1, LANE), _to_flat_halves(cache_v))
    mem_k_all = _to_flat_halves(cache_mem_k)
    mem_v_all = _to_flat_halves(cache_mem_v)
    outs = [[] for _ in range(6)]
    kv_p = kv_s = None
    w_in_t = jnp.swapaxes(w_in, 1, 2)
    for l in range(depth):
        lp = _layer_params(l, norm_pre_w, norm_post_w, w_in_t, conv_w, conv_b, dt_bias, a_log, d_skip,
                           ssm_norm_w, lambda_q1, lambda_k1, lambda_q2, lambda_k2, subln_w,
                           mem_norm_w, w_mem_kv, w_out)
        lambda_init = 0.8 - 0.6 * math.exp(-0.3 * l)
        mk_p, mv_p = _mem_kv(mem2d, lp["mem_norm_w"], lp["w_mem_kv"], min(512, bp * n_mem))
        xp, c_p, h_p, kv_p = _mixer_layer(xp, bp, lp_, l, depth, kv_p, 0, 0, conv0, h_zero, None, 0, mk_p,
                                          mv_p, n_mem, lp, lambda_init, tiles_p)
        xs, c_s, h_s, kv_s = _mixer_layer(xs, bs, ls, l, depth, kv_s, l * bs, l * bs, conv_all, ssm_all,
                                          kv_all, past, mem_k_all, mem_v_all, n_mem, lp, lambda_init,
                                          tiles_s)
        for lst, val in zip(outs, (c_p, h_p, mk_p, mv_p, c_s, h_s)):
            lst.append(val)
    st = [jnp.stack(o) for o in outs]
    return (
        xp.reshape(bp, lp_, D_MODEL),
        xs.reshape(bs, ls, D_MODEL),
        st[0],
        st[1],
        kv_p[0].reshape(depth, bp, lp_, N_DIFF_HEADS, 2, DIFF_HEAD_DIM),
        _from_flat_halves(kv_p[1], (depth, bp, lp_, N_DIFF_HEADS, 2 * DIFF_HEAD_DIM)),
        _from_flat_halves(st[2], (depth, bp, n_mem, N_MEM_HEADS, MEM_HEAD_DIM)),
        _from_flat_halves(st[3], (depth, bp, n_mem, N_MEM_HEADS, MEM_HEAD_DIM)),
        st[4],
        st[5],
        kv_s[0].reshape(depth, bs, ls, N_DIFF_HEADS, 2, DIFF_HEAD_DIM),
        _from_flat_halves(kv_s[1], (depth, bs, ls, N_DIFF_HEADS, 2 * DIFF_HEAD_DIM)),
    )
```

```python
import functools
import math

import jax
import jax.numpy as jnp
from jax import lax
from jax.experimental import pallas as pl
from jax.experimental.pallas import tpu as pltpu

F32 = jnp.float32
BF16 = jnp.bfloat16

D_MODEL = 4096
D_SSM = 2048
SSM_HEAD_DIM = 64
N_SSM_HEADS = 32
N_SSM_GROUPS = 4
HEADS_PER_GROUP = N_SSM_HEADS // N_SSM_GROUPS
D_GROUP = D_SSM // N_SSM_GROUPS
D_STATE = 128
CONV_W = 4
D_BC = N_SSM_GROUPS * D_STATE
D_CONV = D_SSM + 2 * D_BC
D_DIFF = 1024
N_DIFF_HEADS = 4
DIFF_HEAD_DIM = 128
D_MEM = 1024
N_MEM_HEADS = 4
MEM_HEAD_DIM = 256
CHUNK = 64
NORM_EPS = 1e-6
SUBLN_EPS = 1e-5

LANE = 128
SUBLANE = 8
VMEM_LIMIT = 56 * 1024 * 1024
OUT_PROJ_VMEM_LIMIT = 60 * 1024 * 1024

HEAD_W = 2 * DIFF_HEAD_DIM
COL_KD = 0
COL_VD = COL_KD + D_DIFF
COL_Z = COL_VD + D_DIFF
COL_X = COL_Z + D_SSM
COL_B = COL_X + D_SSM
COL_C = COL_B + D_BC
COL_QD = COL_C + D_BC
COL_GD = COL_QD + D_DIFF
COL_QM = COL_GD + D_DIFF
COL_GM = COL_QM + D_MEM
D_MAIN = COL_GM + D_MEM
PROJ_TN = 1024
K_TILE = COL_KD // PROJ_TN
V_TILE = COL_VD // PROJ_TN


def _params(n_axes, vmem_limit=VMEM_LIMIT):
    return pltpu.CompilerParams(dimension_semantics=("arbitrary",) * n_axes,
                                vmem_limit_bytes=vmem_limit)


def _silu(x):
    return x * jax.nn.sigmoid(x)


def _chunk_of(pos):
    assert CHUNK & (CHUNK - 1) == 0
    return lax.shift_right_logical(pos, jnp.int32(CHUNK.bit_length() - 1))


def _rms_scale(x, eps):
    return lax.rsqrt(jnp.mean(x * x, axis=-1, keepdims=True) + eps)


ROW_TILES = 1024 // LANE
N_HEADS_1024 = 1024 // HEAD_W


def _flat_row(c, halves_split):
    head, half = divmod(c, 2)
    return half * N_HEADS_1024 + head if halves_split else c


def _store_flat(ref, val, halves_split):
    rows = val.shape[0]
    for c in range(ROW_TILES):
        ref[pl.ds(_flat_row(c, halves_split), rows, stride=ROW_TILES), :] = val[:, c * LANE:(c + 1) * LANE]


def _load_flat(ref, rows, col0, width, halves_split):
    parts = [ref[pl.ds(_flat_row(col0 // LANE + c, halves_split), rows, stride=ROW_TILES), :]
             for c in range(width // LANE)]
    return parts[0] if len(parts) == 1 else jnp.concatenate(parts, axis=1)


def _to_flat_halves(a):
    return a.reshape(-1, N_HEADS_1024, 2, LANE).transpose(0, 2, 1, 3).reshape(-1, LANE)


def _from_flat_halves(flat, shape):
    return flat.reshape(-1, 2, N_HEADS_1024, LANE).transpose(0, 2, 1, 3).reshape(shape)


NORM_ROWS = 128


def _norm_rows(x_ref, nw_ref, h_scr):
    tm = x_ref.shape[0]
    step = min(NORM_ROWS, tm)
    for r in range(0, tm, step):
        x = x_ref[r:r + step, :]
        h_scr[r:r + step, :] = (x * _rms_scale(x, NORM_EPS) * nw_ref[...]).astype(BF16)


O_DT = D_SSM + D_CONV
O_Q = O_DT + N_SSM_HEADS
MAIN_PIECES = ((COL_Z, 0, O_DT), (COL_QD, O_Q, D_DIFF), (COL_KD, O_Q + D_DIFF, D_DIFF),
               (COL_VD, O_Q + 2 * D_DIFF, D_DIFF), (COL_GD, O_Q + 3 * D_DIFF, D_DIFF),
               (COL_QM, O_Q + 4 * D_DIFF, D_MEM), (COL_GM, O_Q + 4 * D_DIFF + D_MEM, D_MEM))
PREP_COLS = 256
PREP_PER_TILE = PROJ_TN // PREP_COLS


def _prep_source_columns():
    src = []
    for dst, first, width in sorted(MAIN_PIECES):
        assert dst == len(src) * PREP_COLS
        src += list(range(first, first + width, PREP_COLS))
    assert len(src) * PREP_COLS == D_MAIN
    return jnp.asarray(src, jnp.int32)


def _w_prep_kernel(src_ref, w_ref, wdt_ref, main_ref, dt_ref):
    del src_ref
    main_ref[...] = w_ref[0].T.astype(BF16)

    @pl.when(pl.program_id(0) == 0)
    def _():
        lane = lax.broadcasted_iota(jnp.int32, (1, LANE), 1)
        dt_ref[...] = jnp.where(lane < N_SSM_HEADS, wdt_ref[...].T, 0.0).astype(BF16)


def _w_prep(w_in_t, layer):
    return pl.pallas_call(
        _w_prep_kernel,
        grid_spec=pltpu.PrefetchScalarGridSpec(
            num_scalar_prefetch=1,
            grid=(D_MAIN // PREP_COLS,),
            in_specs=[
                pl.BlockSpec((pl.Element(1), pl.Element(PREP_COLS), pl.Element(D_MODEL)),
                             lambda j, src: (layer, pl.multiple_of(src[j], N_SSM_HEADS), 0)),
                pl.BlockSpec((None, LANE, D_MODEL), lambda j, src: (layer, O_DT // LANE, 0)),
            ],
            out_specs=[
                pl.BlockSpec((None, D_MODEL, PREP_COLS),
                             lambda j, src: (j // PREP_PER_TILE, 0, j % PREP_PER_TILE)),
                pl.BlockSpec((D_MODEL, LANE), lambda j, src: (0, 0)),
            ],
        ),
        out_shape=[
            jax.ShapeDtypeStruct((D_MAIN // PROJ_TN, D_MODEL, PROJ_TN), BF16),
            jax.ShapeDtypeStruct((D_MODEL, LANE), BF16),
        ],
        compiler_params=_params(1),
        name="w_prep",
    )(_prep_source_columns(), w_in_t, w_in_t)


CAST_ROWS = 1024


def _cast_kernel(w_ref, o_ref):
    o_ref[...] = w_ref[...].astype(BF16)


def _layer_bf16_tiles(w, layer, tn):
    _, rows, cols = w.shape
    return pl.pallas_call(
        _cast_kernel,
        grid=(rows // CAST_ROWS, cols // tn),
        in_specs=[pl.BlockSpec((None, CAST_ROWS, tn), lambda i, t: (layer, i, t))],
        out_specs=pl.BlockSpec((None, CAST_ROWS, tn), lambda i, t: (t, i, 0)),
        out_shape=jax.ShapeDtypeStruct((cols // tn, rows, tn), BF16),
        compiler_params=_params(2),
        name="cast_bf16",
    )(w)


def _in_proj_kernel(x_ref, nw_ref, w_ref, wdt_ref, *refs):
    main_ref, k_ref, v_ref, dt_ref, h_scr = refs[-5:]
    j = pl.program_id(1)

    @pl.when(j == 0)
    def _():
        _norm_rows(x_ref, nw_ref, h_scr)
        dt_ref[...] = jnp.dot(h_scr[...], wdt_ref[...], preferred_element_type=F32)

    acc = jnp.dot(h_scr[...], w_ref[...], preferred_element_type=F32)
    main_ref[...] = acc.astype(BF16)

    @pl.when(j == K_TILE)
    def _():
        _store_flat(k_ref, acc, halves_split=False)

    @pl.when(j == V_TILE)
    def _():
        _store_flat(v_ref, acc, halves_split=True)


def _in_proj(x2d, norm_w, w_main, w_dt, tm, layer, depth, kv_all):
    m = x2d.shape[0]
    ni = m // tm
    grid = (ni, D_MAIN // PROJ_TN)
    kv_spec = pl.BlockSpec((tm * ROW_TILES, LANE), lambda i, j: (layer * ni + i, 0))
    kv_shape = jax.ShapeDtypeStruct((depth * m * ROW_TILES, LANE), F32)
    in_specs = [
        pl.BlockSpec((tm, D_MODEL), lambda i, j: (i, 0)),
        pl.BlockSpec((1, D_MODEL), lambda i, j: (0, 0)),
        pl.BlockSpec((None, D_MODEL, PROJ_TN), lambda i, j: (j, 0, 0)),
        pl.BlockSpec((D_MODEL, LANE), lambda i, j: (0, 0)),
    ]
    operands = [x2d, norm_w, w_main, w_dt]
    aliases = {}
    if kv_all is not None:
        in_specs += [pl.BlockSpec(memory_space=pl.ANY)] * 2
        aliases = {len(operands): 1, len(operands) + 1: 2}
        operands += list(kv_all)
    return pl.pallas_call(
        _in_proj_kernel,
        grid=grid,
        in_specs=in_specs,
        out_specs=[
            pl.BlockSpec((tm, PROJ_TN), lambda i, j: (i, j)),
            kv_spec,
            kv_spec,
            pl.BlockSpec((tm, LANE), lambda i, j: (i, 0)),
        ],
        out_shape=[
            jax.ShapeDtypeStruct((m, D_MAIN), BF16),
            kv_shape,
            kv_shape,
            jax.ShapeDtypeStruct((m, LANE), F32),
        ],
        scratch_shapes=[pltpu.VMEM((tm, D_MODEL), BF16)],
        input_output_aliases=aliases,
        compiler_params=_params(2),
        name="in_proj",
    )(*operands)


def _mem_kv_kernel(x_ref, nw_ref, w_ref, mk_ref, mv_ref, h_scr):
    j = pl.program_id(1)

    @pl.when(j == 0)
    def _():
        _norm_rows(x_ref, nw_ref, h_scr)

    acc = jnp.dot(h_scr[...], w_ref[...], preferred_element_type=F32)

    @pl.when(j == 0)
    def _():
        _store_flat(mk_ref, acc, halves_split=True)

    @pl.when(j == 1)
    def _():
        _store_flat(mv_ref, acc, halves_split=True)


def _mem_kv(mem2d, norm_w, w_kv, tm):
    m = mem2d.shape[0]
    return pl.pallas_call(
        _mem_kv_kernel,
        grid=(m // tm, 2),
        in_specs=[
            pl.BlockSpec((tm, D_MODEL), lambda i, j: (i, 0)),
            pl.BlockSpec((1, D_MODEL), lambda i, j: (0, 0)),
            pl.BlockSpec((None, D_MODEL, D_MEM), lambda i, j: (j, 0, 0)),
        ],
        out_specs=[
            pl.BlockSpec((tm * ROW_TILES, LANE), lambda i, j: (i, 0)),
            pl.BlockSpec((tm * ROW_TILES, LANE), lambda i, j: (i, 0)),
        ],
        out_shape=[jax.ShapeDtypeStruct((m * ROW_TILES, LANE), F32)] * 2,
        scratch_shapes=[pltpu.VMEM((tm, D_MODEL), BF16)],
        compiler_params=_params(2),
        name="mem_kv",
    )(mem2d, norm_w, w_kv)


def _split_bf16(x, n):
    parts = []
    r = x
    for _ in range(n - 1):
        p = r.astype(BF16)
        parts.append(p)
        r = r - p.astype(F32)
    parts.append(r.astype(BF16))
    return parts


def _sum_rows(a, n, rows):
    out = a[(n - 1) * rows:n * rows]
    for i in range(n - 2, -1, -1):
        out = out + a[i * rows:(i + 1) * rows]
    return out


HIST_ROWS = 16


def _conv_shift_matrix(q):
    t = jnp.arange(q)[:, None]
    col = jnp.arange(2 * HIST_ROWS + q)[None, :]
    mats = []
    for j in range(CONV_W - 1):
        src = t - (CONV_W - 1 - j)
        mats.append((col == 2 * HIST_ROWS + src) | ((src < 0) & (col == HIST_ROWS + src)))
    return jnp.concatenate(mats, axis=0).astype(BF16)


def _ssd_kernel(z_ref, xx_ref, xb_ref, xc_ref, dt_ref, hist_ref, h0_ref,
                cw_ref, cb_ref, dtb_ref, alog_ref, dskip_ref, nw_ref, e_ref, shift_ref,
                y_ref, nconv_ref, hnew_ref, xp_scr, st_scr, y_scr, *, q):
    c = pl.program_id(1)
    nc = pl.num_programs(1)
    blk = slice(2 * HIST_ROWS, 2 * HIST_ROWS + q)

    @pl.when(c == 0)
    def _():
        hist = jnp.concatenate(
            [jnp.zeros((HIST_ROWS - (CONV_W - 1), D_CONV), F32), hist_ref[0]], axis=0)
        hi, lo = _split_bf16(hist, 2)
        xp_scr[0:HIST_ROWS, :] = hi
        xp_scr[HIST_ROWS:2 * HIST_ROWS, :] = lo
        for g in range(N_SSM_GROUPS):
            hg = h0_ref[0, g * HEADS_PER_GROUP:(g + 1) * HEADS_PER_GROUP]
            st_scr[g] = hg.reshape(D_GROUP, D_STATE).T

    xp_scr[blk, 0:D_SSM] = xx_ref[...]
    xp_scr[blk, D_SSM:D_SSM + D_BC] = xb_ref[...]
    xp_scr[blk, D_SSM + D_BC:D_CONV] = xc_ref[...]
    taps = jnp.dot(shift_ref[...], xp_scr[...], preferred_element_type=F32)
    acc = cb_ref[...] + taps[0:q] * cw_ref[0:1, :]
    for j in range(1, CONV_W - 1):
        acc = acc + taps[j * q:(j + 1) * q] * cw_ref[j:j + 1, :]
    cur = xp_scr[blk, :].astype(F32)
    acc = acc + cur * cw_ref[CONV_W - 1:CONV_W, :]
    xbc = _silu(acc)
    xs = xbc[:, 0:D_SSM]
    xs_bf = xs.astype(BF16)
    bm = xbc[:, D_SSM:D_SSM + D_BC].astype(BF16)
    cm = xbc[:, D_SSM + D_BC:D_CONV].astype(BF16)

    @pl.when(c == nc - 1)
    def _():
        nconv_ref[0] = cur[q - (CONV_W - 1):q]

    xp_scr[0:HIST_ROWS, :] = xp_scr[2 * HIST_ROWS + q - HIST_ROWS:2 * HIST_ROWS + q, :]
    xp_scr[HIST_ROWS:2 * HIST_ROWS, :] = jnp.zeros((HIST_ROWS, D_CONV), BF16)

    dtv = jax.nn.softplus(dt_ref[...] + dtb_ref[...])
    a = -jnp.exp(alog_ref[...])
    da = dtv * a
    ri = lax.broadcasted_iota(jnp.int32, (q, q), 0)
    ci = lax.broadcasted_iota(jnp.int32, (q, q), 1)
    causal = ri >= ci
    tril = jnp.where(causal, 1.0, 0.0).astype(BF16)
    triu = jnp.where(ri <= ci, 1.0, 0.0).astype(BF16)
    eye = jnp.where(ri == ci, 1.0, 0.0).astype(BF16)
    da3 = jnp.concatenate(_split_bf16(da, 3), axis=1)
    dt3 = jnp.concatenate(_split_bf16(dtv, 3), axis=1)
    cum3 = jnp.dot(tril, da3, preferred_element_type=F32)
    cum = cum3[:, 2 * LANE:3 * LANE] + cum3[:, LANE:2 * LANE] + cum3[:, 0:LANE]
    tn_dims = (((0,), (0,)), ((), ()))
    cum_t = _sum_rows(lax.dot_general(da3, triu, tn_dims, preferred_element_type=F32), 3, LANE)
    dt_t = _sum_rows(lax.dot_general(dt3, eye, tn_dims, preferred_element_type=F32), 3, LANE)

    ecum = jnp.exp(cum)
    dd = dtv * jnp.exp(cum[q - 1:q, :] - cum)
    ex_in = jnp.concatenate([jnp.concatenate(_split_bf16(ecum, 2), axis=1),
                             jnp.concatenate(_split_bf16(dd, 2), axis=1)], axis=0)
    ex = jnp.dot(ex_in, e_ref[...], preferred_element_type=F32)
    ecum_e = ex[0:q]
    dd_e = ex[q:2 * q]
    xdd = (xs * dd_e).astype(BF16)

    lane = lax.broadcasted_iota(jnp.int32, (q, LANE), 1)
    lo_half = lane < SSM_HEAD_DIM
    nt_dims = (((1,), (1,)), ((), ()))
    for g in range(N_SSM_GROUPS):
        bg = bm[:, g * D_STATE:(g + 1) * D_STATE]
        cg = cm[:, g * D_STATE:(g + 1) * D_STATE]
        gs = slice(g * D_GROUP, (g + 1) * D_GROUP)
        cbg = lax.dot_general(cg, bg, nt_dims, preferred_element_type=F32)
        st = st_scr[g]
        y_off = jnp.dot(cg, st.astype(BF16), preferred_element_type=F32) * ecum_e[:, gs]
        st_scr[g] = st * ecum_e[q - 1:q, gs] + lax.dot_general(
            bg, xdd[:, gs], tn_dims, preferred_element_type=F32)
        for t in range(HEADS_PER_GROUP // 4):
            ws, xrows = [], []
            for k in range(4):
                h = g * HEADS_PER_GROUP + 4 * t + k
                seg = cum[:, h:h + 1] - cum_t[h:h + 1, :]
                lmat = jnp.exp(jnp.where(causal, seg, -jnp.inf))
                ws.append((cbg * lmat * dt_t[h:h + 1, :]).astype(BF16))
                pair = xs_bf[:, g * D_GROUP + (2 * t + k // 2) * LANE:g * D_GROUP + (2 * t + k // 2 + 1) * LANE]
                zero = jnp.zeros_like(pair)
                half = jnp.where(lo_half, pair, zero) if k % 2 == 0 else jnp.where(lo_half, zero, pair)
                xrows.append(jnp.concatenate([half, zero] if k < 2 else [zero, half], axis=1))
            cs = slice(g * D_GROUP + 2 * t * LANE, g * D_GROUP + (2 * t + 2) * LANE)
            y_diag = jnp.dot(jnp.concatenate(ws, axis=1), jnp.concatenate(xrows, axis=0),
                             preferred_element_type=F32)
            y_scr[:, cs] = (y_diag + y_off[:, 2 * t * LANE:(2 * t + 2) * LANE]) + xs[:, cs] * dskip_ref[:, cs]

    for g in range(N_SSM_GROUPS):
        gs = slice(g * D_GROUP, (g + 1) * D_GROUP)
        yg = y_scr[:, gs] * _silu(z_ref[:, gs].astype(F32))
        y_ref[:, gs] = (yg * _rms_scale(yg, NORM_EPS) * nw_ref[:, gs]).astype(BF16)

    @pl.when(c == nc - 1)
    def _():
        for g in range(N_SSM_GROUPS):
            hnew_ref[0, g * HEADS_PER_GROUP:(g + 1) * HEADS_PER_GROUP] = (
                st_scr[g].T.reshape(HEADS_PER_GROUP, SSM_HEAD_DIM, D_STATE))


def _ssd(main, dt, conv_hist, h0, b_off, lp, bsz, length, q):
    nc = length // q
    m = bsz * length
    row = lambda b, c: b * nc + c
    return pl.pallas_call(
        functools.partial(_ssd_kernel, q=q),
        grid=(bsz, nc),
        in_specs=[
            pl.BlockSpec((q, D_SSM), lambda b, c: (row(b, c), COL_Z // D_SSM)),
            pl.BlockSpec((q, D_SSM), lambda b, c: (row(b, c), COL_X // D_SSM)),
            pl.BlockSpec((q, D_BC), lambda b, c: (row(b, c), COL_B // D_BC)),
            pl.BlockSpec((q, D_BC), lambda b, c: (row(b, c), COL_C // D_BC)),
            pl.BlockSpec((q, LANE), lambda b, c: (row(b, c), 0)),
            pl.BlockSpec((1, CONV_W - 1, D_CONV), lambda b, c: (b_off + b, 0, 0)),
            pl.BlockSpec((1, N_SSM_HEADS, SSM_HEAD_DIM, D_STATE), lambda b, c: (b_off + b, 0, 0, 0)),
            pl.BlockSpec((CONV_W, D_CONV), lambda b, c: (0, 0)),
            pl.BlockSpec((1, D_CONV), lambda b, c: (0, 0)),
            pl.BlockSpec((1, LANE), lambda b, c: (0, 0)),
            pl.BlockSpec((1, LANE), lambda b, c: (0, 0)),
            pl.BlockSpec((1, D_SSM), lambda b, c: (0, 0)),
            pl.BlockSpec((1, D_SSM), lambda b, c: (0, 0)),
            pl.BlockSpec((2 * LANE, D_SSM), lambda b, c: (0, 0)),
            pl.BlockSpec(((CONV_W - 1) * q, 2 * HIST_ROWS + q), lambda b, c: (0, 0)),
        ],
        out_specs=[
            pl.BlockSpec((q, D_SSM), lambda b, c: (row(b, c), 0)),
            pl.BlockSpec((1, CONV_W - 1, D_CONV), lambda b, c: (b, 0, 0)),
            pl.BlockSpec((1, N_SSM_HEADS, SSM_HEAD_DIM, D_STATE), lambda b, c: (b, 0, 0, 0)),
        ],
        out_shape=[
            jax.ShapeDtypeStruct((m, D_MODEL), BF16),
            jax.ShapeDtypeStruct((bsz, CONV_W - 1, D_CONV), F32),
            jax.ShapeDtypeStruct((bsz, N_SSM_HEADS, SSM_HEAD_DIM, D_STATE), F32),
        ],
        scratch_shapes=[
            pltpu.VMEM((2 * HIST_ROWS + q, D_CONV), BF16),
            pltpu.VMEM((N_SSM_GROUPS, D_STATE, D_GROUP), F32),
            pltpu.VMEM((q, D_SSM), F32),
        ],
        compiler_params=_params(2),
        name="ssd",
    )(main, main, main, main, dt, conv_hist, h0,
      lp["conv_w"], lp["conv_b"], lp["dt_bias"], lp["a_log"], lp["d_skip_e"], lp["ssm_norm_w"],
      lp["expand"], _conv_shift_matrix(q))


def _lambda(lq1, lk1, lq2, lk2, lambda_init):
    return (jnp.exp(jnp.sum(lq1[...] * lk1[...], axis=-1, keepdims=True))
            - jnp.exp(jnp.sum(lq2[...] * lk2[...], axis=-1, keepdims=True)) + lambda_init)


def _diff_finish(o1, o2, lam, sw_ref, g, lambda_init):
    o = o1 - lam * o2
    on = (o * _rms_scale(o, SUBLN_EPS) * sw_ref[...]) * (1.0 - lambda_init)
    return (on * _silu(g.astype(F32))).astype(BF16)


ATTN_QT = 256


def _attn_scores(k_ref, q_ref, mp, nt, diagonal):
    cols = slice(mp * DIFF_HEAD_DIM, (mp + 1) * DIFF_HEAD_DIM)
    qs = slice(nt * ATTN_QT, (nt + 1) * ATTN_QT)
    kw = (nt + 1) * ATTN_QT if diagonal else k_ref.shape[0]
    s = lax.dot_general(k_ref[0:kw, cols], q_ref[qs, cols], (((1,), (1,)), ((), ())),
                        preferred_element_type=F32)
    if diagonal:
        r = lax.broadcasted_iota(jnp.int32, (ATTN_QT, ATTN_QT), 0)
        q = lax.broadcasted_iota(jnp.int32, (ATTN_QT, ATTN_QT), 1)
        tail = jnp.where(_chunk_of(r) <= _chunk_of(q), s[kw - ATTN_QT:kw], -jnp.inf)
        s = tail if kw == ATTN_QT else jnp.concatenate([s[0:kw - ATTN_QT], tail], axis=0)
    return s


def _attn_softmax(s, m_scr, l_scr, mp, nt):
    c = (DIFF_HEAD_DIM ** -0.5) * math.log2(math.e)
    qs = slice(nt * ATTN_QT, (nt + 1) * ATTN_QT)
    m_prev = m_scr[mp, :, qs]
    m_new = jnp.maximum(m_prev, jnp.max(s, axis=0, keepdims=True))
    p = jnp.exp2((s - m_new) * c)
    alpha = jnp.exp2((m_prev - m_new) * c)
    l_scr[mp, :, qs] = alpha * l_scr[mp, :, qs] + jnp.sum(p, axis=0, keepdims=True)
    m_scr[mp, :, qs] = m_new
    return p.astype(BF16), alpha


def _attn_value(vt, p, alpha, acc_scr, mp, nt):
    qs = slice(nt * ATTN_QT, (nt + 1) * ATTN_QT)
    kw = p.shape[0]
    acc_scr[mp, :, qs] = alpha * acc_scr[mp, :, qs] + jnp.dot(vt[:, 0:kw], p, preferred_element_type=F32)


def _diff_attn_kernel(qi_ref, ki_ref, q_ref, k_ref, v_ref, g_ref, lq1, lk1, lq2, lk2, sw_ref, y_hbm,
                      o_ref, m_scr, l_scr, acc_scr, *, tq, tk, lambda_init):
    del y_hbm
    qi = qi_ref[pl.program_id(2)]
    ki = ki_ref[pl.program_id(2)]
    assert tq == tk and tq % ATTN_QT == 0 and ATTN_QT % CHUNK == 0

    @pl.when(ki == 0)
    def _():
        m_scr[...] = jnp.full(m_scr.shape, -jnp.inf, F32)
        l_scr[...] = jnp.zeros(l_scr.shape, F32)
        acc_scr[...] = jnp.zeros(acc_scr.shape, F32)

    def block(diagonal):
        vt = v_ref[...].T
        chains = [(mp, nt) for mp in range(2) for nt in range(tq // ATTN_QT)]
        scores = [_attn_scores(k_ref, q_ref, mp, nt, diagonal) for mp, nt in chains]
        probs = [_attn_softmax(s, m_scr, l_scr, mp, nt) for s, (mp, nt) in zip(scores, chains)]
        for (p, alpha), (mp, nt) in zip(probs, chains):
            _attn_value(vt, p, alpha, acc_scr, mp, nt)

    @pl.when(ki < qi)
    def _():
        block(False)

    @pl.when(ki == qi)
    def _():
        block(True)
        lam = _lambda(lq1, lk1, lq2, lk2, lambda_init)
        o1 = (acc_scr[0] / l_scr[0]).T
        o2 = (acc_scr[1] / l_scr[1]).T
        o_ref[...] = _diff_finish(o1, o2, lam, sw_ref, g_ref[...], lambda_init)


def _diff_attn_prompt(main, y, lp, bsz, length, lambda_init, tq):
    tk = tq
    nq = length // tq
    m = bsz * length
    pairs = [(qi, ki) for qi in range(nq) for ki in range(qi + 1)]
    qi_tab = jnp.asarray([p[0] for p in pairs], jnp.int32)
    ki_tab = jnp.asarray([p[1] for p in pairs], jnp.int32)
    vec = pl.BlockSpec((1, DIFF_HEAD_DIM), lambda b, h, t, qt, kt: (0, 0))
    return pl.pallas_call(
        functools.partial(_diff_attn_kernel, tq=tq, tk=tk, lambda_init=lambda_init),
        grid_spec=pltpu.PrefetchScalarGridSpec(
            num_scalar_prefetch=2,
            grid=(bsz, N_DIFF_HEADS, len(pairs)),
            in_specs=[
                pl.BlockSpec((tq, HEAD_W), lambda b, h, t, qt, kt: (b * nq + qt[t], COL_QD // HEAD_W + h)),
                pl.BlockSpec((tk, HEAD_W), lambda b, h, t, qt, kt: (b * nq + kt[t], COL_KD // HEAD_W + h)),
                pl.BlockSpec((tk, HEAD_W), lambda b, h, t, qt, kt: (b * nq + kt[t], COL_VD // HEAD_W + h)),
                pl.BlockSpec((tq, HEAD_W), lambda b, h, t, qt, kt: (b * nq + qt[t], COL_GD // HEAD_W + h)),
                vec, vec, vec, vec,
                pl.BlockSpec((1, HEAD_W), lambda b, h, t, qt, kt: (0, 0)),
                pl.BlockSpec(memory_space=pl.ANY),
            ],
            out_specs=pl.BlockSpec((tq, HEAD_W),
                                   lambda b, h, t, qt, kt: (b * nq + qt[t], D_SSM // HEAD_W + h)),
            scratch_shapes=[
                pltpu.VMEM((2, 1, tq), F32),
                pltpu.VMEM((2, 1, tq), F32),
                pltpu.VMEM((2, HEAD_W, tq), F32),
            ],
        ),
        out_shape=jax.ShapeDtypeStruct((m, D_MODEL), BF16),
        input_output_aliases={11: 0},
        compiler_params=_params(3),
        name="diff_attn_prompt",
    )(qi_tab, ki_tab, main, main, main, main, lp["lambda_q1"], lp["lambda_k1"], lp["lambda_q2"],
      lp["lambda_k2"], lp["subln_w"], y)


def _diff_attn_decode_kernel(q_ref, kn_ref, vn_ref, g_ref, kp_ref, vp_ref, lq1, lk1, lq2, lk2, sw_ref,
                             y_hbm, o_ref, *, past, lambda_init):
    del y_hbm
    scale = DIFF_HEAD_DIM ** -0.5
    lq = q_ref.shape[0]
    nt_dims = (((1,), (1,)), ((), ()))
    mask_p = (_chunk_of(lax.broadcasted_iota(jnp.int32, (lq, past), 1))
              <= _chunk_of(past + lax.broadcasted_iota(jnp.int32, (lq, past), 0)))
    mask_n = (_chunk_of(past + lax.broadcasted_iota(jnp.int32, (lq, lq), 1))
              <= _chunk_of(past + lax.broadcasted_iota(jnp.int32, (lq, lq), 0)))
    lam = _lambda(lq1, lk1, lq2, lk2, lambda_init)
    for h in range(N_DIFF_HEADS):
        hs = slice(h * HEAD_W, (h + 1) * HEAD_W)
        vp = _load_flat(vp_ref, past, h * HEAD_W, HEAD_W, halves_split=True).astype(BF16)
        vn = vn_ref[:, hs]
        outs = []
        for mp in range(2):
            c0 = h * HEAD_W + mp * DIFF_HEAD_DIM
            qm = q_ref[:, c0:c0 + DIFF_HEAD_DIM]
            kp = _load_flat(kp_ref, past, c0, DIFF_HEAD_DIM, halves_split=False).astype(BF16)
            sp = lax.dot_general(qm, kp, nt_dims, preferred_element_type=F32) * scale
            sn = lax.dot_general(qm, kn_ref[:, c0:c0 + DIFF_HEAD_DIM], nt_dims,
                                 preferred_element_type=F32) * scale
            sp = jnp.where(mask_p, sp, -jnp.inf)
            sn = jnp.where(mask_n, sn, -jnp.inf)
            mx = jnp.maximum(jnp.max(sp, axis=-1, keepdims=True), jnp.max(sn, axis=-1, keepdims=True))
            pp = jnp.exp(sp - mx)
            pn = jnp.exp(sn - mx)
            denom = jnp.sum(pp, axis=-1, keepdims=True) + jnp.sum(pn, axis=-1, keepdims=True)
            o = (jnp.dot(pp.astype(BF16), vp, preferred_element_type=F32)
                 + jnp.dot(pn.astype(BF16), vn, preferred_element_type=F32))
            outs.append(o / denom)
        o_ref[:, hs] = _diff_finish(outs[0], outs[1], lam, sw_ref, g_ref[:, hs], lambda_init)


def _diff_attn_decode(main, y, k_past, v_past, b_off, lp, bsz, length, past, lambda_init):
    m = bsz * length
    vec = pl.BlockSpec((1, DIFF_HEAD_DIM), lambda b: (0, 0))
    return pl.pallas_call(
        functools.partial(_diff_attn_decode_kernel, past=past, lambda_init=lambda_init),
        grid=(bsz,),
        in_specs=[
            pl.BlockSpec((length, D_DIFF), lambda b: (b, COL_QD // D_DIFF)),
            pl.BlockSpec((length, D_DIFF), lambda b: (b, COL_KD // D_DIFF)),
            pl.BlockSpec((length, D_DIFF), lambda b: (b, COL_VD // D_DIFF)),
            pl.BlockSpec((length, D_DIFF), lambda b: (b, COL_GD // D_DIFF)),
            pl.BlockSpec((past * ROW_TILES, LANE), lambda b: (b_off + b, 0)),
            pl.BlockSpec((past * ROW_TILES, LANE), lambda b: (b_off + b, 0)),
            vec, vec, vec, vec,
            pl.BlockSpec((1, HEAD_W), lambda b: (0, 0)),
            pl.BlockSpec(memory_space=pl.ANY),
        ],
        out_specs=pl.BlockSpec((length, D_DIFF), lambda b: (b, D_SSM // D_DIFF)),
        out_shape=jax.ShapeDtypeStruct((m, D_MODEL), BF16),
        input_output_aliases={11: 0},
        compiler_params=_params(1),
        name="diff_attn_decode",
    )(main, main, main, main, k_past, v_past,
      lp["lambda_q1"], lp["lambda_k1"], lp["lambda_q2"], lp["lambda_k2"], lp["subln_w"], y)


def _mem_attn_kernel(q_ref, g_ref, mk_ref, mv_ref, y_hbm, o_ref, *, n_mem):
    del y_hbm
    scale = MEM_HEAD_DIM ** -0.5
    for h in range(N_MEM_HEADS):
        hs = slice(h * MEM_HEAD_DIM, (h + 1) * MEM_HEAD_DIM)
        mk = _load_flat(mk_ref, n_mem, h * MEM_HEAD_DIM, MEM_HEAD_DIM, halves_split=True).astype(BF16)
        mv = _load_flat(mv_ref, n_mem, h * MEM_HEAD_DIM, MEM_HEAD_DIM, halves_split=True).astype(BF16)
        s = lax.dot_general(q_ref[:, hs], mk, (((1,), (1,)), ((), ())), preferred_element_type=F32) * scale
        p = jnp.exp(s - jnp.max(s, axis=-1, keepdims=True))
        pr = p / jnp.sum(p, axis=-1, keepdims=True)
        o = jnp.dot(pr.astype(BF16), mv, preferred_element_type=F32)
        o_ref[:, hs] = (o * _silu(g_ref[:, hs].astype(F32))).astype(BF16)


def _mem_attn(main, y, mk, mv, b_off, bsz, length, n_mem, tq):
    nq = length // tq
    m = bsz * length
    return pl.pallas_call(
        functools.partial(_mem_attn_kernel, n_mem=n_mem),
        grid=(bsz, nq),
        in_specs=[
            pl.BlockSpec((tq, D_MEM), lambda b, qi: (b * nq + qi, COL_QM // D_MEM)),
            pl.BlockSpec((tq, D_MEM), lambda b, qi: (b * nq + qi, COL_GM // D_MEM)),
            pl.BlockSpec((n_mem * ROW_TILES, LANE), lambda b, qi: (b_off + b, 0)),
            pl.BlockSpec((n_mem * ROW_TILES, LANE), lambda b, qi: (b_off + b, 0)),
            pl.BlockSpec(memory_space=pl.ANY),
        ],
        out_specs=pl.BlockSpec((tq, D_MEM), lambda b, qi: (b * nq + qi, (D_SSM + D_DIFF) // D_MEM)),
        out_shape=jax.ShapeDtypeStruct((m, D_MODEL), BF16),
        input_output_aliases={4: 0},
        compiler_params=_params(2),
        name="mem_attn",
    )(main, main, mk, mv, y)


def _out_proj_kernel(y_ref, w_ref, x_ref, nw_ref, o_ref):
    n_tiles = w_ref.shape[0]
    tn = w_ref.shape[2]
    y = y_ref[...]
    for t in range(n_tiles):
        o_ref[:, t * tn:(t + 1) * tn] = jnp.dot(y, w_ref[t], preferred_element_type=F32)
    tm = o_ref.shape[0]
    step = min(NORM_ROWS, tm)
    for r in range(0, tm, step):
        o = o_ref[r:r + step, :]
        o_ref[r:r + step, :] = x_ref[r:r + step, :] + o * _rms_scale(o, NORM_EPS) * nw_ref[...]


def _out_proj(y, w_out, x2d, norm_w, tm):
    m = x2d.shape[0]
    return pl.pallas_call(
        _out_proj_kernel,
        grid=(m // tm,),
        in_specs=[
            pl.BlockSpec((tm, D_MODEL), lambda i: (i, 0)),
            pl.BlockSpec(w_out.shape, lambda i: (0, 0, 0), pipeline_mode=pl.Buffered(1)),
            pl.BlockSpec((tm, D_MODEL), lambda i: (i, 0)),
            pl.BlockSpec((1, D_MODEL), lambda i: (0, 0)),
        ],
        out_specs=pl.BlockSpec((tm, D_MODEL), lambda i: (i, 0)),
        out_shape=jax.ShapeDtypeStruct((m, D_MODEL), F32),
        compiler_params=_params(1, OUT_PROJ_VMEM_LIMIT),
        name="out_proj",
    )(y, w_out, x2d, norm_w)


def _mixer_layer(x2d, bsz, length, layer, depth, kv_all, b_off, mem_off, conv_hist, h0, kv_past, past, mk,
                 mv, n_mem, lp, lambda_init, tiles):
    main, k32, v32, dt = _in_proj(x2d, lp["norm_pre_w"], lp["w_main"], lp["w_dt"], tiles["proj_tm"],
                                  layer, depth, kv_all)
    y, new_conv, h_new = _ssd(main, dt, conv_hist, h0, b_off, lp, bsz, length, tiles["ssd_q"])
    if kv_past is None:
        y = _diff_attn_prompt(main, y, lp, bsz, length, lambda_init, tiles["attn_tq"])
    else:
        y = _diff_attn_decode(main, y, kv_past[0], kv_past[1], b_off, lp, bsz, length, past, lambda_init)
    y = _mem_attn(main, y, mk, mv, mem_off, bsz, length, n_mem, tiles["mem_tq"])
    x_new = _out_proj(y, lp["w_out"], x2d, lp["norm_post_w"], tiles["out_tm"])
    return x_new, new_conv, h_new, (k32, v32)


def _layer_params(l, norm_pre_w, norm_post_w, w_in_t, conv_w, conv_b, dt_bias, a_log, d_skip,
                  ssm_norm_w, lambda_q1, lambda_k1, lambda_q2, lambda_k2, subln_w, mem_norm_w,
                  w_mem_kv, w_out):
    w_main, w_dt = _w_prep(w_in_t, l)
    pad_heads = lambda v: jnp.pad(v, (0, LANE - N_SSM_HEADS)).reshape(1, LANE)
    head_of_channel = jnp.arange(D_SSM) // SSM_HEAD_DIM
    expand = (jnp.arange(LANE)[:, None] == head_of_channel[None, :]).astype(BF16)
    expand = jnp.concatenate([expand, expand], axis=0)
    return {
        "norm_pre_w": norm_pre_w[l].reshape(1, D_MODEL),
        "norm_post_w": norm_post_w[l].reshape(1, D_MODEL),
        "w_main": w_main,
        "w_dt": w_dt,
        "conv_w": conv_w[l],
        "conv_b": conv_b[l].reshape(1, D_CONV),
        "dt_bias": pad_heads(dt_bias[l]),
        "a_log": pad_heads(a_log[l]),
        "d_skip_e": jnp.repeat(d_skip[l], SSM_HEAD_DIM).reshape(1, D_SSM),
        "ssm_norm_w": ssm_norm_w[l].reshape(1, D_SSM),
        "expand": expand,
        "lambda_q1": lambda_q1[l].reshape(1, DIFF_HEAD_DIM),
        "lambda_k1": lambda_k1[l].reshape(1, DIFF_HEAD_DIM),
        "lambda_q2": lambda_q2[l].reshape(1, DIFF_HEAD_DIM),
        "lambda_k2": lambda_k2[l].reshape(1, DIFF_HEAD_DIM),
        "subln_w": subln_w[l].reshape(1, HEAD_W),
        "mem_norm_w": mem_norm_w[l].reshape(1, D_MODEL),
        "w_mem_kv": _layer_bf16_tiles(w_mem_kv, l, D_MEM),
        "w_out": _layer_bf16_tiles(w_out, l, OUT_TN),
    }


OUT_TN = 512


def _tiles(bsz, length):
    m = bsz * length
    return {
        "proj_tm": min(512, m),
        "ssd_q": min(128, length),
        "attn_tq": min(1024, length),
        "mem_tq": min(1024, length),
        "out_tm": min(256, m),
    }


def kernel(x_prompt, x_sample, mem_prompt, cache_conv, state_ssm, cache_k, cache_v, cache_mem_k,
           cache_mem_v, norm_pre_w, norm_post_w, w_in, conv_w, conv_b, dt_bias, a_log, d_skip,
           ssm_norm_w, lambda_q1, lambda_k1, lambda_q2, lambda_k2, subln_w, mem_norm_w, w_mem_kv,
           w_out):
    depth = w_in.shape[0]
    bp, lp_, _ = x_prompt.shape
    bs, ls, _ = x_sample.shape
    n_mem = mem_prompt.shape[1]
    past = cache_k.shape[2]
    xp = x_prompt.reshape(bp * lp_, D_MODEL)
    xs = x_sample.reshape(bs * ls, D_MODEL)
    mem2d = mem_prompt.reshape(bp * n_mem, D_MODEL)
    tiles_p = _tiles(bp, lp_)
    tiles_s = _tiles(bs, ls)
    conv0 = jnp.zeros((bp, CONV_W - 1, D_CONV), F32)
    h_zero = jnp.zeros((bp, N_SSM_HEADS, SSM_HEAD_DIM, D_STATE), F32)
    conv_all = cache_conv.reshape(depth * bs, CONV_W - 1, D_CONV)
    ssm_all = state_ssm.reshape(depth * bs, N_SSM_HEADS, SSM_HEAD_DIM, D_STATE)
    kv_all = (cache_k.reshape(-1, LANE), _to_flat_halves(cache_v))
    mem_k_all = _to_flat_halves(cache_mem_k)
    mem_v_all = _to_flat_halves(cache_mem_v)
    outs = [[] for _ in range(6)]
    kv_p = kv_s = None
    w_in_t = jnp.swapaxes(w_in, 1, 2)
    for l in range(depth):
        lp = _layer_params(l, norm_pre_w, norm_post_w, w_in_t, conv_w, conv_b, dt_bias, a_log, d_skip,
                           ssm_norm_w, lambda_q1, lambda_k1, lambda_q2, lambda_k2, subln_w,
                           mem_norm_w, w_mem_kv, w_out)
        lambda_init = 0.8 - 0.6 * math.exp(-0.3 * l)
        mk_p, mv_p = _mem_kv(mem2d, lp["mem_norm_w"], lp["w_mem_kv"], min(512, bp * n_mem))
        xp, c_p, h_p, kv_p = _mixer_layer(xp, bp, lp_, l, depth, kv_p, 0, 0, conv0, h_zero, None, 0, mk_p,
                                          mv_p, n_mem, lp, lambda_init, tiles_p)
        xs, c_s, h_s, kv_s = _mixer_layer(xs, bs, ls, l, depth, kv_s, l * bs, l * bs, conv_all, ssm_all,
                                          kv_all, past, mem_k_all, mem_v_all, n_mem, lp, lambda_init,
                                          tiles_s)
        for lst, val in zip(outs, (c_p, h_p, mk_p, mv_p, c_s, h_s)):
            lst.append(val)
    st = [jnp.stack(o) for o in outs]
    return (
        xp.reshape(bp, lp_, D_MODEL),
        xs.reshape(bs, ls, D_MODEL),
        st[0],
        st[1],
        kv_p[0].reshape(depth, bp, lp_, N_DIFF_HEADS, 2, DIFF_HEAD_DIM),
        _from_flat_halves(kv_p[1], (depth, bp, lp_, N_DIFF_HEADS, 2 * DIFF_HEAD_DIM)),
        _from_flat_halves(st[2], (depth, bp, n_mem, N_MEM_HEADS, MEM_HEAD_DIM)),
        _from_flat_halves(st[3], (depth, bp, n_mem, N_MEM_HEADS, MEM_HEAD_DIM)),
        st[4],
        st[5],
        kv_s[0].reshape(depth, bs, ls, N_DIFF_HEADS, 2, DIFF_HEAD_DIM),
        _from_flat_halves(kv_s[1], (depth, bs, ls, N_DIFF_HEADS, 2 * DIFF_HEAD_DIM)),
    )
```

```python
import functools
import math

import jax
import jax.numpy as jnp
from jax import lax
from jax.experimental import pallas as pl
from jax.experimental.pallas import tpu as pltpu

F32 = jnp.float32
BF16 = jnp.bfloat16

D_MODEL = 4096
D_SSM = 2048
SSM_HEAD_DIM = 64
N_SSM_HEADS = 32
N_SSM_GROUPS = 4
HEADS_PER_GROUP = N_SSM_HEADS // N_SSM_GROUPS
D_GROUP = D_SSM // N_SSM_GROUPS
D_STATE = 128
CONV_W = 4
D_BC = N_SSM_GROUPS * D_STATE
D_CONV = D_SSM + 2 * D_BC
D_DIFF = 1024
N_DIFF_HEADS = 4
DIFF_HEAD_DIM = 128
D_MEM = 1024
N_MEM_HEADS = 4
MEM_HEAD_DIM = 256
CHUNK = 64
NORM_EPS = 1e-6
SUBLN_EPS = 1e-5

LANE = 128
SUBLANE = 8
VMEM_LIMIT = 56 * 1024 * 1024
OUT_PROJ_VMEM_LIMIT = 60 * 1024 * 1024

HEAD_W = 2 * DIFF_HEAD_DIM
COL_KD = 0
COL_VD = COL_KD + D_DIFF
COL_Z = COL_VD + D_DIFF
COL_X = COL_Z + D_SSM
COL_B = COL_X + D_SSM
COL_C = COL_B + D_BC
COL_QD = COL_C + D_BC
COL_GD = COL_QD + D_DIFF
COL_QM = COL_GD + D_DIFF
COL_GM = COL_QM + D_MEM
D_MAIN = COL_GM + D_MEM
PROJ_TN = 1024
K_TILE = COL_KD // PROJ_TN
V_TILE = COL_VD // PROJ_TN


def _params(n_axes, vmem_limit=VMEM_LIMIT):
    return pltpu.CompilerParams(dimension_semantics=("arbitrary",) * n_axes,
                                vmem_limit_bytes=vmem_limit)


def _silu(x):
    return x * jax.nn.sigmoid(x)


def _chunk_of(pos):
    assert CHUNK & (CHUNK - 1) == 0
    return lax.shift_right_logical(pos, jnp.int32(CHUNK.bit_length() - 1))


def _rms_scale(x, eps):
    return lax.rsqrt(jnp.mean(x * x, axis=-1, keepdims=True) + eps)


ROW_TILES = 1024 // LANE
N_HEADS_1024 = 1024 // HEAD_W


def _flat_row(c, halves_split):
    head, half = divmod(c, 2)
    return half * N_HEADS_1024 + head if halves_split else c


def _store_flat(ref, val, halves_split):
    rows = val.shape[0]
    for c in range(ROW_TILES):
        ref[pl.ds(_flat_row(c, halves_split), rows, stride=ROW_TILES), :] = val[:, c * LANE:(c + 1) * LANE]


def _load_flat(ref, rows, col0, width, halves_split):
    parts = [ref[pl.ds(_flat_row(col0 // LANE + c, halves_split), rows, stride=ROW_TILES), :]
             for c in range(width // LANE)]
    return parts[0] if len(parts) == 1 else jnp.concatenate(parts, axis=1)


def _to_flat_halves(a):
    return a.reshape(-1, N_HEADS_1024, 2, LANE).transpose(0, 2, 1, 3).reshape(-1, LANE)


def _from_flat_halves(flat, shape):
    return flat.reshape(-1, 2, N_HEADS_1024, LANE).transpose(0, 2, 1, 3).reshape(shape)


NORM_ROWS = 128


def _norm_rows(x_ref, nw_ref, h_scr):
    tm = x_ref.shape[0]
    step = min(NORM_ROWS, tm)
    for r in range(0, tm, step):
        x = x_ref[r:r + step, :]
        h_scr[r:r + step, :] = (x * _rms_scale(x, NORM_EPS) * nw_ref[...]).astype(BF16)


O_DT = D_SSM + D_CONV
O_Q = O_DT + N_SSM_HEADS
MAIN_PIECES = ((COL_Z, 0, O_DT), (COL_QD, O_Q, D_DIFF), (COL_KD, O_Q + D_DIFF, D_DIFF),
               (COL_VD, O_Q + 2 * D_DIFF, D_DIFF), (COL_GD, O_Q + 3 * D_DIFF, D_DIFF),
               (COL_QM, O_Q + 4 * D_DIFF, D_MEM), (COL_GM, O_Q + 4 * D_DIFF + D_MEM, D_MEM))
PREP_COLS = 256
PREP_PER_TILE = PROJ_TN // PREP_COLS


def _prep_source_columns():
    src = []
    for dst, first, width in sorted(MAIN_PIECES):
        assert dst == len(src) * PREP_COLS
        src += list(range(first, first + width, PREP_COLS))
    assert len(src) * PREP_COLS == D_MAIN
    return jnp.asarray(src, jnp.int32)


def _w_prep_kernel(src_ref, w_ref, wdt_ref, main_ref, dt_ref):
    del src_ref
    main_ref[...] = w_ref[0].T.astype(BF16)

    @pl.when(pl.program_id(0) == 0)
    def _():
        lane = lax.broadcasted_iota(jnp.int32, (1, LANE), 1)
        dt_ref[...] = jnp.where(lane < N_SSM_HEADS, wdt_ref[...].T, 0.0).astype(BF16)


def _w_prep(w_in_t, layer):
    return pl.pallas_call(
        _w_prep_kernel,
        grid_spec=pltpu.PrefetchScalarGridSpec(
            num_scalar_prefetch=1,
            grid=(D_MAIN // PREP_COLS,),
            in_specs=[
                pl.BlockSpec((pl.Element(1), pl.Element(PREP_COLS), pl.Element(D_MODEL)),
                             lambda j, src: (layer, pl.multiple_of(src[j], N_SSM_HEADS), 0)),
                pl.BlockSpec((None, LANE, D_MODEL), lambda j, src: (layer, O_DT // LANE, 0)),
            ],
            out_specs=[
                pl.BlockSpec((None, D_MODEL, PREP_COLS),
                             lambda j, src: (j // PREP_PER_TILE, 0, j % PREP_PER_TILE)),
                pl.BlockSpec((D_MODEL, LANE), lambda j, src: (0, 0)),
            ],
        ),
        out_shape=[
            jax.ShapeDtypeStruct((D_MAIN // PROJ_TN, D_MODEL, PROJ_TN), BF16),
            jax.ShapeDtypeStruct((D_MODEL, LANE), BF16),
        ],
        compiler_params=_params(1),
        name="w_prep",
    )(_prep_source_columns(), w_in_t, w_in_t)


CAST_ROWS = 1024


def _cast_kernel(w_ref, o_ref):
    o_ref[...] = w_ref[...].astype(BF16)


def _layer_bf16_tiles(w, layer, tn):
    _, rows, cols = w.shape
    return pl.pallas_call(
        _cast_kernel,
        grid=(rows // CAST_ROWS, cols // tn),
        in_specs=[pl.BlockSpec((None, CAST_ROWS, tn), lambda i, t: (layer, i, t))],
        out_specs=pl.BlockSpec((None, CAST_ROWS, tn), lambda i, t: (t, i, 0)),
        out_shape=jax.ShapeDtypeStruct((cols // tn, rows, tn), BF16),
        compiler_params=_params(2),
        name="cast_bf16",
    )(w)


def _in_proj_kernel(x_ref, nw_ref, w_ref, wdt_ref, *refs):
    main_ref, k_ref, v_ref, dt_ref, h_scr = refs[-5:]
    j = pl.program_id(1)

    @pl.when(j == 0)
    def _():
        _norm_rows(x_ref, nw_ref, h_scr)
        dt_ref[...] = jnp.dot(h_scr[...], wdt_ref[...], preferred_element_type=F32)

    acc = jnp.dot(h_scr[...], w_ref[...], preferred_element_type=F32)
    main_ref[...] = acc.astype(BF16)

    @pl.when(j == K_TILE)
    def _():
        _store_flat(k_ref, acc, halves_split=False)

    @pl.when(j == V_TILE)
    def _():
        _store_flat(v_ref, acc, halves_split=True)


def _in_proj(x2d, norm_w, w_main, w_dt, tm, layer, depth, kv_all):
    m = x2d.shape[0]
    ni = m // tm
    grid = (ni, D_MAIN // PROJ_TN)
    kv_spec = pl.BlockSpec((tm * ROW_TILES, LANE), lambda i, j: (layer * ni + i, 0))
    kv_shape = jax.ShapeDtypeStruct((depth * m * ROW_TILES, LANE), F32)
    in_specs = [
        pl.BlockSpec((tm, D_MODEL), lambda i, j: (i, 0)),
        pl.BlockSpec((1, D_MODEL), lambda i, j: (0, 0)),
        pl.BlockSpec((None, D_MODEL, PROJ_TN), lambda i, j: (j, 0, 0)),
        pl.BlockSpec((D_MODEL, LANE), lambda i, j: (0, 0)),
    ]
    operands = [x2d, norm_w, w_main, w_dt]
    aliases = {}
    if kv_all is not None:
        in_specs += [pl.BlockSpec(memory_space=pl.ANY)] * 2
        aliases = {len(operands): 1, len(operands) + 1: 2}
        operands += list(kv_all)
    return pl.pallas_call(
        _in_proj_kernel,
        grid=grid,
        in_specs=in_specs,
        out_specs=[
            pl.BlockSpec((tm, PROJ_TN), lambda i, j: (i, j)),
            kv_spec,
            kv_spec,
            pl.BlockSpec((tm, LANE), lambda i, j: (i, 0)),
        ],
        out_shape=[
            jax.ShapeDtypeStruct((m, D_MAIN), BF16),
            kv_shape,
            kv_shape,
            jax.ShapeDtypeStruct((m, LANE), F32),
        ],
        scratch_shapes=[pltpu.VMEM((tm, D_MODEL), BF16)],
        input_output_aliases=aliases,
        compiler_params=_params(2),
        name="in_proj",
    )(*operands)


def _mem_kv_kernel(x_ref, nw_ref, w_ref, mk_ref, mv_ref, h_scr):
    j = pl.program_id(1)

    @pl.when(j == 0)
    def _():
        _norm_rows(x_ref, nw_ref, h_scr)

    acc = jnp.dot(h_scr[...], w_ref[...], preferred_element_type=F32)

    @pl.when(j == 0)
    def _():
        _store_flat(mk_ref, acc, halves_split=True)

    @pl.when(j == 1)
    def _():
        _store_flat(mv_ref, acc, halves_split=True)


def _mem_kv(mem2d, norm_w, w_kv, tm):
    m = mem2d.shape[0]
    return pl.pallas_call(
        _mem_kv_kernel,
        grid=(m // tm, 2),
        in_specs=[
            pl.BlockSpec((tm, D_MODEL), lambda i, j: (i, 0)),
            pl.BlockSpec((1, D_MODEL), lambda i, j: (0, 0)),
            pl.BlockSpec((None, D_MODEL, D_MEM), lambda i, j: (j, 0, 0)),
        ],
        out_specs=[
            pl.BlockSpec((tm * ROW_TILES, LANE), lambda i, j: (i, 0)),
            pl.BlockSpec((tm * ROW_TILES, LANE), lambda i, j: (i, 0)),
        ],
        out_shape=[jax.ShapeDtypeStruct((m * ROW_TILES, LANE), F32)] * 2,
        scratch_shapes=[pltpu.VMEM((tm, D_MODEL), BF16)],
        compiler_params=_params(2),
        name="mem_kv",
    )(mem2d, norm_w, w_kv)


def _split_bf16(x, n):
    parts = []
    r = x
    for _ in range(n - 1):
        p = r.astype(BF16)
        parts.append(p)
        r = r - p.astype(F32)
    parts.append(r.astype(BF16))
    return parts


def _sum_rows(a, n, rows):
    out = a[(n - 1) * rows:n * rows]
    for i in range(n - 2, -1, -1):
        out = out + a[i * rows:(i + 1) * rows]
    return out


HIST_ROWS = 16


def _conv_shift_matrices(q):
    t = jnp.arange(q)[:, None]
    col = jnp.arange(2 * HIST_ROWS + q)[None, :]
    with_terms, plain = [], []
    for j in range(CONV_W - 1):
        src = t - (CONV_W - 1 - j)
        band = col == 2 * HIST_ROWS + src
        plain.append(band)
        with_terms.append(band | ((src < 0) & (col == HIST_ROWS + src)))
    return jnp.stack([jnp.concatenate(with_terms, axis=0), jnp.concatenate(plain, axis=0)]).astype(BF16)


def _ssd_chunk(win_ref, shift, z_ref, dt_ref, y_ref, y_scr, st_scr,
               cw_ref, cb_ref, dtb_ref, alog_ref, dskip_ref, nw_ref, e_ref, q):
    taps = jnp.dot(shift, win_ref[...], preferred_element_type=F32)
    acc = cb_ref[...] + taps[0:q] * cw_ref[0:1, :]
    for j in range(1, CONV_W - 1):
        acc = acc + taps[j * q:(j + 1) * q] * cw_ref[j:j + 1, :]
    cur = win_ref[2 * HIST_ROWS:2 * HIST_ROWS + q, :].astype(F32)
    acc = acc + cur * cw_ref[CONV_W - 1:CONV_W, :]
    xbc = _silu(acc)
    xs = xbc[:, 0:D_SSM]
    xs_bf = xs.astype(BF16)
    bm = xbc[:, D_SSM:D_SSM + D_BC].astype(BF16)
    cm = xbc[:, D_SSM + D_BC:D_CONV].astype(BF16)

    dtv = jax.nn.softplus(dt_ref[...] + dtb_ref[...])
    a = -jnp.exp(alog_ref[...])
    da = dtv * a
    ri = lax.broadcasted_iota(jnp.int32, (q, q), 0)
    ci = lax.broadcasted_iota(jnp.int32, (q, q), 1)
    causal = ri >= ci
    tril = jnp.where(causal, 1.0, 0.0).astype(BF16)
    triu = jnp.where(ri <= ci, 1.0, 0.0).astype(BF16)
    eye = jnp.where(ri == ci, 1.0, 0.0).astype(BF16)
    da3 = jnp.concatenate(_split_bf16(da, 3), axis=1)
    dt3 = jnp.concatenate(_split_bf16(dtv, 3), axis=1)
    cum3 = jnp.dot(tril, da3, preferred_element_type=F32)
    cum = cum3[:, 2 * LANE:3 * LANE] + cum3[:, LANE:2 * LANE] + cum3[:, 0:LANE]
    tn_dims = (((0,), (0,)), ((), ()))
    cum_t = _sum_rows(lax.dot_general(da3, triu, tn_dims, preferred_element_type=F32), 3, LANE)
    dt_t = _sum_rows(lax.dot_general(dt3, eye, tn_dims, preferred_element_type=F32), 3, LANE)

    ecum = jnp.exp(cum)
    dd = dtv * jnp.exp(cum[q - 1:q, :] - cum)
    ex_in = jnp.concatenate([jnp.concatenate(_split_bf16(ecum, 2), axis=1),
                             jnp.concatenate(_split_bf16(dd, 2), axis=1)], axis=0)
    ex = jnp.dot(ex_in, e_ref[...], preferred_element_type=F32)
    ecum_e = ex[0:q]
    dd_e = ex[q:2 * q]
    xdd = (xs * dd_e).astype(BF16)

    lane = lax.broadcasted_iota(jnp.int32, (q, LANE), 1)
    lo_half = lane < SSM_HEAD_DIM
    nt_dims = (((1,), (1,)), ((), ()))
    for g in range(N_SSM_GROUPS):
        bg = bm[:, g * D_STATE:(g + 1) * D_STATE]
        cg = cm[:, g * D_STATE:(g + 1) * D_STATE]
        gs = slice(g * D_GROUP, (g + 1) * D_GROUP)
        cbg = lax.dot_general(cg, bg, nt_dims, preferred_element_type=F32)
        st = st_scr[g]
        y_off = jnp.dot(cg, st.astype(BF16), preferred_element_type=F32) * ecum_e[:, gs]
        st_scr[g] = st * ecum_e[q - 1:q, gs] + lax.dot_general(
            bg, xdd[:, gs], tn_dims, preferred_element_type=F32)
        for t in range(HEADS_PER_GROUP // 4):
            ws, xrows = [], []
            for k in range(4):
                h = g * HEADS_PER_GROUP + 4 * t + k
                seg = cum[:, h:h + 1] - cum_t[h:h + 1, :]
                lmat = jnp.exp(jnp.where(causal, seg, -jnp.inf))
                ws.append((cbg * lmat * dt_t[h:h + 1, :]).astype(BF16))
                pair = xs_bf[:, g * D_GROUP + (2 * t + k // 2) * LANE:g * D_GROUP + (2 * t + k // 2 + 1) * LANE]
                zero = jnp.zeros_like(pair)
                half = jnp.where(lo_half, pair, zero) if k % 2 == 0 else jnp.where(lo_half, zero, pair)
                xrows.append(jnp.concatenate([half, zero] if k < 2 else [zero, half], axis=1))
            cs = slice(g * D_GROUP + 2 * t * LANE, g * D_GROUP + (2 * t + 2) * LANE)
            y_diag = jnp.dot(jnp.concatenate(ws, axis=1), jnp.concatenate(xrows, axis=0),
                             preferred_element_type=F32)
            y_scr[:, cs] = (y_diag + y_off[:, 2 * t * LANE:(2 * t + 2) * LANE]) + xs[:, cs] * dskip_ref[:, cs]

    for g in range(N_SSM_GROUPS):
        gs = slice(g * D_GROUP, (g + 1) * D_GROUP)
        yg = y_scr[:, gs] * _silu(z_ref[:, gs].astype(F32))
        y_ref[:, gs] = (yg * _rms_scale(yg, NORM_EPS) * nw_ref[:, gs]).astype(BF16)
    return cur


def _ssd_kernel(z_ref, xx_ref, xb_ref, xc_ref, dt_ref, hist_ref, h0_ref,
                cw_ref, cb_ref, dtb_ref, alog_ref, dskip_ref, nw_ref, e_ref, shift_ref,
                y_ref, nconv_ref, hnew_ref, xp_scr, st_scr, y_scr, *, q, n_sub):
    c = pl.program_id(1)
    nc = pl.num_programs(1)
    rows = n_sub * q

    @pl.when(c == 0)
    def _():
        hist = jnp.concatenate(
            [jnp.zeros((HIST_ROWS - (CONV_W - 1), D_CONV), F32), hist_ref[0]], axis=0)
        hi, lo = _split_bf16(hist, 2)
        xp_scr[0:HIST_ROWS, :] = hi
        xp_scr[HIST_ROWS:2 * HIST_ROWS, :] = lo
        for g in range(N_SSM_GROUPS):
            hg = h0_ref[0, g * HEADS_PER_GROUP:(g + 1) * HEADS_PER_GROUP]
            st_scr[g] = hg.reshape(D_GROUP, D_STATE).T

    blk = slice(2 * HIST_ROWS, 2 * HIST_ROWS + rows)
    xp_scr[blk, 0:D_SSM] = xx_ref[...]
    xp_scr[blk, D_SSM:D_SSM + D_BC] = xb_ref[...]
    xp_scr[blk, D_SSM + D_BC:D_CONV] = xc_ref[...]
    for sub in range(n_sub):
        r = slice(sub * q, (sub + 1) * q)
        cur = _ssd_chunk(xp_scr.at[sub * q:sub * q + 2 * HIST_ROWS + q, :], shift_ref[min(sub, 1)],
                         z_ref.at[r, :], dt_ref.at[r, :], y_ref.at[r, :], y_scr.at[r, :], st_scr,
                         cw_ref, cb_ref, dtb_ref, alog_ref, dskip_ref, nw_ref, e_ref, q)

    @pl.when(c == nc - 1)
    def _():
        nconv_ref[0] = cur[q - (CONV_W - 1):q]
        for g in range(N_SSM_GROUPS):
            hnew_ref[0, g * HEADS_PER_GROUP:(g + 1) * HEADS_PER_GROUP] = (
                st_scr[g].T.reshape(HEADS_PER_GROUP, SSM_HEAD_DIM, D_STATE))

    xp_scr[0:HIST_ROWS, :] = xp_scr[2 * HIST_ROWS + rows - HIST_ROWS:2 * HIST_ROWS + rows, :]
    xp_scr[HIST_ROWS:2 * HIST_ROWS, :] = jnp.zeros((HIST_ROWS, D_CONV), BF16)


def _ssd(main, dt, conv_hist, h0, b_off, lp, bsz, length, q, n_sub):
    chunk, q = q, n_sub * q
    nc = length // q
    m = bsz * length
    row = lambda b, c: b * nc + c
    return pl.pallas_call(
        functools.partial(_ssd_kernel, q=chunk, n_sub=n_sub),
        grid=(bsz, nc),
        in_specs=[
            pl.BlockSpec((q, D_SSM), lambda b, c: (row(b, c), COL_Z // D_SSM)),
            pl.BlockSpec((q, D_SSM), lambda b, c: (row(b, c), COL_X // D_SSM)),
            pl.BlockSpec((q, D_BC), lambda b, c: (row(b, c), COL_B // D_BC)),
            pl.BlockSpec((q, D_BC), lambda b, c: (row(b, c), COL_C // D_BC)),
            pl.BlockSpec((q, LANE), lambda b, c: (row(b, c), 0)),
            pl.BlockSpec((1, CONV_W - 1, D_CONV), lambda b, c: (b_off + b, 0, 0)),
            pl.BlockSpec((1, N_SSM_HEADS, SSM_HEAD_DIM, D_STATE), lambda b, c: (b_off + b, 0, 0, 0)),
            pl.BlockSpec((CONV_W, D_CONV), lambda b, c: (0, 0)),
            pl.BlockSpec((1, D_CONV), lambda b, c: (0, 0)),
            pl.BlockSpec((1, LANE), lambda b, c: (0, 0)),
            pl.BlockSpec((1, LANE), lambda b, c: (0, 0)),
            pl.BlockSpec((1, D_SSM), lambda b, c: (0, 0)),
            pl.BlockSpec((1, D_SSM), lambda b, c: (0, 0)),
            pl.BlockSpec((2 * LANE, D_SSM), lambda b, c: (0, 0)),
            pl.BlockSpec((2, (CONV_W - 1) * chunk, 2 * HIST_ROWS + chunk), lambda b, c: (0, 0, 0)),
        ],
        out_specs=[
            pl.BlockSpec((q, D_SSM), lambda b, c: (row(b, c), 0)),
            pl.BlockSpec((1, CONV_W - 1, D_CONV), lambda b, c: (b, 0, 0)),
            pl.BlockSpec((1, N_SSM_HEADS, SSM_HEAD_DIM, D_STATE), lambda b, c: (b, 0, 0, 0)),
        ],
        out_shape=[
            jax.ShapeDtypeStruct((m, D_MODEL), BF16),
            jax.ShapeDtypeStruct((bsz, CONV_W - 1, D_CONV), F32),
            jax.ShapeDtypeStruct((bsz, N_SSM_HEADS, SSM_HEAD_DIM, D_STATE), F32),
        ],
        scratch_shapes=[
            pltpu.VMEM((2 * HIST_ROWS + q, D_CONV), BF16),
            pltpu.VMEM((N_SSM_GROUPS, D_STATE, D_GROUP), F32),
            pltpu.VMEM((q, D_SSM), F32),
        ],
        compiler_params=_params(2),
        name="ssd",
    )(main, main, main, main, dt, conv_hist, h0,
      lp["conv_w"], lp["conv_b"], lp["dt_bias"], lp["a_log"], lp["d_skip_e"], lp["ssm_norm_w"],
      lp["expand"], _conv_shift_matrices(chunk))


def _lambda(lq1, lk1, lq2, lk2, lambda_init):
    return (jnp.exp(jnp.sum(lq1[...] * lk1[...], axis=-1, keepdims=True))
            - jnp.exp(jnp.sum(lq2[...] * lk2[...], axis=-1, keepdims=True)) + lambda_init)


def _diff_finish(o1, o2, lam, sw_ref, g, lambda_init):
    o = o1 - lam * o2
    on = (o * _rms_scale(o, SUBLN_EPS) * sw_ref[...]) * (1.0 - lambda_init)
    return (on * _silu(g.astype(F32))).astype(BF16)


ATTN_QT = 256


def _attn_scores(k_ref, q_ref, mp, nt, diagonal):
    cols = slice(mp * DIFF_HEAD_DIM, (mp + 1) * DIFF_HEAD_DIM)
    qs = slice(nt * ATTN_QT, (nt + 1) * ATTN_QT)
    kw = (nt + 1) * ATTN_QT if diagonal else k_ref.shape[0]
    s = lax.dot_general(k_ref[0:kw, cols], q_ref[qs, cols], (((1,), (1,)), ((), ())),
                        preferred_element_type=F32)
    if diagonal:
        r = lax.broadcasted_iota(jnp.int32, (ATTN_QT, ATTN_QT), 0)
        q = lax.broadcasted_iota(jnp.int32, (ATTN_QT, ATTN_QT), 1)
        tail = jnp.where(_chunk_of(r) <= _chunk_of(q), s[kw - ATTN_QT:kw], -jnp.inf)
        s = tail if kw == ATTN_QT else jnp.concatenate([s[0:kw - ATTN_QT], tail], axis=0)
    return s


def _attn_softmax(s, m_scr, l_scr, mp, nt):
    c = (DIFF_HEAD_DIM ** -0.5) * math.log2(math.e)
    qs = slice(nt * ATTN_QT, (nt + 1) * ATTN_QT)
    m_prev = m_scr[mp, :, qs]
    m_new = jnp.maximum(m_prev, jnp.max(s, axis=0, keepdims=True))
    p = jnp.exp2((s - m_new) * c)
    alpha = jnp.exp2((m_prev - m_new) * c)
    l_scr[mp, :, qs] = alpha * l_scr[mp, :, qs] + jnp.sum(p, axis=0, keepdims=True)
    m_scr[mp, :, qs] = m_new
    return p.astype(BF16), alpha


def _attn_value(vt, p, alpha, acc_scr, mp, nt):
    qs = slice(nt * ATTN_QT, (nt + 1) * ATTN_QT)
    kw = p.shape[0]
    acc_scr[mp, :, qs] = alpha * acc_scr[mp, :, qs] + jnp.dot(vt[:, 0:kw], p, preferred_element_type=F32)


def _diff_attn_kernel(qi_ref, ki_ref, q_ref, k_ref, v_ref, g_ref, lq1, lk1, lq2, lk2, sw_ref, y_hbm,
                      o_ref, m_scr, l_scr, acc_scr, *, tq, tk, lambda_init):
    del y_hbm
    qi = qi_ref[pl.program_id(2)]
    ki = ki_ref[pl.program_id(2)]
    assert tq == tk and tq % ATTN_QT == 0 and ATTN_QT % CHUNK == 0

    @pl.when(ki == 0)
    def _():
        m_scr[...] = jnp.full(m_scr.shape, -jnp.inf, F32)
        l_scr[...] = jnp.zeros(l_scr.shape, F32)
        acc_scr[...] = jnp.zeros(acc_scr.shape, F32)

    def block(diagonal):
        vt = v_ref[...].T
        chains = [(mp, nt) for mp in range(2) for nt in range(tq // ATTN_QT)]
        scores = [_attn_scores(k_ref, q_ref, mp, nt, diagonal) for mp, nt in chains]
        probs = [_attn_softmax(s, m_scr, l_scr, mp, nt) for s, (mp, nt) in zip(scores, chains)]
        for (p, alpha), (mp, nt) in zip(probs, chains):
            _attn_value(vt, p, alpha, acc_scr, mp, nt)

    @pl.when(ki < qi)
    def _():
        block(False)

    @pl.when(ki == qi)
    def _():
        block(True)
        lam = _lambda(lq1, lk1, lq2, lk2, lambda_init)
        o1 = (acc_scr[0] / l_scr[0]).T
        o2 = (acc_scr[1] / l_scr[1]).T
        o_ref[...] = _diff_finish(o1, o2, lam, sw_ref, g_ref[...], lambda_init)


def _diff_attn_prompt(main, y, lp, bsz, length, lambda_init, tq):
    tk = tq
    nq = length // tq
    m = bsz * length
    pairs = [(qi, ki) for qi in range(nq) for ki in range(qi + 1)]
    qi_tab = jnp.asarray([p[0] for p in pairs], jnp.int32)
    ki_tab = jnp.asarray([p[1] for p in pairs], jnp.int32)
    vec = pl.BlockSpec((1, DIFF_HEAD_DIM), lambda b, h, t, qt, kt: (0, 0))
    return pl.pallas_call(
        functools.partial(_diff_attn_kernel, tq=tq, tk=tk, lambda_init=lambda_init),
        grid_spec=pltpu.PrefetchScalarGridSpec(
            num_scalar_prefetch=2,
            grid=(bsz, N_DIFF_HEADS, len(pairs)),
            in_specs=[
                pl.BlockSpec((tq, HEAD_W), lambda b, h, t, qt, kt: (b * nq + qt[t], COL_QD // HEAD_W + h)),
                pl.BlockSpec((tk, HEAD_W), lambda b, h, t, qt, kt: (b * nq + kt[t], COL_KD // HEAD_W + h)),
                pl.BlockSpec((tk, HEAD_W), lambda b, h, t, qt, kt: (b * nq + kt[t], COL_VD // HEAD_W + h)),
                pl.BlockSpec((tq, HEAD_W), lambda b, h, t, qt, kt: (b * nq + qt[t], COL_GD // HEAD_W + h)),
                vec, vec, vec, vec,
                pl.BlockSpec((1, HEAD_W), lambda b, h, t, qt, kt: (0, 0)),
                pl.BlockSpec(memory_space=pl.ANY),
            ],
            out_specs=pl.BlockSpec((tq, HEAD_W),
                                   lambda b, h, t, qt, kt: (b * nq + qt[t], D_SSM // HEAD_W + h)),
            scratch_shapes=[
                pltpu.VMEM((2, 1, tq), F32),
                pltpu.VMEM((2, 1, tq), F32),
                pltpu.VMEM((2, HEAD_W, tq), F32),
            ],
        ),
        out_shape=jax.ShapeDtypeStruct((m, D_MODEL), BF16),
        input_output_aliases={11: 0},
        compiler_params=_params(3),
        name="diff_attn_prompt",
    )(qi_tab, ki_tab, main, main, main, main, lp["lambda_q1"], lp["lambda_k1"], lp["lambda_q2"],
      lp["lambda_k2"], lp["subln_w"], y)


def _diff_attn_decode_kernel(q_ref, kn_ref, vn_ref, g_ref, kp_ref, vp_ref, lq1, lk1, lq2, lk2, sw_ref,
                             y_hbm, o_ref, *, past, lambda_init):
    del y_hbm
    scale = DIFF_HEAD_DIM ** -0.5
    lq = q_ref.shape[0]
    nt_dims = (((1,), (1,)), ((), ()))
    mask_p = (_chunk_of(lax.broadcasted_iota(jnp.int32, (lq, past), 1))
              <= _chunk_of(past + lax.broadcasted_iota(jnp.int32, (lq, past), 0)))
    mask_n = (_chunk_of(past + lax.broadcasted_iota(jnp.int32, (lq, lq), 1))
              <= _chunk_of(past + lax.broadcasted_iota(jnp.int32, (lq, lq), 0)))
    lam = _lambda(lq1, lk1, lq2, lk2, lambda_init)
    for h in range(N_DIFF_HEADS):
        hs = slice(h * HEAD_W, (h + 1) * HEAD_W)
        vp = _load_flat(vp_ref, past, h * HEAD_W, HEAD_W, halves_split=True).astype(BF16)
        vn = vn_ref[:, hs]
        outs = []
        for mp in range(2):
            c0 = h * HEAD_W + mp * DIFF_HEAD_DIM
            qm = q_ref[:, c0:c0 + DIFF_HEAD_DIM]
            kp = _load_flat(kp_ref, past, c0, DIFF_HEAD_DIM, halves_split=False).astype(BF16)
            sp = lax.dot_general(qm, kp, nt_dims, preferred_element_type=F32) * scale
            sn = lax.dot_general(qm, kn_ref[:, c0:c0 + DIFF_HEAD_DIM], nt_dims,
                                 preferred_element_type=F32) * scale
            sp = jnp.where(mask_p, sp, -jnp.inf)
            sn = jnp.where(mask_n, sn, -jnp.inf)
            mx = jnp.maximum(jnp.max(sp, axis=-1, keepdims=True), jnp.max(sn, axis=-1, keepdims=True))
            pp = jnp.exp(sp - mx)
            pn = jnp.exp(sn - mx)
            denom = jnp.sum(pp, axis=-1, keepdims=True) + jnp.sum(pn, axis=-1, keepdims=True)
            o = (jnp.dot(pp.astype(BF16), vp, preferred_element_type=F32)
                 + jnp.dot(pn.astype(BF16), vn, preferred_element_type=F32))
            outs.append(o / denom)
        o_ref[:, hs] = _diff_finish(outs[0], outs[1], lam, sw_ref, g_ref[:, hs], lambda_init)


def _diff_attn_decode(main, y, k_past, v_past, b_off, lp, bsz, length, past, lambda_init):
    m = bsz * length
    vec = pl.BlockSpec((1, DIFF_HEAD_DIM), lambda b: (0, 0))
    return pl.pallas_call(
        functools.partial(_diff_attn_decode_kernel, past=past, lambda_init=lambda_init),
        grid=(bsz,),
        in_specs=[
            pl.BlockSpec((length, D_DIFF), lambda b: (b, COL_QD // D_DIFF)),
            pl.BlockSpec((length, D_DIFF), lambda b: (b, COL_KD // D_DIFF)),
            pl.BlockSpec((length, D_DIFF), lambda b: (b, COL_VD // D_DIFF)),
            pl.BlockSpec((length, D_DIFF), lambda b: (b, COL_GD // D_DIFF)),
            pl.BlockSpec((past * ROW_TILES, LANE), lambda b: (b_off + b, 0)),
            pl.BlockSpec((past * ROW_TILES, LANE), lambda b: (b_off + b, 0)),
            vec, vec, vec, vec,
            pl.BlockSpec((1, HEAD_W), lambda b: (0, 0)),
            pl.BlockSpec(memory_space=pl.ANY),
        ],
        out_specs=pl.BlockSpec((length, D_DIFF), lambda b: (b, D_SSM // D_DIFF)),
        out_shape=jax.ShapeDtypeStruct((m, D_MODEL), BF16),
        input_output_aliases={11: 0},
        compiler_params=_params(1),
        name="diff_attn_decode",
    )(main, main, main, main, k_past, v_past,
      lp["lambda_q1"], lp["lambda_k1"], lp["lambda_q2"], lp["lambda_k2"], lp["subln_w"], y)


def _mem_attn_kernel(q_ref, g_ref, mk_ref, mv_ref, y_hbm, o_ref, *, n_mem):
    del y_hbm
    scale = MEM_HEAD_DIM ** -0.5
    for h in range(N_MEM_HEADS):
        hs = slice(h * MEM_HEAD_DIM, (h + 1) * MEM_HEAD_DIM)
        mk = _load_flat(mk_ref, n_mem, h * MEM_HEAD_DIM, MEM_HEAD_DIM, halves_split=True).astype(BF16)
        mv = _load_flat(mv_ref, n_mem, h * MEM_HEAD_DIM, MEM_HEAD_DIM, halves_split=True).astype(BF16)
        s = lax.dot_general(q_ref[:, hs], mk, (((1,), (1,)), ((), ())), preferred_element_type=F32) * scale
        p = jnp.exp(s - jnp.max(s, axis=-1, keepdims=True))
        pr = p / jnp.sum(p, axis=-1, keepdims=True)
        o = jnp.dot(pr.astype(BF16), mv, preferred_element_type=F32)
        o_ref[:, hs] = (o * _silu(g_ref[:, hs].astype(F32))).astype(BF16)


def _mem_attn(main, y, mk, mv, b_off, bsz, length, n_mem, tq):
    nq = length // tq
    m = bsz * length
    return pl.pallas_call(
        functools.partial(_mem_attn_kernel, n_mem=n_mem),
        grid=(bsz, nq),
        in_specs=[
            pl.BlockSpec((tq, D_MEM), lambda b, qi: (b * nq + qi, COL_QM // D_MEM)),
            pl.BlockSpec((tq, D_MEM), lambda b, qi: (b * nq + qi, COL_GM // D_MEM)),
            pl.BlockSpec((n_mem * ROW_TILES, LANE), lambda b, qi: (b_off + b, 0)),
            pl.BlockSpec((n_mem * ROW_TILES, LANE), lambda b, qi: (b_off + b, 0)),
            pl.BlockSpec(memory_space=pl.ANY),
        ],
        out_specs=pl.BlockSpec((tq, D_MEM), lambda b, qi: (b * nq + qi, (D_SSM + D_DIFF) // D_MEM)),
        out_shape=jax.ShapeDtypeStruct((m, D_MODEL), BF16),
        input_output_aliases={4: 0},
        compiler_params=_params(2),
        name="mem_attn",
    )(main, main, mk, mv, y)


def _out_proj_kernel(y_ref, w_ref, x_ref, nw_ref, o_ref):
    n_tiles = w_ref.shape[0]
    tn = w_ref.shape[2]
    y = y_ref[...]
    for t in range(n_tiles):
        o_ref[:, t * tn:(t + 1) * tn] = jnp.dot(y, w_ref[t], preferred_element_type=F32)
    tm = o_ref.shape[0]
    step = min(NORM_ROWS, tm)
    for r in range(0, tm, step):
        o = o_ref[r:r + step, :]
        o_ref[r:r + step, :] = x_ref[r:r + step, :] + o * _rms_scale(o, NORM_EPS) * nw_ref[...]


def _out_proj(y, w_out, x2d, norm_w, tm):
    m = x2d.shape[0]
    return pl.pallas_call(
        _out_proj_kernel,
        grid=(m // tm,),
        in_specs=[
            pl.BlockSpec((tm, D_MODEL), lambda i: (i, 0)),
            pl.BlockSpec(w_out.shape, lambda i: (0, 0, 0), pipeline_mode=pl.Buffered(1)),
            pl.BlockSpec((tm, D_MODEL), lambda i: (i, 0)),
            pl.BlockSpec((1, D_MODEL), lambda i: (0, 0)),
        ],
        out_specs=pl.BlockSpec((tm, D_MODEL), lambda i: (i, 0)),
        out_shape=jax.ShapeDtypeStruct((m, D_MODEL), F32),
        compiler_params=_params(1, OUT_PROJ_VMEM_LIMIT),
        name="out_proj",
    )(y, w_out, x2d, norm_w)


def _mixer_layer(x2d, bsz, length, layer, depth, kv_all, b_off, mem_off, conv_hist, h0, kv_past, past, mk,
                 mv, n_mem, lp, lambda_init, tiles):
    main, k32, v32, dt = _in_proj(x2d, lp["norm_pre_w"], lp["w_main"], lp["w_dt"], tiles["proj_tm"],
                                  layer, depth, kv_all)
    y, new_conv, h_new = _ssd(main, dt, conv_hist, h0, b_off, lp, bsz, length, tiles["ssd_q"],
                              tiles["ssd_sub"])
    if kv_past is None:
        y = _diff_attn_prompt(main, y, lp, bsz, length, lambda_init, tiles["attn_tq"])
    else:
        y = _diff_attn_decode(main, y, kv_past[0], kv_past[1], b_off, lp, bsz, length, past, lambda_init)
    y = _mem_attn(main, y, mk, mv, mem_off, bsz, length, n_mem, tiles["mem_tq"])
    x_new = _out_proj(y, lp["w_out"], x2d, lp["norm_post_w"], tiles["out_tm"])
    return x_new, new_conv, h_new, (k32, v32)


def _layer_params(l, norm_pre_w, norm_post_w, w_in_t, conv_w, conv_b, dt_bias, a_log, d_skip,
                  ssm_norm_w, lambda_q1, lambda_k1, lambda_q2, lambda_k2, subln_w, mem_norm_w,
                  w_mem_kv, w_out):
    w_main, w_dt = _w_prep(w_in_t, l)
    pad_heads = lambda v: jnp.pad(v, (0, LANE - N_SSM_HEADS)).reshape(1, LANE)
    head_of_channel = jnp.arange(D_SSM) // SSM_HEAD_DIM
    expand = (jnp.arange(LANE)[:, None] == head_of_channel[None, :]).astype(BF16)
    expand = jnp.concatenate([expand, expand], axis=0)
    return {
        "norm_pre_w": norm_pre_w[l].reshape(1, D_MODEL),
        "norm_post_w": norm_post_w[l].reshape(1, D_MODEL),
        "w_main": w_main,
        "w_dt": w_dt,
        "conv_w": conv_w[l],
        "conv_b": conv_b[l].reshape(1, D_CONV),
        "dt_bias": pad_heads(dt_bias[l]),
        "a_log": pad_heads(a_log[l]),
        "d_skip_e": jnp.repeat(d_skip[l], SSM_HEAD_DIM).reshape(1, D_SSM),
        "ssm_norm_w": ssm_norm_w[l].reshape(1, D_SSM),
        "expand": expand,
        "lambda_q1": lambda_q1[l].reshape(1, DIFF_HEAD_DIM),
        "lambda_k1": lambda_k1[l].reshape(1, DIFF_HEAD_DIM),
        "lambda_q2": lambda_q2[l].reshape(1, DIFF_HEAD_DIM),
        "lambda_k2": lambda_k2[l].reshape(1, DIFF_HEAD_DIM),
        "subln_w": subln_w[l].reshape(1, HEAD_W),
        "mem_norm_w": mem_norm_w[l].reshape(1, D_MODEL),
        "w_mem_kv": _layer_bf16_tiles(w_mem_kv, l, D_MEM),
        "w_out": _layer_bf16_tiles(w_out, l, OUT_TN),
    }


OUT_TN = 512


def _tiles(bsz, length):
    m = bsz * length
    return {
        "proj_tm": min(512, m),
        "ssd_q": min(128, length),
        "ssd_sub": 2 if length % 256 == 0 else 1,
        "attn_tq": min(1024, length),
        "mem_tq": min(1024, length),
        "out_tm": min(256, m),
    }


def kernel(x_prompt, x_sample, mem_prompt, cache_conv, state_ssm, cache_k, cache_v, cache_mem_k,
           cache_mem_v, norm_pre_w, norm_post_w, w_in, conv_w, conv_b, dt_bias, a_log, d_skip,
           ssm_norm_w, lambda_q1, lambda_k1, lambda_q2, lambda_k2, subln_w, mem_norm_w, w_mem_kv,
           w_out):
    depth = w_in.shape[0]
    bp, lp_, _ = x_prompt.shape
    bs, ls, _ = x_sample.shape
    n_mem = mem_prompt.shape[1]
    past = cache_k.shape[2]
    xp = x_prompt.reshape(bp * lp_, D_MODEL)
    xs = x_sample.reshape(bs * ls, D_MODEL)
    mem2d = mem_prompt.reshape(bp * n_mem, D_MODEL)
    tiles_p = _tiles(bp, lp_)
    tiles_s = _tiles(bs, ls)
    conv0 = jnp.zeros((bp, CONV_W - 1, D_CONV), F32)
    h_zero = jnp.zeros((bp, N_SSM_HEADS, SSM_HEAD_DIM, D_STATE), F32)
    conv_all = cache_conv.reshape(depth * bs, CONV_W - 1, D_CONV)
    ssm_all = state_ssm.reshape(depth * bs, N_SSM_HEADS, SSM_HEAD_DIM, D_STATE)
    kv_all = (cache_k.reshape(-1, LANE), _to_flat_halves(cache_v))
    mem_k_all = _to_flat_halves(cache_mem_k)
    mem_v_all = _to_flat_halves(cache_mem_v)
    outs = [[] for _ in range(6)]
    kv_p = kv_s = None
    w_in_t = jnp.swapaxes(w_in, 1, 2)
    for l in range(depth):
        lp = _layer_params(l, norm_pre_w, norm_post_w, w_in_t, conv_w, conv_b, dt_bias, a_log, d_skip,
                           ssm_norm_w, lambda_q1, lambda_k1, lambda_q2, lambda_k2, subln_w,
                           mem_norm_w, w_mem_kv, w_out)
        lambda_init = 0.8 - 0.6 * math.exp(-0.3 * l)
        mk_p, mv_p = _mem_kv(mem2d, lp["mem_norm_w"], lp["w_mem_kv"], min(512, bp * n_mem))
        xp, c_p, h_p, kv_p = _mixer_layer(xp, bp, lp_, l, depth, kv_p, 0, 0, conv0, h_zero, None, 0, mk_p,
                                          mv_p, n_mem, lp, lambda_init, tiles_p)
        xs, c_s, h_s, kv_s = _mixer_layer(xs, bs, ls, l, depth, kv_s, l * bs, l * bs, conv_all, ssm_all,
                                          kv_all, past, mem_k_all, mem_v_all, n_mem, lp, lambda_init,
                                          tiles_s)
        for lst, val in zip(outs, (c_p, h_p, mk_p, mv_p, c_s, h_s)):
            lst.append(val)
    st = [jnp.stack(o) for o in outs]
    return (
        xp.reshape(bp, lp_, D_MODEL),
        xs.reshape(bs, ls, D_MODEL),
        st[0],
        st[1],
        kv_p[0].reshape(depth, bp, lp_, N_DIFF_HEADS, 2, DIFF_HEAD_DIM),
        _from_flat_halves(kv_p[1], (depth, bp, lp_, N_DIFF_HEADS, 2 * DIFF_HEAD_DIM)),
        _from_flat_halves(st[2], (depth, bp, n_mem, N_MEM_HEADS, MEM_HEAD_DIM)),
        _from_flat_halves(st[3], (depth, bp, n_mem, N_MEM_HEADS, MEM_HEAD_DIM)),
        st[4],
        st[5],
        kv_s[0].reshape(depth, bs, ls, N_DIFF_HEADS, 2, DIFF_HEAD_DIM),
        _from_flat_halves(kv_s[1], (depth, bs, ls, N_DIFF_HEADS, 2 * DIFF_HEAD_DIM)),
    )
```

```python
import functools
import math

import jax
import jax.numpy as jnp
from jax import lax
from jax.experimental import pallas as pl
from jax.experimental.pallas import tpu as pltpu

F32 = jnp.float32
BF16 = jnp.bfloat16

D_MODEL = 4096
D_SSM = 2048
SSM_HEAD_DIM = 64
N_SSM_HEADS = 32
N_SSM_GROUPS = 4
HEADS_PER_GROUP = N_SSM_HEADS // N_SSM_GROUPS
D_GROUP = D_SSM // N_SSM_GROUPS
D_STATE = 128
CONV_W = 4
D_BC = N_SSM_GROUPS * D_STATE
D_CONV = D_SSM + 2 * D_BC
D_DIFF = 1024
N_DIFF_HEADS = 4
DIFF_HEAD_DIM = 128
D_MEM = 1024
N_MEM_HEADS = 4
MEM_HEAD_DIM = 256
CHUNK = 64
NORM_EPS = 1e-6
SUBLN_EPS = 1e-5

LANE = 128
VMEM_LIMIT = 56 * 1024 * 1024
OUT_PROJ_VMEM_LIMIT = 60 * 1024 * 1024

HEAD_W = 2 * DIFF_HEAD_DIM
COL_KD = 0
COL_VD = COL_KD + D_DIFF
COL_Z = COL_VD + D_DIFF
COL_X = COL_Z + D_SSM
COL_B = COL_X + D_SSM
COL_C = COL_B + D_BC
COL_QD = COL_C + D_BC
COL_GD = COL_QD + D_DIFF
COL_QM = COL_GD + D_DIFF
COL_GM = COL_QM + D_MEM
D_MAIN = COL_GM + D_MEM
PROJ_TN = 1024
K_TILE = COL_KD // PROJ_TN
V_TILE = COL_VD // PROJ_TN


def _params(n_axes, vmem_limit=VMEM_LIMIT):
    return pltpu.CompilerParams(dimension_semantics=("arbitrary",) * n_axes,
                                vmem_limit_bytes=vmem_limit)


def _silu(x):
    return x * jax.nn.sigmoid(x)


def _chunk_of(pos):
    assert CHUNK & (CHUNK - 1) == 0
    return lax.shift_right_logical(pos, jnp.int32(CHUNK.bit_length() - 1))


def _rms_scale(x, eps):
    return lax.rsqrt(jnp.mean(x * x, axis=-1, keepdims=True) + eps)


ROW_TILES = 1024 // LANE
N_HEADS_1024 = 1024 // HEAD_W


def _flat_row(c, halves_split):
    head, half = divmod(c, 2)
    return half * N_HEADS_1024 + head if halves_split else c


def _store_flat(ref, val, halves_split):
    rows = val.shape[0]
    for c in range(ROW_TILES):
        ref[pl.ds(_flat_row(c, halves_split), rows, stride=ROW_TILES), :] = val[:, c * LANE:(c + 1) * LANE]


def _load_flat(ref, rows, col0, width, halves_split):
    parts = [ref[pl.ds(_flat_row(col0 // LANE + c, halves_split), rows, stride=ROW_TILES), :]
             for c in range(width // LANE)]
    return parts[0] if len(parts) == 1 else jnp.concatenate(parts, axis=1)


def _to_flat_halves(a):
    return a.reshape(-1, N_HEADS_1024, 2, LANE).transpose(0, 2, 1, 3).reshape(-1, LANE)


def _from_flat_halves(flat, shape):
    return flat.reshape(-1, 2, N_HEADS_1024, LANE).transpose(0, 2, 1, 3).reshape(shape)


NORM_ROWS = 128


def _norm_rows(x_ref, nw_ref, h_scr):
    tm = x_ref.shape[0]
    step = min(NORM_ROWS, tm)
    for r in range(0, tm, step):
        x = x_ref[r:r + step, :]
        h_scr[r:r + step, :] = (x * _rms_scale(x, NORM_EPS) * nw_ref[...]).astype(BF16)


O_DT = D_SSM + D_CONV
O_Q = O_DT + N_SSM_HEADS
MAIN_PIECES = ((COL_Z, 0, O_DT), (COL_QD, O_Q, D_DIFF), (COL_KD, O_Q + D_DIFF, D_DIFF),
               (COL_VD, O_Q + 2 * D_DIFF, D_DIFF), (COL_GD, O_Q + 3 * D_DIFF, D_DIFF),
               (COL_QM, O_Q + 4 * D_DIFF, D_MEM), (COL_GM, O_Q + 4 * D_DIFF + D_MEM, D_MEM))
PREP_COLS = 256
PREP_PER_TILE = PROJ_TN // PREP_COLS


def _prep_source_columns():
    src = []
    for dst, first, width in sorted(MAIN_PIECES):
        assert dst == len(src) * PREP_COLS
        src += list(range(first, first + width, PREP_COLS))
    assert len(src) * PREP_COLS == D_MAIN
    return jnp.asarray(src, jnp.int32)


def _w_prep_kernel(src_ref, w_ref, wdt_ref, main_ref, dt_ref):
    del src_ref
    main_ref[...] = w_ref[0].T.astype(BF16)

    @pl.when(pl.program_id(0) == 0)
    def _():
        lane = lax.broadcasted_iota(jnp.int32, (1, LANE), 1)
        dt_ref[...] = jnp.where(lane < N_SSM_HEADS, wdt_ref[...].T, 0.0).astype(BF16)


def _w_prep(w_in_t, layer):
    return pl.pallas_call(
        _w_prep_kernel,
        grid_spec=pltpu.PrefetchScalarGridSpec(
            num_scalar_prefetch=1,
            grid=(D_MAIN // PREP_COLS,),
            in_specs=[
                pl.BlockSpec((pl.Element(1), pl.Element(PREP_COLS), pl.Element(D_MODEL)),
                             lambda j, src: (layer, pl.multiple_of(src[j], N_SSM_HEADS), 0)),
                pl.BlockSpec((None, LANE, D_MODEL), lambda j, src: (layer, O_DT // LANE, 0)),
            ],
            out_specs=[
                pl.BlockSpec((None, D_MODEL, PREP_COLS),
                             lambda j, src: (j // PREP_PER_TILE, 0, j % PREP_PER_TILE)),
                pl.BlockSpec((D_MODEL, LANE), lambda j, src: (0, 0)),
            ],
        ),
        out_shape=[
            jax.ShapeDtypeStruct((D_MAIN // PROJ_TN, D_MODEL, PROJ_TN), BF16),
            jax.ShapeDtypeStruct((D_MODEL, LANE), BF16),
        ],
        compiler_params=_params(1),
        name="w_prep",
    )(_prep_source_columns(), w_in_t, w_in_t)


CAST_ROWS = 1024


def _cast_kernel(w_ref, o_ref):
    o_ref[...] = w_ref[...].astype(BF16)


def _layer_bf16_tiles(w, layer, tn):
    _, rows, cols = w.shape
    return pl.pallas_call(
        _cast_kernel,
        grid=(rows // CAST_ROWS, cols // tn),
        in_specs=[pl.BlockSpec((None, CAST_ROWS, tn), lambda i, t: (layer, i, t))],
        out_specs=pl.BlockSpec((None, CAST_ROWS, tn), lambda i, t: (t, i, 0)),
        out_shape=jax.ShapeDtypeStruct((cols // tn, rows, tn), BF16),
        compiler_params=_params(2),
        name="cast_bf16",
    )(w)


def _in_proj_kernel(x_ref, nw_ref, w_ref, wdt_ref, *refs):
    main_ref, k_ref, v_ref, dt_ref, h_scr = refs[-5:]
    j = pl.program_id(1)

    @pl.when(j == 0)
    def _():
        _norm_rows(x_ref, nw_ref, h_scr)
        dt_ref[...] = jnp.dot(h_scr[...], wdt_ref[...], preferred_element_type=F32)

    acc = jnp.dot(h_scr[...], w_ref[...], preferred_element_type=F32)
    main_ref[...] = acc.astype(BF16)

    @pl.when(j == K_TILE)
    def _():
        _store_flat(k_ref, acc, halves_split=False)

    @pl.when(j == V_TILE)
    def _():
        _store_flat(v_ref, acc, halves_split=True)


def _in_proj(x2d, norm_w, w_main, w_dt, tm, layer, depth, kv_all):
    m = x2d.shape[0]
    ni = m // tm
    grid = (ni, D_MAIN // PROJ_TN)
    kv_spec = pl.BlockSpec((tm * ROW_TILES, LANE), lambda i, j: (layer * ni + i, 0))
    kv_shape = jax.ShapeDtypeStruct((depth * m * ROW_TILES, LANE), F32)
    in_specs = [
        pl.BlockSpec((tm, D_MODEL), lambda i, j: (i, 0)),
        pl.BlockSpec((1, D_MODEL), lambda i, j: (0, 0)),
        pl.BlockSpec((None, D_MODEL, PROJ_TN), lambda i, j: (j, 0, 0)),
        pl.BlockSpec((D_MODEL, LANE), lambda i, j: (0, 0)),
    ]
    operands = [x2d, norm_w, w_main, w_dt]
    aliases = {}
    if kv_all is not None:
        in_specs += [pl.BlockSpec(memory_space=pl.ANY)] * 2
        aliases = {len(operands): 1, len(operands) + 1: 2}
        operands += list(kv_all)
    return pl.pallas_call(
        _in_proj_kernel,
        grid=grid,
        in_specs=in_specs,
        out_specs=[
            pl.BlockSpec((tm, PROJ_TN), lambda i, j: (i, j)),
            kv_spec,
            kv_spec,
            pl.BlockSpec((tm, LANE), lambda i, j: (i, 0)),
        ],
        out_shape=[
            jax.ShapeDtypeStruct((m, D_MAIN), BF16),
            kv_shape,
            kv_shape,
            jax.ShapeDtypeStruct((m, LANE), F32),
        ],
        scratch_shapes=[pltpu.VMEM((tm, D_MODEL), BF16)],
        input_output_aliases=aliases,
        compiler_params=_params(2),
        name="in_proj",
    )(*operands)


def _mem_kv_kernel(x_ref, nw_ref, w_ref, mk_ref, mv_ref, h_scr):
    j = pl.program_id(1)

    @pl.when(j == 0)
    def _():
        _norm_rows(x_ref, nw_ref, h_scr)

    acc = jnp.dot(h_scr[...], w_ref[...], preferred_element_type=F32)

    @pl.when(j == 0)
    def _():
        _store_flat(mk_ref, acc, halves_split=True)

    @pl.when(j == 1)
    def _():
        _store_flat(mv_ref, acc, halves_split=True)


def _mem_kv(mem2d, norm_w, w_kv, tm):
    m = mem2d.shape[0]
    return pl.pallas_call(
        _mem_kv_kernel,
        grid=(m // tm, 2),
        in_specs=[
            pl.BlockSpec((tm, D_MODEL), lambda i, j: (i, 0)),
            pl.BlockSpec((1, D_MODEL), lambda i, j: (0, 0)),
            pl.BlockSpec((None, D_MODEL, D_MEM), lambda i, j: (j, 0, 0)),
        ],
        out_specs=[
            pl.BlockSpec((tm * ROW_TILES, LANE), lambda i, j: (i, 0)),
            pl.BlockSpec((tm * ROW_TILES, LANE), lambda i, j: (i, 0)),
        ],
        out_shape=[jax.ShapeDtypeStruct((m * ROW_TILES, LANE), F32)] * 2,
        scratch_shapes=[pltpu.VMEM((tm, D_MODEL), BF16)],
        compiler_params=_params(2),
        name="mem_kv",
    )(mem2d, norm_w, w_kv)


def _split_bf16(x, n):
    parts = []
    r = x
    for _ in range(n - 1):
        p = r.astype(BF16)
        parts.append(p)
        r = r - p.astype(F32)
    parts.append(r.astype(BF16))
    return parts


def _sum_rows(a, n, rows):
    out = a[(n - 1) * rows:n * rows]
    for i in range(n - 2, -1, -1):
        out = out + a[i * rows:(i + 1) * rows]
    return out


HIST_ROWS = 16


def _conv_shift_matrices(q):
    t = jnp.arange(q)[:, None]
    col = jnp.arange(2 * HIST_ROWS + q)[None, :]
    with_terms, plain = [], []
    for j in range(CONV_W - 1):
        src = t - (CONV_W - 1 - j)
        band = col == 2 * HIST_ROWS + src
        plain.append(band)
        with_terms.append(band | ((src < 0) & (col == HIST_ROWS + src)))
    return jnp.stack([jnp.concatenate(with_terms, axis=0), jnp.concatenate(plain, axis=0)]).astype(BF16)


def _ssd_chunk(win_ref, shift, z_ref, dt_ref, y_ref, y_scr, st_scr,
               cw_ref, cb_ref, dtb_ref, alog_ref, dskip_ref, nw_ref, e_ref, q):
    taps = jnp.dot(shift, win_ref[...], preferred_element_type=F32)
    acc = cb_ref[...] + taps[0:q] * cw_ref[0:1, :]
    for j in range(1, CONV_W - 1):
        acc = acc + taps[j * q:(j + 1) * q] * cw_ref[j:j + 1, :]
    cur = win_ref[2 * HIST_ROWS:2 * HIST_ROWS + q, :].astype(F32)
    acc = acc + cur * cw_ref[CONV_W - 1:CONV_W, :]
    xbc = _silu(acc)
    xs = xbc[:, 0:D_SSM]
    xs_bf = xs.astype(BF16)
    bm = xbc[:, D_SSM:D_SSM + D_BC].astype(BF16)
    cm = xbc[:, D_SSM + D_BC:D_CONV].astype(BF16)

    dtv = jax.nn.softplus(dt_ref[...] + dtb_ref[...])
    a = -jnp.exp(alog_ref[...])
    da = dtv * a
    ri = lax.broadcasted_iota(jnp.int32, (q, q), 0)
    ci = lax.broadcasted_iota(jnp.int32, (q, q), 1)
    causal = ri >= ci
    tril = jnp.where(causal, 1.0, 0.0).astype(BF16)
    triu = jnp.where(ri <= ci, 1.0, 0.0).astype(BF16)
    eye = jnp.where(ri == ci, 1.0, 0.0).astype(BF16)
    da3 = jnp.concatenate(_split_bf16(da, 3), axis=1)
    dt3 = jnp.concatenate(_split_bf16(dtv, 3), axis=1)
    cum3 = jnp.dot(tril, da3, preferred_element_type=F32)
    cum = cum3[:, 2 * LANE:3 * LANE] + cum3[:, LANE:2 * LANE] + cum3[:, 0:LANE]
    tn_dims = (((0,), (0,)), ((), ()))
    cum_t = _sum_rows(lax.dot_general(da3, triu, tn_dims, preferred_element_type=F32), 3, LANE)
    dt_t = _sum_rows(lax.dot_general(dt3, eye, tn_dims, preferred_element_type=F32), 3, LANE)

    ecum = jnp.exp(cum)
    dd = dtv * jnp.exp(cum[q - 1:q, :] - cum)
    ex_in = jnp.concatenate([jnp.concatenate(_split_bf16(ecum, 2), axis=1),
                             jnp.concatenate(_split_bf16(dd, 2), axis=1)], axis=0)
    ex = jnp.dot(ex_in, e_ref[...], preferred_element_type=F32)
    ecum_e = ex[0:q]
    dd_e = ex[q:2 * q]
    xdd = (xs * dd_e).astype(BF16)

    lane = lax.broadcasted_iota(jnp.int32, (q, LANE), 1)
    lo_half = lane < SSM_HEAD_DIM
    nt_dims = (((1,), (1,)), ((), ()))
    for g in range(N_SSM_GROUPS):
        bg = bm[:, g * D_STATE:(g + 1) * D_STATE]
        cg = cm[:, g * D_STATE:(g + 1) * D_STATE]
        gs = slice(g * D_GROUP, (g + 1) * D_GROUP)
        cbg = lax.dot_general(cg, bg, nt_dims, preferred_element_type=F32)
        st = st_scr[g]
        y_off = jnp.dot(cg, st.astype(BF16), preferred_element_type=F32) * ecum_e[:, gs]
        st_scr[g] = st * ecum_e[q - 1:q, gs] + lax.dot_general(
            bg, xdd[:, gs], tn_dims, preferred_element_type=F32)
        for t in range(HEADS_PER_GROUP // 4):
            ws, xrows = [], []
            for k in range(4):
                h = g * HEADS_PER_GROUP + 4 * t + k
                seg = cum[:, h:h + 1] - cum_t[h:h + 1, :]
                lmat = jnp.exp(jnp.where(causal, seg, -jnp.inf))
                ws.append((cbg * lmat * dt_t[h:h + 1, :]).astype(BF16))
                pair = xs_bf[:, g * D_GROUP + (2 * t + k // 2) * LANE:g * D_GROUP + (2 * t + k // 2 + 1) * LANE]
                zero = jnp.zeros_like(pair)
                half = jnp.where(lo_half, pair, zero) if k % 2 == 0 else jnp.where(lo_half, zero, pair)
                xrows.append(jnp.concatenate([half, zero] if k < 2 else [zero, half], axis=1))
            cs = slice(g * D_GROUP + 2 * t * LANE, g * D_GROUP + (2 * t + 2) * LANE)
            y_diag = jnp.dot(jnp.concatenate(ws, axis=1), jnp.concatenate(xrows, axis=0),
                             preferred_element_type=F32)
            y_scr[:, cs] = (y_diag + y_off[:, 2 * t * LANE:(2 * t + 2) * LANE]) + xs[:, cs] * dskip_ref[:, cs]

    for g in range(N_SSM_GROUPS):
        gs = slice(g * D_GROUP, (g + 1) * D_GROUP)
        yg = y_scr[:, gs] * _silu(z_ref[:, gs].astype(F32))
        y_ref[:, gs] = (yg * _rms_scale(yg, NORM_EPS) * nw_ref[:, gs]).astype(BF16)
    return cur


def _ssd_kernel(z_ref, xx_ref, xb_ref, xc_ref, dt_ref, hist_ref, h0_ref,
                cw_ref, cb_ref, dtb_ref, alog_ref, dskip_ref, nw_ref, e_ref, shift_ref,
                y_ref, nconv_ref, hnew_ref, xp_scr, st_scr, y_scr, *, q, n_sub):
    c = pl.program_id(1)
    nc = pl.num_programs(1)
    rows = n_sub * q

    @pl.when(c == 0)
    def _():
        hist = jnp.concatenate(
            [jnp.zeros((HIST_ROWS - (CONV_W - 1), D_CONV), F32), hist_ref[0]], axis=0)
        hi, lo = _split_bf16(hist, 2)
        xp_scr[0:HIST_ROWS, :] = hi
        xp_scr[HIST_ROWS:2 * HIST_ROWS, :] = lo
        for g in range(N_SSM_GROUPS):
            hg = h0_ref[0, g * HEADS_PER_GROUP:(g + 1) * HEADS_PER_GROUP]
            st_scr[g] = hg.reshape(D_GROUP, D_STATE).T

    blk = slice(2 * HIST_ROWS, 2 * HIST_ROWS + rows)
    xp_scr[blk, 0:D_SSM] = xx_ref[...]
    xp_scr[blk, D_SSM:D_SSM + D_BC] = xb_ref[...]
    xp_scr[blk, D_SSM + D_BC:D_CONV] = xc_ref[...]
    for sub in range(n_sub):
        r = slice(sub * q, (sub + 1) * q)
        cur = _ssd_chunk(xp_scr.at[sub * q:sub * q + 2 * HIST_ROWS + q, :], shift_ref[min(sub, 1)],
                         z_ref.at[r, :], dt_ref.at[r, :], y_ref.at[r, :], y_scr.at[r, :], st_scr,
                         cw_ref, cb_ref, dtb_ref, alog_ref, dskip_ref, nw_ref, e_ref, q)

    @pl.when(c == nc - 1)
    def _():
        nconv_ref[0] = cur[q - (CONV_W - 1):q]
        for g in range(N_SSM_GROUPS):
            hnew_ref[0, g * HEADS_PER_GROUP:(g + 1) * HEADS_PER_GROUP] = (
                st_scr[g].T.reshape(HEADS_PER_GROUP, SSM_HEAD_DIM, D_STATE))

    xp_scr[0:HIST_ROWS, :] = xp_scr[2 * HIST_ROWS + rows - HIST_ROWS:2 * HIST_ROWS + rows, :]
    xp_scr[HIST_ROWS:2 * HIST_ROWS, :] = jnp.zeros((HIST_ROWS, D_CONV), BF16)


def _ssd(main, dt, conv_hist, h0, b_off, lp, bsz, length, q, n_sub):
    chunk, q = q, n_sub * q
    nc = length // q
    m = bsz * length
    row = lambda b, c: b * nc + c
    return pl.pallas_call(
        functools.partial(_ssd_kernel, q=chunk, n_sub=n_sub),
        grid=(bsz, nc),
        in_specs=[
            pl.BlockSpec((q, D_SSM), lambda b, c: (row(b, c), COL_Z // D_SSM)),
            pl.BlockSpec((q, D_SSM), lambda b, c: (row(b, c), COL_X // D_SSM)),
            pl.BlockSpec((q, D_BC), lambda b, c: (row(b, c), COL_B // D_BC)),
            pl.BlockSpec((q, D_BC), lambda b, c: (row(b, c), COL_C // D_BC)),
            pl.BlockSpec((q, LANE), lambda b, c: (row(b, c), 0)),
            pl.BlockSpec((1, CONV_W - 1, D_CONV), lambda b, c: (b_off + b, 0, 0)),
            pl.BlockSpec((1, N_SSM_HEADS, SSM_HEAD_DIM, D_STATE), lambda b, c: (b_off + b, 0, 0, 0)),
            pl.BlockSpec((CONV_W, D_CONV), lambda b, c: (0, 0)),
            pl.BlockSpec((1, D_CONV), lambda b, c: (0, 0)),
            pl.BlockSpec((1, LANE), lambda b, c: (0, 0)),
            pl.BlockSpec((1, LANE), lambda b, c: (0, 0)),
            pl.BlockSpec((1, D_SSM), lambda b, c: (0, 0)),
            pl.BlockSpec((1, D_SSM), lambda b, c: (0, 0)),
            pl.BlockSpec((2 * LANE, D_SSM), lambda b, c: (0, 0)),
            pl.BlockSpec((2, (CONV_W - 1) * chunk, 2 * HIST_ROWS + chunk), lambda b, c: (0, 0, 0)),
        ],
        out_specs=[
            pl.BlockSpec((q, D_SSM), lambda b, c: (row(b, c), 0)),
            pl.BlockSpec((1, CONV_W - 1, D_CONV), lambda b, c: (b, 0, 0)),
            pl.BlockSpec((1, N_SSM_HEADS, SSM_HEAD_DIM, D_STATE), lambda b, c: (b, 0, 0, 0)),
        ],
        out_shape=[
            jax.ShapeDtypeStruct((m, D_MODEL), BF16),
            jax.ShapeDtypeStruct((bsz, CONV_W - 1, D_CONV), F32),
            jax.ShapeDtypeStruct((bsz, N_SSM_HEADS, SSM_HEAD_DIM, D_STATE), F32),
        ],
        scratch_shapes=[
            pltpu.VMEM((2 * HIST_ROWS + q, D_CONV), BF16),
            pltpu.VMEM((N_SSM_GROUPS, D_STATE, D_GROUP), F32),
            pltpu.VMEM((q, D_SSM), F32),
        ],
        compiler_params=_params(2),
        name="ssd",
    )(main, main, main, main, dt, conv_hist, h0,
      lp["conv_w"], lp["conv_b"], lp["dt_bias"], lp["a_log"], lp["d_skip_e"], lp["ssm_norm_w"],
      lp["expand"], _conv_shift_matrices(chunk))


def _lambda(lq1, lk1, lq2, lk2, lambda_init):
    return (jnp.exp(jnp.sum(lq1[...] * lk1[...], axis=-1, keepdims=True))
            - jnp.exp(jnp.sum(lq2[...] * lk2[...], axis=-1, keepdims=True)) + lambda_init)


def _diff_finish(o1, o2, lam, sw_ref, g, lambda_init):
    o = o1 - lam * o2
    on = (o * _rms_scale(o, SUBLN_EPS) * sw_ref[...]) * (1.0 - lambda_init)
    return (on * _silu(g.astype(F32))).astype(BF16)


ATTN_QT = 256
ATTN_HEADS = 2


def _attn_scores(k_ref, q_ref, mp, nt, diagonal):
    cols = slice(mp * DIFF_HEAD_DIM, (mp + 1) * DIFF_HEAD_DIM)
    qs = slice(nt * ATTN_QT, (nt + 1) * ATTN_QT)
    kw = (nt + 1) * ATTN_QT if diagonal else k_ref.shape[0]
    s = lax.dot_general(k_ref[0:kw, cols], q_ref[qs, cols], (((1,), (1,)), ((), ())),
                        preferred_element_type=F32)
    if diagonal:
        r = lax.broadcasted_iota(jnp.int32, (ATTN_QT, ATTN_QT), 0)
        q = lax.broadcasted_iota(jnp.int32, (ATTN_QT, ATTN_QT), 1)
        tail = jnp.where(_chunk_of(r) <= _chunk_of(q), s[kw - ATTN_QT:kw], -jnp.inf)
        s = tail if kw == ATTN_QT else jnp.concatenate([s[0:kw - ATTN_QT], tail], axis=0)
    return s


def _attn_softmax(s, m_scr, l_scr, mp, nt):
    c = (DIFF_HEAD_DIM ** -0.5) * math.log2(math.e)
    qs = slice(nt * ATTN_QT, (nt + 1) * ATTN_QT)
    m_prev = m_scr[mp, :, qs]
    m_new = jnp.maximum(m_prev, jnp.max(s, axis=0, keepdims=True))
    p = jnp.exp2((s - m_new) * c)
    alpha = jnp.exp2((m_prev - m_new) * c)
    l_scr[mp, :, qs] = alpha * l_scr[mp, :, qs] + jnp.sum(p, axis=0, keepdims=True)
    m_scr[mp, :, qs] = m_new
    return p.astype(BF16), alpha


def _attn_value(vt, p, alpha, acc_scr, mp, nt):
    qs = slice(nt * ATTN_QT, (nt + 1) * ATTN_QT)
    kw = p.shape[0]
    acc_scr[mp, :, qs] = alpha * acc_scr[mp, :, qs] + jnp.dot(vt[:, 0:kw], p, preferred_element_type=F32)


def _diff_attn_kernel(qi_ref, ki_ref, q_ref, k_ref, v_ref, g_ref, lq1, lk1, lq2, lk2, sw_ref, y_hbm,
                      o_ref, m_scr, l_scr, acc_scr, *, tq, tk, lambda_init):
    del y_hbm
    qi = qi_ref[pl.program_id(2)]
    ki = ki_ref[pl.program_id(2)]
    assert tq == tk and tq % ATTN_QT == 0 and ATTN_QT % CHUNK == 0

    @pl.when(ki == 0)
    def _():
        m_scr[...] = jnp.full(m_scr.shape, -jnp.inf, F32)
        l_scr[...] = jnp.zeros(l_scr.shape, F32)
        acc_scr[...] = jnp.zeros(acc_scr.shape, F32)

    def block(diagonal):
        vts = [v_ref[:, hl * HEAD_W:(hl + 1) * HEAD_W].T for hl in range(ATTN_HEADS)]
        chains = [(mp, nt) for mp in range(2 * ATTN_HEADS) for nt in range(tq // ATTN_QT)]
        scores = [_attn_scores(k_ref, q_ref, mp, nt, diagonal) for mp, nt in chains]
        probs = [_attn_softmax(s, m_scr, l_scr, mp, nt) for s, (mp, nt) in zip(scores, chains)]
        for (p, alpha), (mp, nt) in zip(probs, chains):
            _attn_value(vts[mp // 2], p, alpha, acc_scr, mp, nt)

    @pl.when(ki < qi)
    def _():
        block(False)

    @pl.when(ki == qi)
    def _():
        block(True)
        lam = _lambda(lq1, lk1, lq2, lk2, lambda_init)
        for hl in range(ATTN_HEADS):
            hs = slice(hl * HEAD_W, (hl + 1) * HEAD_W)
            o1 = (acc_scr[2 * hl] / l_scr[2 * hl]).T
            o2 = (acc_scr[2 * hl + 1] / l_scr[2 * hl + 1]).T
            o_ref[:, hs] = _diff_finish(o1, o2, lam, sw_ref, g_ref[:, hs], lambda_init)


def _diff_attn_prompt(main, y, lp, bsz, length, lambda_init, tq):
    tk = tq
    nq = length // tq
    m = bsz * length
    wb = ATTN_HEADS * HEAD_W
    pairs = [(qi, ki) for qi in range(nq) for ki in range(qi + 1)]
    qi_tab = jnp.asarray([p[0] for p in pairs], jnp.int32)
    ki_tab = jnp.asarray([p[1] for p in pairs], jnp.int32)
    vec = pl.BlockSpec((1, DIFF_HEAD_DIM), lambda b, h, t, qt, kt: (0, 0))
    return pl.pallas_call(
        functools.partial(_diff_attn_kernel, tq=tq, tk=tk, lambda_init=lambda_init),
        grid_spec=pltpu.PrefetchScalarGridSpec(
            num_scalar_prefetch=2,
            grid=(bsz, N_DIFF_HEADS // ATTN_HEADS, len(pairs)),
            in_specs=[
                pl.BlockSpec((tq, wb), lambda b, h, t, qt, kt: (b * nq + qt[t], COL_QD // wb + h)),
                pl.BlockSpec((tk, wb), lambda b, h, t, qt, kt: (b * nq + kt[t], COL_KD // wb + h)),
                pl.BlockSpec((tk, wb), lambda b, h, t, qt, kt: (b * nq + kt[t], COL_VD // wb + h)),
                pl.BlockSpec((tq, wb), lambda b, h, t, qt, kt: (b * nq + qt[t], COL_GD // wb + h)),
                vec, vec, vec, vec,
                pl.BlockSpec((1, HEAD_W), lambda b, h, t, qt, kt: (0, 0)),
                pl.BlockSpec(memory_space=pl.ANY),
            ],
            out_specs=pl.BlockSpec((tq, wb), lambda b, h, t, qt, kt: (b * nq + qt[t], D_SSM // wb + h)),
            scratch_shapes=[
                pltpu.VMEM((2 * ATTN_HEADS, 1, tq), F32),
                pltpu.VMEM((2 * ATTN_HEADS, 1, tq), F32),
                pltpu.VMEM((2 * ATTN_HEADS, HEAD_W, tq), F32),
            ],
        ),
        out_shape=jax.ShapeDtypeStruct((m, D_MODEL), BF16),
        input_output_aliases={11: 0},
        compiler_params=_params(3),
        name="diff_attn_prompt",
    )(qi_tab, ki_tab, main, main, main, main, lp["lambda_q1"], lp["lambda_k1"], lp["lambda_q2"],
      lp["lambda_k2"], lp["subln_w"], y)


def _diff_attn_decode_kernel(q_ref, kn_ref, vn_ref, g_ref, kp_ref, vp_ref, lq1, lk1, lq2, lk2, sw_ref,
                             y_hbm, o_ref, *, past, lambda_init):
    del y_hbm
    scale = DIFF_HEAD_DIM ** -0.5
    lq = q_ref.shape[0]
    nt_dims = (((1,), (1,)), ((), ()))
    mask_p = (_chunk_of(lax.broadcasted_iota(jnp.int32, (lq, past), 1))
              <= _chunk_of(past + lax.broadcasted_iota(jnp.int32, (lq, past), 0)))
    mask_n = (_chunk_of(past + lax.broadcasted_iota(jnp.int32, (lq, lq), 1))
              <= _chunk_of(past + lax.broadcasted_iota(jnp.int32, (lq, lq), 0)))
    lam = _lambda(lq1, lk1, lq2, lk2, lambda_init)
    for h in range(N_DIFF_HEADS):
        hs = slice(h * HEAD_W, (h + 1) * HEAD_W)
        vp = _load_flat(vp_ref, past, h * HEAD_W, HEAD_W, halves_split=True).astype(BF16)
        vn = vn_ref[:, hs]
        outs = []
        for mp in range(2):
            c0 = h * HEAD_W + mp * DIFF_HEAD_DIM
            qm = q_ref[:, c0:c0 + DIFF_HEAD_DIM]
            kp = _load_flat(kp_ref, past, c0, DIFF_HEAD_DIM, halves_split=False).astype(BF16)
            sp = lax.dot_general(qm, kp, nt_dims, preferred_element_type=F32) * scale
            sn = lax.dot_general(qm, kn_ref[:, c0:c0 + DIFF_HEAD_DIM], nt_dims,
                                 preferred_element_type=F32) * scale
            sp = jnp.where(mask_p, sp, -jnp.inf)
            sn = jnp.where(mask_n, sn, -jnp.inf)
            mx = jnp.maximum(jnp.max(sp, axis=-1, keepdims=True), jnp.max(sn, axis=-1, keepdims=True))
            pp = jnp.exp(sp - mx)
            pn = jnp.exp(sn - mx)
            denom = jnp.sum(pp, axis=-1, keepdims=True) + jnp.sum(pn, axis=-1, keepdims=True)
            o = (jnp.dot(pp.astype(BF16), vp, preferred_element_type=F32)
                 + jnp.dot(pn.astype(BF16), vn, preferred_element_type=F32))
            outs.append(o / denom)
        o_ref[:, hs] = _diff_finish(outs[0], outs[1], lam, sw_ref, g_ref[:, hs], lambda_init)


def _diff_attn_decode(main, y, k_past, v_past, b_off, lp, bsz, length, past, lambda_init):
    m = bsz * length
    vec = pl.BlockSpec((1, DIFF_HEAD_DIM), lambda b: (0, 0))
    return pl.pallas_call(
        functools.partial(_diff_attn_decode_kernel, past=past, lambda_init=lambda_init),
        grid=(bsz,),
        in_specs=[
            pl.BlockSpec((length, D_DIFF), lambda b: (b, COL_QD // D_DIFF)),
            pl.BlockSpec((length, D_DIFF), lambda b: (b, COL_KD // D_DIFF)),
            pl.BlockSpec((length, D_DIFF), lambda b: (b, COL_VD // D_DIFF)),
            pl.BlockSpec((length, D_DIFF), lambda b: (b, COL_GD // D_DIFF)),
            pl.BlockSpec((past * ROW_TILES, LANE), lambda b: (b_off + b, 0)),
            pl.BlockSpec((past * ROW_TILES, LANE), lambda b: (b_off + b, 0)),
            vec, vec, vec, vec,
            pl.BlockSpec((1, HEAD_W), lambda b: (0, 0)),
            pl.BlockSpec(memory_space=pl.ANY),
        ],
        out_specs=pl.BlockSpec((length, D_DIFF), lambda b: (b, D_SSM // D_DIFF)),
        out_shape=jax.ShapeDtypeStruct((m, D_MODEL), BF16),
        input_output_aliases={11: 0},
        compiler_params=_params(1),
        name="diff_attn_decode",
    )(main, main, main, main, k_past, v_past,
      lp["lambda_q1"], lp["lambda_k1"], lp["lambda_q2"], lp["lambda_k2"], lp["subln_w"], y)


def _mem_attn_kernel(q_ref, g_ref, mk_ref, mv_ref, y_hbm, o_ref, *, n_mem):
    del y_hbm
    scale = MEM_HEAD_DIM ** -0.5
    for h in range(N_MEM_HEADS):
        hs = slice(h * MEM_HEAD_DIM, (h + 1) * MEM_HEAD_DIM)
        mk = _load_flat(mk_ref, n_mem, h * MEM_HEAD_DIM, MEM_HEAD_DIM, halves_split=True).astype(BF16)
        mv = _load_flat(mv_ref, n_mem, h * MEM_HEAD_DIM, MEM_HEAD_DIM, halves_split=True).astype(BF16)
        s = lax.dot_general(q_ref[:, hs], mk, (((1,), (1,)), ((), ())), preferred_element_type=F32) * scale
        p = jnp.exp(s - jnp.max(s, axis=-1, keepdims=True))
        pr = p / jnp.sum(p, axis=-1, keepdims=True)
        o = jnp.dot(pr.astype(BF16), mv, preferred_element_type=F32)
        o_ref[:, hs] = (o * _silu(g_ref[:, hs].astype(F32))).astype(BF16)


def _mem_attn(main, y, mk, mv, b_off, bsz, length, n_mem, tq):
    nq = length // tq
    m = bsz * length
    return pl.pallas_call(
        functools.partial(_mem_attn_kernel, n_mem=n_mem),
        grid=(bsz, nq),
        in_specs=[
            pl.BlockSpec((tq, D_MEM), lambda b, qi: (b * nq + qi, COL_QM // D_MEM)),
            pl.BlockSpec((tq, D_MEM), lambda b, qi: (b * nq + qi, COL_GM // D_MEM)),
            pl.BlockSpec((n_mem * ROW_TILES, LANE), lambda b, qi: (b_off + b, 0)),
            pl.BlockSpec((n_mem * ROW_TILES, LANE), lambda b, qi: (b_off + b, 0)),
            pl.BlockSpec(memory_space=pl.ANY),
        ],
        out_specs=pl.BlockSpec((tq, D_MEM), lambda b, qi: (b * nq + qi, (D_SSM + D_DIFF) // D_MEM)),
        out_shape=jax.ShapeDtypeStruct((m, D_MODEL), BF16),
        input_output_aliases={4: 0},
        compiler_params=_params(2),
        name="mem_attn",
    )(main, main, mk, mv, y)


def _out_proj_kernel(y_ref, w_ref, x_ref, nw_ref, o_ref):
    n_tiles = w_ref.shape[0]
    tn = w_ref.shape[2]
    y = y_ref[...]
    for t in range(n_tiles):
        o_ref[:, t * tn:(t + 1) * tn] = jnp.dot(y, w_ref[t], preferred_element_type=F32)
    tm = o_ref.shape[0]
    step = min(NORM_ROWS, tm)
    for r in range(0, tm, step):
        o = o_ref[r:r + step, :]
        o_ref[r:r + step, :] = x_ref[r:r + step, :] + o * _rms_scale(o, NORM_EPS) * nw_ref[...]


def _out_proj(y, w_out, x2d, norm_w, tm):
    m = x2d.shape[0]
    return pl.pallas_call(
        _out_proj_kernel,
        grid=(m // tm,),
        in_specs=[
            pl.BlockSpec((tm, D_MODEL), lambda i: (i, 0)),
            pl.BlockSpec(w_out.shape, lambda i: (0, 0, 0), pipeline_mode=pl.Buffered(1)),
            pl.BlockSpec((tm, D_MODEL), lambda i: (i, 0)),
            pl.BlockSpec((1, D_MODEL), lambda i: (0, 0)),
        ],
        out_specs=pl.BlockSpec((tm, D_MODEL), lambda i: (i, 0)),
        out_shape=jax.ShapeDtypeStruct((m, D_MODEL), F32),
        compiler_params=_params(1, OUT_PROJ_VMEM_LIMIT),
        name="out_proj",
    )(y, w_out, x2d, norm_w)


def _mixer_layer(x2d, bsz, length, layer, depth, kv_all, b_off, mem_off, conv_hist, h0, kv_past, past, mk,
                 mv, n_mem, lp, lambda_init, tiles):
    main, k32, v32, dt = _in_proj(x2d, lp["norm_pre_w"], lp["w_main"], lp["w_dt"], tiles["proj_tm"],
                                  layer, depth, kv_all)
    y, new_conv, h_new = _ssd(main, dt, conv_hist, h0, b_off, lp, bsz, length, tiles["ssd_q"],
                              tiles["ssd_sub"])
    if kv_past is None:
        y = _diff_attn_prompt(main, y, lp, bsz, length, lambda_init, tiles["attn_tq"])
    else:
        y = _diff_attn_decode(main, y, kv_past[0], kv_past[1], b_off, lp, bsz, length, past, lambda_init)
    y = _mem_attn(main, y, mk, mv, mem_off, bsz, length, n_mem, tiles["mem_tq"])
    x_new = _out_proj(y, lp["w_out"], x2d, lp["norm_post_w"], tiles["out_tm"])
    return x_new, new_conv, h_new, (k32, v32)


def _layer_params(l, norm_pre_w, norm_post_w, w_in_t, conv_w, conv_b, dt_bias, a_log, d_skip,
                  ssm_norm_w, lambda_q1, lambda_k1, lambda_q2, lambda_k2, subln_w, mem_norm_w,
                  w_mem_kv, w_out):
    w_main, w_dt = _w_prep(w_in_t, l)
    pad_heads = lambda v: jnp.pad(v, (0, LANE - N_SSM_HEADS)).reshape(1, LANE)
    head_of_channel = jnp.arange(D_SSM) // SSM_HEAD_DIM
    expand = (jnp.arange(LANE)[:, None] == head_of_channel[None, :]).astype(BF16)
    expand = jnp.concatenate([expand, expand], axis=0)
    return {
        "norm_pre_w": norm_pre_w[l].reshape(1, D_MODEL),
        "norm_post_w": norm_post_w[l].reshape(1, D_MODEL),
        "w_main": w_main,
        "w_dt": w_dt,
        "conv_w": conv_w[l],
        "conv_b": conv_b[l].reshape(1, D_CONV),
        "dt_bias": pad_heads(dt_bias[l]),
        "a_log": pad_heads(a_log[l]),
        "d_skip_e": jnp.repeat(d_skip[l], SSM_HEAD_DIM).reshape(1, D_SSM),
        "ssm_norm_w": ssm_norm_w[l].reshape(1, D_SSM),
        "expand": expand,
        "lambda_q1": lambda_q1[l].reshape(1, DIFF_HEAD_DIM),
        "lambda_k1": lambda_k1[l].reshape(1, DIFF_HEAD_DIM),
        "lambda_q2": lambda_q2[l].reshape(1, DIFF_HEAD_DIM),
        "lambda_k2": lambda_k2[l].reshape(1, DIFF_HEAD_DIM),
        "subln_w": subln_w[l].reshape(1, HEAD_W),
        "mem_norm_w": mem_norm_w[l].reshape(1, D_MODEL),
        "w_mem_kv": _layer_bf16_tiles(w_mem_kv, l, D_MEM),
        "w_out": _layer_bf16_tiles(w_out, l, OUT_TN),
    }


OUT_TN = 512


def _tiles(bsz, length):
    m = bsz * length
    return {
        "proj_tm": min(512, m),
        "ssd_q": min(128, length),
        "ssd_sub": 2 if length % 256 == 0 else 1,
        "attn_tq": min(1024, length),
        "mem_tq": min(1024, length),
        "out_tm": min(256, m),
    }


def kernel(x_prompt, x_sample, mem_prompt, cache_conv, state_ssm, cache_k, cache_v, cache_mem_k,
           cache_mem_v, norm_pre_w, norm_post_w, w_in, conv_w, conv_b, dt_bias, a_log, d_skip,
           ssm_norm_w, lambda_q1, lambda_k1, lambda_q2, lambda_k2, subln_w, mem_norm_w, w_mem_kv,
           w_out):
    depth = w_in.shape[0]
    bp, lp_, _ = x_prompt.shape
    bs, ls, _ = x_sample.shape
    n_mem = mem_prompt.shape[1]
    past = cache_k.shape[2]
    xp = x_prompt.reshape(bp * lp_, D_MODEL)
    xs = x_sample.reshape(bs * ls, D_MODEL)
    mem2d = mem_prompt.reshape(bp * n_mem, D_MODEL)
    tiles_p = _tiles(bp, lp_)
    tiles_s = _tiles(bs, ls)
    conv0 = jnp.zeros((bp, CONV_W - 1, D_CONV), F32)
    h_zero = jnp.zeros((bp, N_SSM_HEADS, SSM_HEAD_DIM, D_STATE), F32)
    conv_all = cache_conv.reshape(depth * bs, CONV_W - 1, D_CONV)
    ssm_all = state_ssm.reshape(depth * bs, N_SSM_HEADS, SSM_HEAD_DIM, D_STATE)
    kv_all = (cache_k.reshape(-1, LANE), _to_flat_halves(cache_v))
    mem_k_all = _to_flat_halves(cache_mem_k)
    mem_v_all = _to_flat_halves(cache_mem_v)
    outs = [[] for _ in range(6)]
    kv_p = kv_s = None
    w_in_t = jnp.swapaxes(w_in, 1, 2)
    for l in range(depth):
        lp = _layer_params(l, norm_pre_w, norm_post_w, w_in_t, conv_w, conv_b, dt_bias, a_log, d_skip,
                           ssm_norm_w, lambda_q1, lambda_k1, lambda_q2, lambda_k2, subln_w,
                           mem_norm_w, w_mem_kv, w_out)
        lambda_init = 0.8 - 0.6 * math.exp(-0.3 * l)
        mk_p, mv_p = _mem_kv(mem2d, lp["mem_norm_w"], lp["w_mem_kv"], min(512, bp * n_mem))
        xp, c_p, h_p, kv_p = _mixer_layer(xp, bp, lp_, l, depth, kv_p, 0, 0, conv0, h_zero, None, 0, mk_p,
                                          mv_p, n_mem, lp, lambda_init, tiles_p)
        xs, c_s, h_s, kv_s = _mixer_layer(xs, bs, ls, l, depth, kv_s, l * bs, l * bs, conv_all, ssm_all,
                                          kv_all, past, mem_k_all, mem_v_all, n_mem, lp, lambda_init,
                                          tiles_s)
        for lst, val in zip(outs, (c_p, h_p, mk_p, mv_p, c_s, h_s)):
            lst.append(val)
    st = [jnp.stack(o) for o in outs]
    return (
        xp.reshape(bp, lp_, D_MODEL),
        xs.reshape(bs, ls, D_MODEL),
        st[0],
        st[1],
        kv_p[0].reshape(depth, bp, lp_, N_DIFF_HEADS, 2, DIFF_HEAD_DIM),
        _from_flat_halves(kv_p[1], (depth, bp, lp_, N_DIFF_HEADS, 2 * DIFF_HEAD_DIM)),
        _from_flat_halves(st[2], (depth, bp, n_mem, N_MEM_HEADS, MEM_HEAD_DIM)),
        _from_flat_halves(st[3], (depth, bp, n_mem, N_MEM_HEADS, MEM_HEAD_DIM)),
        st[4],
        st[5],
        kv_s[0].reshape(depth, bs, ls, N_DIFF_HEADS, 2, DIFF_HEAD_DIM),
        _from_flat_halves(kv_s[1], (depth, bs, ls, N_DIFF_HEADS, 2 * DIFF_HEAD_DIM)),
    )
```

```python
import functools
import math

import jax
import jax.numpy as jnp
from jax import lax
from jax.experimental import pallas as pl
from jax.experimental.pallas import tpu as pltpu

F32 = jnp.float32
BF16 = jnp.bfloat16

D_MODEL = 4096
D_SSM = 2048
SSM_HEAD_DIM = 64
N_SSM_HEADS = 32
N_SSM_GROUPS = 4
HEADS_PER_GROUP = N_SSM_HEADS // N_SSM_GROUPS
D_GROUP = D_SSM // N_SSM_GROUPS
D_STATE = 128
CONV_W = 4
D_BC = N_SSM_GROUPS * D_STATE
D_CONV = D_SSM + 2 * D_BC
D_DIFF = 1024
N_DIFF_HEADS = 4
DIFF_HEAD_DIM = 128
D_MEM = 1024
N_MEM_HEADS = 4
MEM_HEAD_DIM = 256
CHUNK = 64
NORM_EPS = 1e-6
SUBLN_EPS = 1e-5

LANE = 128
VMEM_LIMIT = 56 * 1024 * 1024
OUT_PROJ_VMEM_LIMIT = 60 * 1024 * 1024

HEAD_W = 2 * DIFF_HEAD_DIM
COL_KD = 0
COL_VD = COL_KD + D_DIFF
COL_Z = COL_VD + D_DIFF
COL_X = COL_Z + D_SSM
COL_B = COL_X + D_SSM
COL_C = COL_B + D_BC
COL_QD = COL_C + D_BC
COL_GD = COL_QD + D_DIFF
COL_QM = COL_GD + D_DIFF
COL_GM = COL_QM + D_MEM
D_MAIN = COL_GM + D_MEM
PROJ_TN = 1024
K_TILE = COL_KD // PROJ_TN
V_TILE = COL_VD // PROJ_TN


def _params(n_axes, vmem_limit=VMEM_LIMIT):
    return pltpu.CompilerParams(dimension_semantics=("arbitrary",) * n_axes,
                                vmem_limit_bytes=vmem_limit)


def _silu(x):
    return x * jax.nn.sigmoid(x)


def _chunk_of(pos):
    assert CHUNK & (CHUNK - 1) == 0
    return lax.shift_right_logical(pos, jnp.int32(CHUNK.bit_length() - 1))


def _rms_scale(x, eps):
    return lax.rsqrt(jnp.mean(x * x, axis=-1, keepdims=True) + eps)


ROW_TILES = 1024 // LANE
N_HEADS_1024 = 1024 // HEAD_W


def _flat_row(c, halves_split):
    head, half = divmod(c, 2)
    return half * N_HEADS_1024 + head if halves_split else c


def _store_flat(ref, val, halves_split):
    rows = val.shape[0]
    for c in range(ROW_TILES):
        ref[pl.ds(_flat_row(c, halves_split), rows, stride=ROW_TILES), :] = val[:, c * LANE:(c + 1) * LANE]


def _load_flat(ref, rows, col0, width, halves_split):
    parts = [ref[pl.ds(_flat_row(col0 // LANE + c, halves_split), rows, stride=ROW_TILES), :]
             for c in range(width // LANE)]
    return parts[0] if len(parts) == 1 else jnp.concatenate(parts, axis=1)


def _to_flat_halves(a):
    return a.reshape(-1, N_HEADS_1024, 2, LANE).transpose(0, 2, 1, 3).reshape(-1, LANE)


def _from_flat_halves(flat, shape):
    return flat.reshape(-1, 2, N_HEADS_1024, LANE).transpose(0, 2, 1, 3).reshape(shape)


NORM_ROWS = 128


def _norm_rows(x_ref, nw_ref, h_scr):
    tm = x_ref.shape[0]
    step = min(NORM_ROWS, tm)
    for r in range(0, tm, step):
        x = x_ref[r:r + step, :]
        h_scr[r:r + step, :] = (x * _rms_scale(x, NORM_EPS) * nw_ref[...]).astype(BF16)


O_DT = D_SSM + D_CONV
O_Q = O_DT + N_SSM_HEADS
MAIN_PIECES = ((COL_Z, 0, O_DT), (COL_QD, O_Q, D_DIFF), (COL_KD, O_Q + D_DIFF, D_DIFF),
               (COL_VD, O_Q + 2 * D_DIFF, D_DIFF), (COL_GD, O_Q + 3 * D_DIFF, D_DIFF),
               (COL_QM, O_Q + 4 * D_DIFF, D_MEM), (COL_GM, O_Q + 4 * D_DIFF + D_MEM, D_MEM))
PREP_COLS = 256
PREP_PER_TILE = PROJ_TN // PREP_COLS


def _prep_source_columns():
    src = []
    for dst, first, width in sorted(MAIN_PIECES):
        assert dst == len(src) * PREP_COLS
        src += list(range(first, first + width, PREP_COLS))
    assert len(src) * PREP_COLS == D_MAIN
    return jnp.asarray(src, jnp.int32)


def _w_prep_kernel(src_ref, w_ref, wdt_ref, main_ref, dt_ref):
    del src_ref
    main_ref[...] = w_ref[0].T.astype(BF16)

    @pl.when(pl.program_id(0) == 0)
    def _():
        lane = lax.broadcasted_iota(jnp.int32, (1, LANE), 1)
        dt_ref[...] = jnp.where(lane < N_SSM_HEADS, wdt_ref[...].T, 0.0).astype(BF16)


def _w_prep(w_in_t, layer):
    return pl.pallas_call(
        _w_prep_kernel,
        grid_spec=pltpu.PrefetchScalarGridSpec(
            num_scalar_prefetch=1,
            grid=(D_MAIN // PREP_COLS,),
            in_specs=[
                pl.BlockSpec((pl.Element(1), pl.Element(PREP_COLS), pl.Element(D_MODEL)),
                             lambda j, src: (layer, pl.multiple_of(src[j], N_SSM_HEADS), 0)),
                pl.BlockSpec((None, LANE, D_MODEL), lambda j, src: (layer, O_DT // LANE, 0)),
            ],
            out_specs=[
                pl.BlockSpec((None, D_MODEL, PREP_COLS),
                             lambda j, src: (j // PREP_PER_TILE, 0, j % PREP_PER_TILE)),
                pl.BlockSpec((D_MODEL, LANE), lambda j, src: (0, 0)),
            ],
        ),
        out_shape=[
            jax.ShapeDtypeStruct((D_MAIN // PROJ_TN, D_MODEL, PROJ_TN), BF16),
            jax.ShapeDtypeStruct((D_MODEL, LANE), BF16),
        ],
        compiler_params=_params(1),
        name="w_prep",
    )(_prep_source_columns(), w_in_t, w_in_t)


CAST_ROWS = 1024


def _cast_kernel(w_ref, o_ref):
    o_ref[...] = w_ref[...].astype(BF16)


def _layer_bf16_tiles(w, layer, tn):
    _, rows, cols = w.shape
    return pl.pallas_call(
        _cast_kernel,
        grid=(rows // CAST_ROWS, cols // tn),
        in_specs=[pl.BlockSpec((None, CAST_ROWS, tn), lambda i, t: (layer, i, t))],
        out_specs=pl.BlockSpec((None, CAST_ROWS, tn), lambda i, t: (t, i, 0)),
        out_shape=jax.ShapeDtypeStruct((cols // tn, rows, tn), BF16),
        compiler_params=_params(2),
        name="cast_bf16",
    )(w)


def _in_proj_kernel(x_ref, nw_ref, w_ref, wdt_ref, *refs):
    main_ref, k_ref, v_ref, dt_ref, h_scr = refs[-5:]
    j = pl.program_id(1)

    @pl.when(j == 0)
    def _():
        _norm_rows(x_ref, nw_ref, h_scr)
        dt_ref[...] = jnp.dot(h_scr[...], wdt_ref[...], preferred_element_type=F32)

    acc = jnp.dot(h_scr[...], w_ref[...], preferred_element_type=F32)
    main_ref[...] = acc.astype(BF16)

    @pl.when(j == K_TILE)
    def _():
        _store_flat(k_ref, acc, halves_split=False)

    @pl.when(j == V_TILE)
    def _():
        _store_flat(v_ref, acc, halves_split=True)


def _in_proj(x2d, norm_w, w_main, w_dt, tm, layer, depth, kv_all):
    m = x2d.shape[0]
    ni = m // tm
    grid = (ni, D_MAIN // PROJ_TN)
    kv_spec = pl.BlockSpec((tm * ROW_TILES, LANE), lambda i, j: (layer * ni + i, 0))
    kv_shape = jax.ShapeDtypeStruct((depth * m * ROW_TILES, LANE), F32)
    in_specs = [
        pl.BlockSpec((tm, D_MODEL), lambda i, j: (i, 0)),
        pl.BlockSpec((1, D_MODEL), lambda i, j: (0, 0)),
        pl.BlockSpec((None, D_MODEL, PROJ_TN), lambda i, j: (j, 0, 0)),
        pl.BlockSpec((D_MODEL, LANE), lambda i, j: (0, 0)),
    ]
    operands = [x2d, norm_w, w_main, w_dt]
    aliases = {}
    if kv_all is not None:
        in_specs += [pl.BlockSpec(memory_space=pl.ANY)] * 2
        aliases = {len(operands): 1, len(operands) + 1: 2}
        operands += list(kv_all)
    return pl.pallas_call(
        _in_proj_kernel,
        grid=grid,
        in_specs=in_specs,
        out_specs=[
            pl.BlockSpec((tm, PROJ_TN), lambda i, j: (i, j)),
            kv_spec,
            kv_spec,
            pl.BlockSpec((tm, LANE), lambda i, j: (i, 0)),
        ],
        out_shape=[
            jax.ShapeDtypeStruct((m, D_MAIN), BF16),
            kv_shape,
            kv_shape,
            jax.ShapeDtypeStruct((m, LANE), F32),
        ],
        scratch_shapes=[pltpu.VMEM((tm, D_MODEL), BF16)],
        input_output_aliases=aliases,
        compiler_params=_params(2),
        name="in_proj",
    )(*operands)


def _mem_kv_kernel(x_ref, nw_ref, w_ref, mk_ref, mv_ref, h_scr):
    j = pl.program_id(1)

    @pl.when(j == 0)
    def _():
        _norm_rows(x_ref, nw_ref, h_scr)

    acc = jnp.dot(h_scr[...], w_ref[...], preferred_element_type=F32)

    @pl.when(j == 0)
    def _():
        _store_flat(mk_ref, acc, halves_split=True)

    @pl.when(j == 1)
    def _():
        _store_flat(mv_ref, acc, halves_split=True)


def _mem_kv(mem2d, norm_w, w_kv, tm):
    m = mem2d.shape[0]
    return pl.pallas_call(
        _mem_kv_kernel,
        grid=(m // tm, 2),
        in_specs=[
            pl.BlockSpec((tm, D_MODEL), lambda i, j: (i, 0)),
            pl.BlockSpec((1, D_MODEL), lambda i, j: (0, 0)),
            pl.BlockSpec((None, D_MODEL, D_MEM), lambda i, j: (j, 0, 0)),
        ],
        out_specs=[
            pl.BlockSpec((tm * ROW_TILES, LANE), lambda i, j: (i, 0)),
            pl.BlockSpec((tm * ROW_TILES, LANE), lambda i, j: (i, 0)),
        ],
        out_shape=[jax.ShapeDtypeStruct((m * ROW_TILES, LANE), F32)] * 2,
        scratch_shapes=[pltpu.VMEM((tm, D_MODEL), BF16)],
        compiler_params=_params(2),
        name="mem_kv",
    )(mem2d, norm_w, w_kv)


def _split_bf16(x, n):
    parts = []
    r = x
    for _ in range(n - 1):
        p = r.astype(BF16)
        parts.append(p)
        r = r - p.astype(F32)
    parts.append(r.astype(BF16))
    return parts


def _sum_rows(a, n, rows):
    out = a[(n - 1) * rows:n * rows]
    for i in range(n - 2, -1, -1):
        out = out + a[i * rows:(i + 1) * rows]
    return out


HIST_ROWS = 16


def _conv_shift_matrices(q):
    t = jnp.arange(q)[:, None]
    col = jnp.arange(2 * HIST_ROWS + q)[None, :]
    with_terms, plain = [], []
    for j in range(CONV_W - 1):
        src = t - (CONV_W - 1 - j)
        band = col == 2 * HIST_ROWS + src
        plain.append(band)
        with_terms.append(band | ((src < 0) & (col == HIST_ROWS + src)))
    return jnp.stack([jnp.concatenate(with_terms, axis=0), jnp.concatenate(plain, axis=0)]).astype(BF16)


def _ssd_chunk(win_ref, shift, z_ref, dt_ref, y_ref, y_scr, st_scr,
               cw_ref, cb_ref, dtb_ref, alog_ref, dskip_ref, nw_ref, e_ref, q):
    taps = jnp.dot(shift, win_ref[...], preferred_element_type=F32)
    acc = cb_ref[...] + taps[0:q] * cw_ref[0:1, :]
    for j in range(1, CONV_W - 1):
        acc = acc + taps[j * q:(j + 1) * q] * cw_ref[j:j + 1, :]
    cur = win_ref[2 * HIST_ROWS:2 * HIST_ROWS + q, :].astype(F32)
    acc = acc + cur * cw_ref[CONV_W - 1:CONV_W, :]
    xbc = _silu(acc)
    xs = xbc[:, 0:D_SSM]
    xs_bf = xs.astype(BF16)
    bm = xbc[:, D_SSM:D_SSM + D_BC].astype(BF16)
    cm = xbc[:, D_SSM + D_BC:D_CONV].astype(BF16)

    dtv = jax.nn.softplus(dt_ref[...] + dtb_ref[...])
    a = -jnp.exp(alog_ref[...])
    da = dtv * a
    ri = lax.broadcasted_iota(jnp.int32, (q, q), 0)
    ci = lax.broadcasted_iota(jnp.int32, (q, q), 1)
    causal = ri >= ci
    tril = jnp.where(causal, 1.0, 0.0).astype(BF16)
    triu = jnp.where(ri <= ci, 1.0, 0.0).astype(BF16)
    eye = jnp.where(ri == ci, 1.0, 0.0).astype(BF16)
    da3 = jnp.concatenate(_split_bf16(da, 3), axis=1)
    dt3 = jnp.concatenate(_split_bf16(dtv, 3), axis=1)
    cum3 = jnp.dot(tril, da3, preferred_element_type=F32)
    cum = cum3[:, 2 * LANE:3 * LANE] + cum3[:, LANE:2 * LANE] + cum3[:, 0:LANE]
    tn_dims = (((0,), (0,)), ((), ()))
    cum_t = _sum_rows(lax.dot_general(da3, triu, tn_dims, preferred_element_type=F32), 3, LANE)
    dt_t = _sum_rows(lax.dot_general(dt3, eye, tn_dims, preferred_element_type=F32), 3, LANE)

    ecum = jnp.exp(cum)
    dd = dtv * jnp.exp(cum[q - 1:q, :] - cum)
    ex_in = jnp.concatenate([jnp.concatenate(_split_bf16(ecum, 2), axis=1),
                             jnp.concatenate(_split_bf16(dd, 2), axis=1)], axis=0)
    ex = jnp.dot(ex_in, e_ref[...], preferred_element_type=F32)
    ecum_e = ex[0:q]
    dd_e = ex[q:2 * q]
    xdd = (xs * dd_e).astype(BF16)

    lane = lax.broadcasted_iota(jnp.int32, (q, LANE), 1)
    lo_half = lane < SSM_HEAD_DIM
    nt_dims = (((1,), (1,)), ((), ()))
    for g in range(N_SSM_GROUPS):
        bg = bm[:, g * D_STATE:(g + 1) * D_STATE]
        cg = cm[:, g * D_STATE:(g + 1) * D_STATE]
        gs = slice(g * D_GROUP, (g + 1) * D_GROUP)
        cbg = lax.dot_general(cg, bg, nt_dims, preferred_element_type=F32)
        st = st_scr[g]
        y_off = jnp.dot(cg, st.astype(BF16), preferred_element_type=F32) * ecum_e[:, gs]
        st_scr[g] = st * ecum_e[q - 1:q, gs] + lax.dot_general(
            bg, xdd[:, gs], tn_dims, preferred_element_type=F32)
        for t in range(HEADS_PER_GROUP // 4):
            ws, xrows = [], []
            for k in range(4):
                h = g * HEADS_PER_GROUP + 4 * t + k
                seg = cum[:, h:h + 1] - cum_t[h:h + 1, :]
                lmat = jnp.exp(jnp.where(causal, seg, -jnp.inf))
                ws.append((cbg * lmat * dt_t[h:h + 1, :]).astype(BF16))
                pair = xs_bf[:, g * D_GROUP + (2 * t + k // 2) * LANE:g * D_GROUP + (2 * t + k // 2 + 1) * LANE]
                zero = jnp.zeros_like(pair)
                half = jnp.where(lo_half, pair, zero) if k % 2 == 0 else jnp.where(lo_half, zero, pair)
                xrows.append(jnp.concatenate([half, zero] if k < 2 else [zero, half], axis=1))
            cs = slice(g * D_GROUP + 2 * t * LANE, g * D_GROUP + (2 * t + 2) * LANE)
            y_diag = jnp.dot(jnp.concatenate(ws, axis=1), jnp.concatenate(xrows, axis=0),
                             preferred_element_type=F32)
            y_scr[:, cs] = (y_diag + y_off[:, 2 * t * LANE:(2 * t + 2) * LANE]) + xs[:, cs] * dskip_ref[:, cs]

    for g in range(N_SSM_GROUPS):
        gs = slice(g * D_GROUP, (g + 1) * D_GROUP)
        yg = y_scr[:, gs] * _silu(z_ref[:, gs].astype(F32))
        y_ref[:, gs] = (yg * _rms_scale(yg, NORM_EPS) * nw_ref[:, gs]).astype(BF16)
    return cur


def _ssd_kernel(z_ref, xx_ref, xb_ref, xc_ref, dt_ref, hist_ref, h0_ref,
                cw_ref, cb_ref, dtb_ref, alog_ref, dskip_ref, nw_ref, e_ref, shift_ref,
                y_ref, nconv_ref, hnew_ref, xp_scr, st_scr, y_scr, *, q, n_sub):
    c = pl.program_id(1)
    nc = pl.num_programs(1)
    rows = n_sub * q

    @pl.when(c == 0)
    def _():
        hist = jnp.concatenate(
            [jnp.zeros((HIST_ROWS - (CONV_W - 1), D_CONV), F32), hist_ref[0]], axis=0)
        hi, lo = _split_bf16(hist, 2)
        xp_scr[0:HIST_ROWS, :] = hi
        xp_scr[HIST_ROWS:2 * HIST_ROWS, :] = lo
        for g in range(N_SSM_GROUPS):
            hg = h0_ref[0, g * HEADS_PER_GROUP:(g + 1) * HEADS_PER_GROUP]
            st_scr[g] = hg.reshape(D_GROUP, D_STATE).T

    blk = slice(2 * HIST_ROWS, 2 * HIST_ROWS + rows)
    xp_scr[blk, 0:D_SSM] = xx_ref[...]
    xp_scr[blk, D_SSM:D_SSM + D_BC] = xb_ref[...]
    xp_scr[blk, D_SSM + D_BC:D_CONV] = xc_ref[...]
    for sub in range(n_sub):
        r = slice(sub * q, (sub + 1) * q)
        cur = _ssd_chunk(xp_scr.at[sub * q:sub * q + 2 * HIST_ROWS + q, :], shift_ref[min(sub, 1)],
                         z_ref.at[r, :], dt_ref.at[r, :], y_ref.at[r, :], y_scr.at[r, :], st_scr,
                         cw_ref, cb_ref, dtb_ref, alog_ref, dskip_ref, nw_ref, e_ref, q)

    @pl.when(c == nc - 1)
    def _():
        nconv_ref[0] = cur[q - (CONV_W - 1):q]
        for g in range(N_SSM_GROUPS):
            hnew_ref[0, g * HEADS_PER_GROUP:(g + 1) * HEADS_PER_GROUP] = (
                st_scr[g].T.reshape(HEADS_PER_GROUP, SSM_HEAD_DIM, D_STATE))

    xp_scr[0:HIST_ROWS, :] = xp_scr[2 * HIST_ROWS + rows - HIST_ROWS:2 * HIST_ROWS + rows, :]
    xp_scr[HIST_ROWS:2 * HIST_ROWS, :] = jnp.zeros((HIST_ROWS, D_CONV), BF16)


def _ssd(main, dt, conv_hist, h0, b_off, lp, bsz, length, q, n_sub):
    chunk, q = q, n_sub * q
    nc = length // q
    m = bsz * length
    row = lambda b, c: b * nc + c
    return pl.pallas_call(
        functools.partial(_ssd_kernel, q=chunk, n_sub=n_sub),
        grid=(bsz, nc),
        in_specs=[
            pl.BlockSpec((q, D_SSM), lambda b, c: (row(b, c), COL_Z // D_SSM)),
            pl.BlockSpec((q, D_SSM), lambda b, c: (row(b, c), COL_X // D_SSM)),
            pl.BlockSpec((q, D_BC), lambda b, c: (row(b, c), COL_B // D_BC)),
            pl.BlockSpec((q, D_BC), lambda b, c: (row(b, c), COL_C // D_BC)),
            pl.BlockSpec((q, LANE), lambda b, c: (row(b, c), 0)),
            pl.BlockSpec((1, CONV_W - 1, D_CONV), lambda b, c: (b_off + b, 0, 0)),
            pl.BlockSpec((1, N_SSM_HEADS, SSM_HEAD_DIM, D_STATE), lambda b, c: (b_off + b, 0, 0, 0)),
            pl.BlockSpec((CONV_W, D_CONV), lambda b, c: (0, 0)),
            pl.BlockSpec((1, D_CONV), lambda b, c: (0, 0)),
            pl.BlockSpec((1, LANE), lambda b, c: (0, 0)),
            pl.BlockSpec((1, LANE), lambda b, c: (0, 0)),
            pl.BlockSpec((1, D_SSM), lambda b, c: (0, 0)),
            pl.BlockSpec((1, D_SSM), lambda b, c: (0, 0)),
            pl.BlockSpec((2 * LANE, D_SSM), lambda b, c: (0, 0)),
            pl.BlockSpec((2, (CONV_W - 1) * chunk, 2 * HIST_ROWS + chunk), lambda b, c: (0, 0, 0)),
        ],
        out_specs=[
            pl.BlockSpec((q, D_SSM), lambda b, c: (row(b, c), 0)),
            pl.BlockSpec((1, CONV_W - 1, D_CONV), lambda b, c: (b, 0, 0)),
            pl.BlockSpec((1, N_SSM_HEADS, SSM_HEAD_DIM, D_STATE), lambda b, c: (b, 0, 0, 0)),
        ],
        out_shape=[
            jax.ShapeDtypeStruct((m, D_MODEL), BF16),
            jax.ShapeDtypeStruct((bsz, CONV_W - 1, D_CONV), F32),
            jax.ShapeDtypeStruct((bsz, N_SSM_HEADS, SSM_HEAD_DIM, D_STATE), F32),
        ],
        scratch_shapes=[
            pltpu.VMEM((2 * HIST_ROWS + q, D_CONV), BF16),
            pltpu.VMEM((N_SSM_GROUPS, D_STATE, D_GROUP), F32),
            pltpu.VMEM((q, D_SSM), F32),
        ],
        compiler_params=_params(2),
        name="ssd",
    )(main, main, main, main, dt, conv_hist, h0,
      lp["conv_w"], lp["conv_b"], lp["dt_bias"], lp["a_log"], lp["d_skip_e"], lp["ssm_norm_w"],
      lp["expand"], _conv_shift_matrices(chunk))


def _lambda(lq1, lk1, lq2, lk2, lambda_init):
    return (jnp.exp(jnp.sum(lq1[...] * lk1[...], axis=-1, keepdims=True))
            - jnp.exp(jnp.sum(lq2[...] * lk2[...], axis=-1, keepdims=True)) + lambda_init)


def _diff_finish(o1, o2, lam, sw_ref, g, lambda_init):
    o = o1 - lam * o2
    on = (o * _rms_scale(o, SUBLN_EPS) * sw_ref[...]) * (1.0 - lambda_init)
    return (on * _silu(g.astype(F32))).astype(BF16)


ATTN_QT = 256
ATTN_HEADS = 2


def _attn_scores(k_ref, q_ref, mp, nt, diagonal):
    cols = slice(mp * DIFF_HEAD_DIM, (mp + 1) * DIFF_HEAD_DIM)
    qs = slice(nt * ATTN_QT, (nt + 1) * ATTN_QT)
    kw = (nt + 1) * ATTN_QT if diagonal else k_ref.shape[0]
    s = lax.dot_general(k_ref[0:kw, cols], q_ref[qs, cols], (((1,), (1,)), ((), ())),
                        preferred_element_type=F32)
    if diagonal:
        r = lax.broadcasted_iota(jnp.int32, (ATTN_QT, ATTN_QT), 0)
        q = lax.broadcasted_iota(jnp.int32, (ATTN_QT, ATTN_QT), 1)
        tail = jnp.where(_chunk_of(r) <= _chunk_of(q), s[kw - ATTN_QT:kw], -jnp.inf)
        s = tail if kw == ATTN_QT else jnp.concatenate([s[0:kw - ATTN_QT], tail], axis=0)
    return s


def _attn_softmax(s, m_scr, l_scr, mp, nt):
    c = (DIFF_HEAD_DIM ** -0.5) * math.log2(math.e)
    qs = slice(nt * ATTN_QT, (nt + 1) * ATTN_QT)
    m_prev = m_scr[mp, :, qs]
    m_new = jnp.maximum(m_prev, jnp.max(s, axis=0, keepdims=True))
    p = jnp.exp2((s - m_new) * c)
    alpha = jnp.exp2((m_prev - m_new) * c)
    l_scr[mp, :, qs] = alpha * l_scr[mp, :, qs] + jnp.sum(p, axis=0, keepdims=True)
    m_scr[mp, :, qs] = m_new
    return p.astype(BF16), alpha


def _attn_value(vt, p, alpha, acc_scr, mp, nt):
    qs = slice(nt * ATTN_QT, (nt + 1) * ATTN_QT)
    kw = p.shape[0]
    acc_scr[mp, :, qs] = alpha * acc_scr[mp, :, qs] + jnp.dot(vt[:, 0:kw], p, preferred_element_type=F32)


def _diff_attn_kernel(qi_ref, ki_ref, q_ref, k_ref, v_ref, g_ref, lq1, lk1, lq2, lk2, sw_ref, y_hbm,
                      o_ref, m_scr, l_scr, acc_scr, *, tq, tk, lambda_init):
    del y_hbm
    qi = qi_ref[pl.program_id(2)]
    ki = ki_ref[pl.program_id(2)]
    assert tq == tk and tq % ATTN_QT == 0 and ATTN_QT % CHUNK == 0

    @pl.when(ki == 0)
    def _():
        m_scr[...] = jnp.full(m_scr.shape, -jnp.inf, F32)
        l_scr[...] = jnp.zeros(l_scr.shape, F32)
        acc_scr[...] = jnp.zeros(acc_scr.shape, F32)

    def block(diagonal):
        vts = [v_ref[:, hl * HEAD_W:(hl + 1) * HEAD_W].T for hl in range(ATTN_HEADS)]
        chains = [(mp, nt) for mp in range(2 * ATTN_HEADS) for nt in range(tq // ATTN_QT)]
        scores = [_attn_scores(k_ref, q_ref, mp, nt, diagonal) for mp, nt in chains]
        probs = [_attn_softmax(s, m_scr, l_scr, mp, nt) for s, (mp, nt) in zip(scores, chains)]
        for (p, alpha), (mp, nt) in zip(probs, chains):
            _attn_value(vts[mp // 2], p, alpha, acc_scr, mp, nt)

    @pl.when(ki < qi)
    def _():
        block(False)

    @pl.when(ki == qi)
    def _():
        block(True)
        lam = _lambda(lq1, lk1, lq2, lk2, lambda_init)
        for hl in range(ATTN_HEADS):
            hs = slice(hl * HEAD_W, (hl + 1) * HEAD_W)
            o1 = (acc_scr[2 * hl] / l_scr[2 * hl]).T
            o2 = (acc_scr[2 * hl + 1] / l_scr[2 * hl + 1]).T
            o_ref[:, hs] = _diff_finish(o1, o2, lam, sw_ref, g_ref[:, hs], lambda_init)


def _diff_attn_prompt(main, y, lp, bsz, length, lambda_init, tq):
    tk = tq
    nq = length // tq
    m = bsz * length
    wb = ATTN_HEADS * HEAD_W
    pairs = [(qi, ki) for qi in range(nq) for ki in range(qi + 1)]
    qi_tab = jnp.asarray([p[0] for p in pairs], jnp.int32)
    ki_tab = jnp.asarray([p[1] for p in pairs], jnp.int32)
    vec = pl.BlockSpec((1, DIFF_HEAD_DIM), lambda b, h, t, qt, kt: (0, 0))
    return pl.pallas_call(
        functools.partial(_diff_attn_kernel, tq=tq, tk=tk, lambda_init=lambda_init),
        grid_spec=pltpu.PrefetchScalarGridSpec(
            num_scalar_prefetch=2,
            grid=(bsz, N_DIFF_HEADS // ATTN_HEADS, len(pairs)),
            in_specs=[
                pl.BlockSpec((tq, wb), lambda b, h, t, qt, kt: (b * nq + qt[t], COL_QD // wb + h)),
                pl.BlockSpec((tk, wb), lambda b, h, t, qt, kt: (b * nq + kt[t], COL_KD // wb + h)),
                pl.BlockSpec((tk, wb), lambda b, h, t, qt, kt: (b * nq + kt[t], COL_VD // wb + h)),
                pl.BlockSpec((tq, wb), lambda b, h, t, qt, kt: (b * nq + qt[t], COL_GD // wb + h)),
                vec, vec, vec, vec,
                pl.BlockSpec((1, HEAD_W), lambda b, h, t, qt, kt: (0, 0)),
                pl.BlockSpec(memory_space=pl.ANY),
            ],
            out_specs=pl.BlockSpec((tq, wb), lambda b, h, t, qt, kt: (b * nq + qt[t], D_SSM // wb + h)),
            scratch_shapes=[
                pltpu.VMEM((2 * ATTN_HEADS, 1, tq), F32),
                pltpu.VMEM((2 * ATTN_HEADS, 1, tq), F32),
                pltpu.VMEM((2 * ATTN_HEADS, HEAD_W, tq), F32),
            ],
        ),
        out_shape=jax.ShapeDtypeStruct((m, D_MODEL), BF16),
        input_output_aliases={11: 0},
        compiler_params=_params(3),
        name="diff_attn_prompt",
    )(qi_tab, ki_tab, main, main, main, main, lp["lambda_q1"], lp["lambda_k1"], lp["lambda_q2"],
      lp["lambda_k2"], lp["subln_w"], y)


def _diff_attn_decode_kernel(q_ref, kn_ref, vn_ref, g_ref, kp_ref, vp_ref, lq1, lk1, lq2, lk2, sw_ref,
                             y_hbm, o_ref, *, past, lambda_init):
    del y_hbm
    scale = DIFF_HEAD_DIM ** -0.5
    lq = q_ref.shape[0]
    nt_dims = (((1,), (1,)), ((), ()))
    mask_p = (_chunk_of(lax.broadcasted_iota(jnp.int32, (lq, past), 1))
              <= _chunk_of(past + lax.broadcasted_iota(jnp.int32, (lq, past), 0)))
    mask_n = (_chunk_of(past + lax.broadcasted_iota(jnp.int32, (lq, lq), 1))
              <= _chunk_of(past + lax.broadcasted_iota(jnp.int32, (lq, lq), 0)))
    lam = _lambda(lq1, lk1, lq2, lk2, lambda_init)
    for h in range(N_DIFF_HEADS):
        hs = slice(h * HEAD_W, (h + 1) * HEAD_W)
        vp = _load_flat(vp_ref, past, h * HEAD_W, HEAD_W, halves_split=True).astype(BF16)
        vn = vn_ref[:, hs]
        outs = []
        for mp in range(2):
            c0 = h * HEAD_W + mp * DIFF_HEAD_DIM
            qm = q_ref[:, c0:c0 + DIFF_HEAD_DIM]
            kp = _load_flat(kp_ref, past, c0, DIFF_HEAD_DIM, halves_split=False).astype(BF16)
            sp = lax.dot_general(qm, kp, nt_dims, preferred_element_type=F32) * scale
            sn = lax.dot_general(qm, kn_ref[:, c0:c0 + DIFF_HEAD_DIM], nt_dims,
                                 preferred_element_type=F32) * scale
            sp = jnp.where(mask_p, sp, -jnp.inf)
            sn = jnp.where(mask_n, sn, -jnp.inf)
            mx = jnp.maximum(jnp.max(sp, axis=-1, keepdims=True), jnp.max(sn, axis=-1, keepdims=True))
            pp = jnp.exp(sp - mx)
            pn = jnp.exp(sn - mx)
            denom = jnp.sum(pp, axis=-1, keepdims=True) + jnp.sum(pn, axis=-1, keepdims=True)
            o = (jnp.dot(pp.astype(BF16), vp, preferred_element_type=F32)
                 + jnp.dot(pn.astype(BF16), vn, preferred_element_type=F32))
            outs.append(o / denom)
        o_ref[:, hs] = _diff_finish(outs[0], outs[1], lam, sw_ref, g_ref[:, hs], lambda_init)


def _diff_attn_decode(main, y, k_past, v_past, b_off, lp, bsz, length, past, lambda_init):
    m = bsz * length
    vec = pl.BlockSpec((1, DIFF_HEAD_DIM), lambda b: (0, 0))
    return pl.pallas_call(
        functools.partial(_diff_attn_decode_kernel, past=past, lambda_init=lambda_init),
        grid=(bsz,),
        in_specs=[
            pl.BlockSpec((length, D_DIFF), lambda b: (b, COL_QD // D_DIFF)),
            pl.BlockSpec((length, D_DIFF), lambda b: (b, COL_KD // D_DIFF)),
            pl.BlockSpec((length, D_DIFF), lambda b: (b, COL_VD // D_DIFF)),
            pl.BlockSpec((length, D_DIFF), lambda b: (b, COL_GD // D_DIFF)),
            pl.BlockSpec((past * ROW_TILES, LANE), lambda b: (b_off + b, 0)),
            pl.BlockSpec((past * ROW_TILES, LANE), lambda b: (b_off + b, 0)),
            vec, vec, vec, vec,
            pl.BlockSpec((1, HEAD_W), lambda b: (0, 0)),
            pl.BlockSpec(memory_space=pl.ANY),
        ],
        out_specs=pl.BlockSpec((length, D_DIFF), lambda b: (b, D_SSM // D_DIFF)),
        out_shape=jax.ShapeDtypeStruct((m, D_MODEL), BF16),
        input_output_aliases={11: 0},
        compiler_params=_params(1),
        name="diff_attn_decode",
    )(main, main, main, main, k_past, v_past,
      lp["lambda_q1"], lp["lambda_k1"], lp["lambda_q2"], lp["lambda_k2"], lp["subln_w"], y)


def _mem_attn_kernel(q_ref, g_ref, mk_ref, mv_ref, y_hbm, o_ref, *, n_mem):
    del y_hbm
    scale = MEM_HEAD_DIM ** -0.5
    for h in range(N_MEM_HEADS):
        hs = slice(h * MEM_HEAD_DIM, (h + 1) * MEM_HEAD_DIM)
        mk = _load_flat(mk_ref, n_mem, h * MEM_HEAD_DIM, MEM_HEAD_DIM, halves_split=True).astype(BF16)
        mv = _load_flat(mv_ref, n_mem, h * MEM_HEAD_DIM, MEM_HEAD_DIM, halves_split=True).astype(BF16)
        s = lax.dot_general(q_ref[:, hs], mk, (((1,), (1,)), ((), ())), preferred_element_type=F32) * scale
        p = jnp.exp(s - jnp.max(s, axis=-1, keepdims=True))
        pr = p / jnp.sum(p, axis=-1, keepdims=True)
        o = jnp.dot(pr.astype(BF16), mv, preferred_element_type=F32)
        o_ref[:, hs] = (o * _silu(g_ref[:, hs].astype(F32))).astype(BF16)


def _mem_attn(main, y, mk, mv, b_off, bsz, length, n_mem, tq):
    nq = length // tq
    m = bsz * length
    return pl.pallas_call(
        functools.partial(_mem_attn_kernel, n_mem=n_mem),
        grid=(bsz, nq),
        in_specs=[
            pl.BlockSpec((tq, D_MEM), lambda b, qi: (b * nq + qi, COL_QM // D_MEM)),
            pl.BlockSpec((tq, D_MEM), lambda b, qi: (b * nq + qi, COL_GM // D_MEM)),
            pl.BlockSpec((n_mem * ROW_TILES, LANE), lambda b, qi: (b_off + b, 0)),
            pl.BlockSpec((n_mem * ROW_TILES, LANE), lambda b, qi: (b_off + b, 0)),
            pl.BlockSpec(memory_space=pl.ANY),
        ],
        out_specs=pl.BlockSpec((tq, D_MEM), lambda b, qi: (b * nq + qi, (D_SSM + D_DIFF) // D_MEM)),
        out_shape=jax.ShapeDtypeStruct((m, D_MODEL), BF16),
        input_output_aliases={4: 0},
        compiler_params=_params(2),
        name="mem_attn",
    )(main, main, mk, mv, y)


def _out_proj_kernel(y_ref, w_ref, x_ref, nw_ref, o_ref):
    n_tiles = w_ref.shape[0]
    tn = w_ref.shape[2]
    y = y_ref[...]
    for t in range(n_tiles):
        o_ref[:, t * tn:(t + 1) * tn] = jnp.dot(y, w_ref[t], preferred_element_type=F32)
    tm = o_ref.shape[0]
    step = min(NORM_ROWS, tm)
    for r in range(0, tm, step):
        o = o_ref[r:r + step, :]
        o_ref[r:r + step, :] = x_ref[r:r + step, :] + o * _rms_scale(o, NORM_EPS) * nw_ref[...]


def _out_proj(y, w_out, x2d, norm_w, tm):
    m = x2d.shape[0]
    return pl.pallas_call(
        _out_proj_kernel,
        grid=(m // tm,),
        in_specs=[
            pl.BlockSpec((tm, D_MODEL), lambda i: (i, 0)),
            pl.BlockSpec(w_out.shape, lambda i: (0, 0, 0), pipeline_mode=pl.Buffered(1)),
            pl.BlockSpec((tm, D_MODEL), lambda i: (i, 0)),
            pl.BlockSpec((1, D_MODEL), lambda i: (0, 0)),
        ],
        out_specs=pl.BlockSpec((tm, D_MODEL), lambda i: (i, 0)),
        out_shape=jax.ShapeDtypeStruct((m, D_MODEL), F32),
        compiler_params=_params(1, OUT_PROJ_VMEM_LIMIT),
        name="out_proj",
    )(y, w_out, x2d, norm_w)


def _mixer_layer(x2d, bsz, length, layer, depth, kv_all, b_off, mem_off, conv_hist, h0, kv_past, past, mk,
                 mv, n_mem, lp, lambda_init, tiles):
    main, k32, v32, dt = _in_proj(x2d, lp["norm_pre_w"], lp["w_main"], lp["w_dt"], tiles["proj_tm"],
                                  layer, depth, kv_all)
    y, new_conv, h_new = _ssd(main, dt, conv_hist, h0, b_off, lp, bsz, length, tiles["ssd_q"],
                              tiles["ssd_sub"])
    if kv_past is None:
        y = _diff_attn_prompt(main, y, lp, bsz, length, lambda_init, tiles["attn_tq"])
    else:
        y = _diff_attn_decode(main, y, kv_past[0], kv_past[1], b_off, lp, bsz, length, past, lambda_init)
    y = _mem_attn(main, y, mk, mv, mem_off, bsz, length, n_mem, tiles["mem_tq"])
    x_new = _out_proj(y, lp["w_out"], x2d, lp["norm_post_w"], tiles["out_tm"])
    return x_new, new_conv, h_new, (k32, v32)


def _layer_params(l, norm_pre_w, norm_post_w, w_in_t, conv_w, conv_b, dt_bias, a_log, d_skip,
                  ssm_norm_w, lambda_q1, lambda_k1, lambda_q2, lambda_k2, subln_w, mem_norm_w,
                  w_mem_kv, w_out):
    w_main, w_dt = _w_prep(w_in_t, l)
    pad_heads = lambda v: jnp.pad(v, (0, LANE - N_SSM_HEADS)).reshape(1, LANE)
    head_of_channel = jnp.arange(D_SSM) // SSM_HEAD_DIM
    expand = (jnp.arange(LANE)[:, None] == head_of_channel[None, :]).astype(BF16)
    expand = jnp.concatenate([expand, expand], axis=0)
    return {
        "norm_pre_w": norm_pre_w[l].reshape(1, D_MODEL),
        "norm_post_w": norm_post_w[l].reshape(1, D_MODEL),
        "w_main": w_main,
        "w_dt": w_dt,
        "conv_w": conv_w[l],
        "conv_b": conv_b[l].reshape(1, D_CONV),
        "dt_bias": pad_heads(dt_bias[l]),
        "a_log": pad_heads(a_log[l]),
        "d_skip_e": jnp.repeat(d_skip[l], SSM_HEAD_DIM).reshape(1, D_SSM),
        "ssm_norm_w": ssm_norm_w[l].reshape(1, D_SSM),
        "expand": expand,
        "lambda_q1": lambda_q1[l].reshape(1, DIFF_HEAD_DIM),
        "lambda_k1": lambda_k1[l].reshape(1, DIFF_HEAD_DIM),
        "lambda_q2": lambda_q2[l].reshape(1, DIFF_HEAD_DIM),
        "lambda_k2": lambda_k2[l].reshape(1, DIFF_HEAD_DIM),
        "subln_w": subln_w[l].reshape(1, HEAD_W),
        "mem_norm_w": mem_norm_w[l].reshape(1, D_MODEL),
        "w_mem_kv": _layer_bf16_tiles(w_mem_kv, l, D_MEM),
        "w_out": _layer_bf16_tiles(w_out, l, OUT_TN),
    }


OUT_TN = 512


def _tiles(bsz, length):
    m = bsz * length
    return {
        "proj_tm": min(512, m),
        "ssd_q": min(128, length),
        "ssd_sub": 4 if length % 512 == 0 else 1,
        "attn_tq": min(1024, length),
        "mem_tq": min(1024, length),
        "out_tm": min(256, m),
    }


def kernel(x_prompt, x_sample, mem_prompt, cache_conv, state_ssm, cache_k, cache_v, cache_mem_k,
           cache_mem_v, norm_pre_w, norm_post_w, w_in, conv_w, conv_b, dt_bias, a_log, d_skip,
           ssm_norm_w, lambda_q1, lambda_k1, lambda_q2, lambda_k2, subln_w, mem_norm_w, w_mem_kv,
           w_out):
    depth = w_in.shape[0]
    bp, lp_, _ = x_prompt.shape
    bs, ls, _ = x_sample.shape
    n_mem = mem_prompt.shape[1]
    past = cache_k.shape[2]
    xp = x_prompt.reshape(bp * lp_, D_MODEL)
    xs = x_sample.reshape(bs * ls, D_MODEL)
    mem2d = mem_prompt.reshape(bp * n_mem, D_MODEL)
    tiles_p = _tiles(bp, lp_)
    tiles_s = _tiles(bs, ls)
    conv0 = jnp.zeros((bp, CONV_W - 1, D_CONV), F32)
    h_zero = jnp.zeros((bp, N_SSM_HEADS, SSM_HEAD_DIM, D_STATE), F32)
    conv_all = cache_conv.reshape(depth * bs, CONV_W - 1, D_CONV)
    ssm_all = state_ssm.reshape(depth * bs, N_SSM_HEADS, SSM_HEAD_DIM, D_STATE)
    kv_all = (cache_k.reshape(-1, LANE), _to_flat_halves(cache_v))
    mem_k_all = _to_flat_halves(cache_mem_k)
    mem_v_all = _to_flat_halves(cache_mem_v)
    outs = [[] for _ in range(6)]
    kv_p = kv_s = None
    w_in_t = jnp.swapaxes(w_in, 1, 2)
    for l in range(depth):
        lp = _layer_params(l, norm_pre_w, norm_post_w, w_in_t, conv_w, conv_b, dt_bias, a_log, d_skip,
                           ssm_norm_w, lambda_q1, lambda_k1, lambda_q2, lambda_k2, subln_w,
                           mem_norm_w, w_mem_kv, w_out)
        lambda_init = 0.8 - 0.6 * math.exp(-0.3 * l)
        mk_p, mv_p = _mem_kv(mem2d, lp["mem_norm_w"], lp["w_mem_kv"], min(512, bp * n_mem))
        xp, c_p, h_p, kv_p = _mixer_layer(xp, bp, lp_, l, depth, kv_p, 0, 0, conv0, h_zero, None, 0, mk_p,
                                          mv_p, n_mem, lp, lambda_init, tiles_p)
        xs, c_s, h_s, kv_s = _mixer_layer(xs, bs, ls, l, depth, kv_s, l * bs, l * bs, conv_all, ssm_all,
                                          kv_all, past, mem_k_all, mem_v_all, n_mem, lp, lambda_init,
                                          tiles_s)
        for lst, val in zip(outs, (c_p, h_p, mk_p, mv_p, c_s, h_s)):
            lst.append(val)
    st = [jnp.stack(o) for o in outs]
    return (
        xp.reshape(bp, lp_, D_MODEL),
        xs.reshape(bs, ls, D_MODEL),
        st[0],
        st[1],
        kv_p[0].reshape(depth, bp, lp_, N_DIFF_HEADS, 2, DIFF_HEAD_DIM),
        _from_flat_halves(kv_p[1], (depth, bp, lp_, N_DIFF_HEADS, 2 * DIFF_HEAD_DIM)),
        _from_flat_halves(st[2], (depth, bp, n_mem, N_MEM_HEADS, MEM_HEAD_DIM)),
        _from_flat_halves(st[3], (depth, bp, n_mem, N_MEM_HEADS, MEM_HEAD_DIM)),
        st[4],
        st[5],
        kv_s[0].reshape(depth, bs, ls, N_DIFF_HEADS, 2, DIFF_HEAD_DIM),
        _from_flat_halves(kv_s[1], (depth, bs, ls, N_DIFF_HEADS, 2 * DIFF_HEAD_DIM)),
    )
```
